```python
import jax, jax.numpy as jnp
from jax import lax
import numpy as np

D_MODEL = 1024
BATCH = 4
SEQ = 4096
DEPTH = 4

CHUNK = 64
N_MIXERS = 2
EPS = 1e-6

SSD_EXPAND = 2
SSD_D_INNER = SSD_EXPAND * D_MODEL
SSD_HEAD_DIM = 64
SSD_N_HEADS = SSD_D_INNER // SSD_HEAD_DIM
SSD_N_GROUPS = 8
SSD_HEADS_PER_GROUP = SSD_N_HEADS // SSD_N_GROUPS
SSD_D_STATE = 128
SSD_CONV_W = 4
SSD_BC_DIM = SSD_N_GROUPS * SSD_D_STATE
SSD_CONV_DIM = SSD_D_INNER + 2 * SSD_BC_DIM
SSD_IN_DIM = SSD_D_INNER + SSD_CONV_DIM + SSD_N_HEADS

SC_WIDTH = D_MODEL
SC_CONV_W = 3

FFN_HIDDEN = 2816
FFN_CONV_W = 3

N_SSD_LAYERS = (DEPTH + 1) // 2
N_SC_LAYERS = DEPTH // 2

kernel_name = "hybrid_ssd_shortconv_convffn_sandwich"


def rms_norm(x, g):
    xf = x.astype(jnp.float32)
    y = xf * lax.rsqrt(jnp.mean(xf * xf, axis=-1, keepdims=True) + EPS)
    return (y * g.astype(jnp.float32)).astype(x.dtype)


def causal_dwconv(x, w, b=None):
    k = w.shape[0]
    length = x.shape[1]
    xp = jnp.pad(x, ((0, 0), (k - 1, 0), (0, 0)))
    out = xp[:, 0:length] * w[0]
    for t in range(1, k):
        out = out + xp[:, t:t + length] * w[t]
    if b is not None:
        out = out + b
    return out


def ssd_scan(x, dt, a_head, b_in, c_in):
    bsz, length, h, p = x.shape
    g, n = b_in.shape[2], b_in.shape[3]
    r = h // g
    nc = length // CHUNK
    xf = (x.astype(jnp.float32) * dt[..., None]).reshape(bsz, nc, CHUNK, g, r, p)
    a = (dt * a_head).reshape(bsz, nc, CHUNK, g, r)
    a = jnp.moveaxis(a, 2, -1)
    a_cs = jnp.cumsum(a, axis=-1)
    bc = b_in.astype(jnp.float32).reshape(bsz, nc, CHUNK, g, n)
    cc = c_in.astype(jnp.float32).reshape(bsz, nc, CHUNK, g, n)
    seg = a_cs[..., :, None] - a_cs[..., None, :]
    tri = jnp.tril(jnp.ones((CHUNK, CHUNK), dtype=bool))
    lmat = jnp.exp(jnp.where(tri, seg, -jnp.inf))
    cb = jnp.einsum("bclgn,bcsgn->bcgls", cc, bc)
    y_diag = jnp.einsum("bcgls,bcgrls,bcsgrp->bclgrp", cb, lmat, xf)
    decay_states = jnp.exp(a_cs[..., -1:] - a_cs)
    states = jnp.einsum("bclgn,bcgrl,bclgrp->bcgrpn", bc, decay_states, xf)
    chunk_decay = jnp.exp(a_cs[..., -1])

    def step(s, inp):
        st, dec = inp
        return s * dec[..., None, None] + st, s

    init = jnp.zeros((bsz, g, r, p, n), jnp.float32)
    _, prev = lax.scan(step, init, (jnp.moveaxis(states, 1, 0), jnp.moveaxis(chunk_decay, 1, 0)))
    prev = jnp.moveaxis(prev, 0, 1)
    y_off = jnp.einsum("bclgn,bcgrpn,bcgrl->bclgrp", cc, prev, jnp.exp(a_cs))
    return (y_diag + y_off).reshape(bsz, length, h, p)


def ssd_mixer(h, w_in, conv_w, conv_b, dt_bias, a_log, d_skip, norm_w, w_out):
    bsz, length, _ = h.shape
    zxbcdt = h @ w_in
    z = zxbcdt[..., :SSD_D_INNER]
    xbc = zxbcdt[..., SSD_D_INNER:SSD_D_INNER + SSD_CONV_DIM]
    dt_raw = zxbcdt[..., SSD_D_INNER + SSD_CONV_DIM:]
    xbc = jax.nn.silu(causal_dwconv(xbc, conv_w, conv_b))
    xs = xbc[..., :SSD_D_INNER].reshape(bsz, length, SSD_N_HEADS, SSD_HEAD_DIM)
    bs = xbc[..., SSD_D_INNER:SSD_D_INNER + SSD_BC_DIM].reshape(bsz, length, SSD_N_GROUPS, SSD_D_STATE)
    cs = xbc[..., SSD_D_INNER + SSD_BC_DIM:].reshape(bsz, length, SSD_N_GROUPS, SSD_D_STATE)
    dt = jax.nn.softplus(dt_raw.astype(jnp.float32) + dt_bias.astype(jnp.float32))
    a_head = -jnp.exp(a_log.astype(jnp.float32))
    y = ssd_scan(xs, dt, a_head, bs, cs)
    y = y + xs.astype(jnp.float32) * d_skip.astype(jnp.float32)[:, None]
    y = y.reshape(bsz, length, SSD_D_INNER).astype(h.dtype)
    y = rms_norm(y * jax.nn.silu(z), norm_w)
    return y @ w_out


def shortconv_mixer(h, w_in, conv_w, w_out):
    bcv = h @ w_in
    gb = bcv[..., :SC_WIDTH]
    gc = bcv[..., SC_WIDTH:2 * SC_WIDTH]
    v = bcv[..., 2 * SC_WIDTH:]
    u = causal_dwconv(gc * v, conv_w)
    return (gb * u) @ w_out


def conv_ffn(h, w_up, conv_w, conv_b, w_down):
    up = h @ w_up
    gate = causal_dwconv(up[..., :FFN_HIDDEN], conv_w, conv_b)
    val = up[..., FFN_HIDDEN:]
    return (jax.nn.silu(gate) * val) @ w_down


def setup_inputs(seed: int = 0) -> dict:
    key = jax.random.key(seed)
    ks = jax.random.split(key, 20)
    f32 = jnp.float32
    nrm = lambda k, shape, s: jax.random.normal(k, shape, f32) * s
    x = jax.random.normal(ks[0], (BATCH, SEQ, D_MODEL), f32)
    gains = lambda k: 1.0 + nrm(k, (DEPTH, D_MODEL), 0.02)
    dt0 = jnp.exp(jax.random.uniform(ks[8], (N_SSD_LAYERS, SSD_N_HEADS), f32,
                                     np.float32(np.log(1e-3)), np.float32(np.log(1e-1))))
    return {
        "x": x,
        "mix_pre_g": gains(ks[1]),
        "mix_post_g": gains(ks[2]),
        "ffn_pre_g": gains(ks[3]),
        "ffn_post_g": gains(ks[4]),
        "ssd_w_in": nrm(ks[5], (N_SSD_LAYERS, D_MODEL, SSD_IN_DIM), D_MODEL ** -0.5),
        "ssd_conv_w": nrm(ks[6], (N_SSD_LAYERS, SSD_CONV_W, SSD_CONV_DIM), SSD_CONV_W ** -0.5),
        "ssd_conv_b": nrm(ks[7], (N_SSD_LAYERS, SSD_CONV_DIM), 0.01),
        "ssd_dt_bias": dt0 + jnp.log(-jnp.expm1(-dt0)),
        "ssd_A_log": jnp.log(jax.random.uniform(ks[9], (N_SSD_LAYERS, SSD_N_HEADS), f32, 1.0, 16.0)),
        "ssd_D": 1.0 + nrm(ks[10], (N_SSD_LAYERS, SSD_N_HEADS), 0.1),
        "ssd_norm_w": 1.0 + nrm(ks[11], (N_SSD_LAYERS, SSD_D_INNER), 0.02),
        "ssd_w_out": nrm(ks[12], (N_SSD_LAYERS, SSD_D_INNER, D_MODEL), SSD_D_INNER ** -0.5),
        "sc_w_in": nrm(ks[13], (N_SC_LAYERS, D_MODEL, 3 * SC_WIDTH), D_MODEL ** -0.5),
        "sc_conv_w": nrm(ks[14], (N_SC_LAYERS, SC_CONV_W, SC_WIDTH), SC_CONV_W ** -0.5),
        "sc_w_out": nrm(ks[15], (N_SC_LAYERS, SC_WIDTH, D_MODEL), SC_WIDTH ** -0.5),
        "ffn_w_up": nrm(ks[16], (DEPTH, D_MODEL, 2 * FFN_HIDDEN), D_MODEL ** -0.5),
        "ffn_conv_w": nrm(ks[17], (DEPTH, FFN_CONV_W, FFN_HIDDEN), FFN_CONV_W ** -0.5),
        "ffn_conv_b": nrm(ks[18], (DEPTH, FFN_HIDDEN), 0.01),
        "ffn_w_down": nrm(ks[19], (DEPTH, FFN_HIDDEN, D_MODEL), FFN_HIDDEN ** -0.5),
    }


def reference(x, mix_pre_g, mix_post_g, ffn_pre_g, ffn_post_g,
              ssd_w_in, ssd_conv_w, ssd_conv_b, ssd_dt_bias, ssd_A_log, ssd_D,
              ssd_norm_w, ssd_w_out, sc_w_in, sc_conv_w, sc_w_out,
              ffn_w_up, ffn_conv_w, ffn_conv_b, ffn_w_down):
    for i in range(DEPTH):
        j = i // N_MIXERS
        h = rms_norm(x, mix_pre_g[i])
        if i % N_MIXERS == 0:
            m = ssd_mixer(h, ssd_w_in[j], ssd_conv_w[j], ssd_conv_b[j], ssd_dt_bias[j],
                          ssd_A_log[j], ssd_D[j], ssd_norm_w[j], ssd_w_out[j])
        else:
            m = shortconv_mixer(h, sc_w_in[j], sc_conv_w[j], sc_w_out[j])
        x = x + rms_norm(m, mix_post_g[i])
        f = conv_ffn(rms_norm(x, ffn_pre_g[i]), ffn_w_up[i], ffn_conv_w[i], ffn_conv_b[i], ffn_w_down[i])
        x = x + rms_norm(f, ffn_post_g[i])
    return x
```

```python
import functools

import jax
import jax.numpy as jnp
from jax import lax
from jax.experimental import pallas as pl
from jax.experimental.pallas import tpu as pltpu

EPS = 1e-6
D_MODEL = 1024

SSD_D_INNER = 2048
SSD_HEAD_DIM = 64
SSD_N_HEADS = 32
SSD_N_GROUPS = 8
SSD_HEADS_PER_GROUP = 4
SSD_D_STATE = 128
SSD_CONV_W = 4
SSD_BC_DIM = SSD_N_GROUPS * SSD_D_STATE
SSD_CONV_DIM = SSD_D_INNER + 2 * SSD_BC_DIM
GROUP_COLS = SSD_HEADS_PER_GROUP * SSD_HEAD_DIM

SC_WIDTH = 1024
SC_CONV_W = 3
FFN_HIDDEN = 2816
FFN_CONV_W = 3

LANES = 128
SUBLANES = 8
MXU_COLS = 256
VMEM_LIMIT_BYTES = 56 * 1024 * 1024

TIME_BLOCK = 256
SCAN_CHUNK = 128
NEG_BIG = -1e30

BF16 = jnp.bfloat16
F32 = jnp.float32


def _rms_norm(x, g):
    ms = jnp.mean(x * x, axis=-1, keepdims=True)
    return x * lax.rsqrt(ms + EPS) * g


def _silu(x):
    return x * (1.0 / (1.0 + jnp.exp(-x)))


def _softplus(x):
    return jnp.maximum(x, 0.0) + jnp.log1p(jnp.exp(-jnp.abs(x)))


def _shift_rows(cur, prev_tail, k):
    rolled = pltpu.roll(cur, k, axis=0)
    tail = pltpu.roll(prev_tail, k, axis=0)
    row = lax.broadcasted_iota(jnp.int32, tail.shape, 0)
    first = jnp.where(row < k, tail, rolled[:SUBLANES])
    return jnp.concatenate([first, rolled[SUBLANES:]], axis=0)


def _causal_conv(cur, prev_tail, w, b=None):
    k = w.shape[0]
    out = cur * w[k - 1:k]
    for s in range(1, k):
        out = out + _shift_rows(cur, prev_tail, s) * w[k - 1 - s:k - s]
    if b is not None:
        out = out + b
    return out


def _dot(a, b):
    return jnp.dot(a, b, preferred_element_type=F32)


def _ffn_kernel(x_ref, gpre_ref, wup_ref, cw_ref, cb_ref, wdown_ref, gpost_ref,
                o_ref, tail_ref, hid_ref):
    t = pl.program_id(1)

    @pl.when(t == 0)
    def _():
        tail_ref[...] = jnp.zeros_like(tail_ref)

    x = x_ref[0]
    hb = _rms_norm(x, gpre_ref[...]).astype(BF16)
    rows = x.shape[0]
    for c in range(FFN_HIDDEN // MXU_COLS):
        cols = pl.ds(c * MXU_COLS, MXU_COLS)
        gate = _dot(hb, wup_ref[:, cols])
        val = _dot(hb, wup_ref[:, pl.ds(FFN_HIDDEN + c * MXU_COLS, MXU_COLS)])
        conv = _causal_conv(gate, tail_ref[:, cols], cw_ref[:, cols], cb_ref[:, cols])
        tail_ref[:, cols] = gate[rows - SUBLANES:]
        hid_ref[:, cols] = (_silu(conv) * val).astype(BF16)
    f = _dot(hid_ref[...], wdown_ref[...])
    o_ref[0] = x + _rms_norm(f, gpost_ref[...])


def _const_spec(shape):
    return pl.BlockSpec(shape, lambda b, t: (0,) * len(shape), pipeline_mode=pl.Buffered(1))


def _x_spec(tl):
    return pl.BlockSpec((1, tl, D_MODEL), lambda b, t: (b, t, 0))


def _compiler_params():
    return pltpu.CompilerParams(
        dimension_semantics=("arbitrary", "arbitrary"),
        vmem_limit_bytes=VMEM_LIMIT_BYTES,
    )


def _ffn_layer(x, gpre, wup, cw, cb, wdown, gpost):
    bsz, length, _ = x.shape
    tl = TIME_BLOCK
    return pl.pallas_call(
        _ffn_kernel,
        name="conv_ffn",
        grid=(bsz, length // tl),
        in_specs=[
            _x_spec(tl),
            _const_spec((1, D_MODEL)),
            _const_spec((D_MODEL, 2 * FFN_HIDDEN)),
            _const_spec((FFN_CONV_W, FFN_HIDDEN)),
            _const_spec((1, FFN_HIDDEN)),
            _const_spec((FFN_HIDDEN, D_MODEL)),
            _const_spec((1, D_MODEL)),
        ],
        out_specs=_x_spec(tl),
        out_shape=jax.ShapeDtypeStruct(x.shape, x.dtype),
        scratch_shapes=[
            pltpu.VMEM((SUBLANES, FFN_HIDDEN), F32),
            pltpu.VMEM((tl, FFN_HIDDEN), BF16),
        ],
        compiler_params=_compiler_params(),
    )(x, gpre, wup, cw, cb, wdown, gpost)


def _sc_kernel(x_ref, gpre_ref, win_ref, cw_ref, wout_ref, gpost_ref, o_ref, tail_ref):
    t = pl.program_id(1)

    @pl.when(t == 0)
    def _():
        tail_ref[...] = jnp.zeros_like(tail_ref)

    x = x_ref[0]
    hb = _rms_norm(x, gpre_ref[...]).astype(BF16)
    rows = x.shape[0]
    gb = _dot(hb, win_ref[:, pl.ds(0, SC_WIDTH)])
    gc = _dot(hb, win_ref[:, pl.ds(SC_WIDTH, SC_WIDTH)])
    v = _dot(hb, win_ref[:, pl.ds(2 * SC_WIDTH, SC_WIDTH)])
    gcv = gc * v
    u = _causal_conv(gcv, tail_ref[...], cw_ref[...])
    tail_ref[...] = gcv[rows - SUBLANES:]
    m = _dot((gb * u).astype(BF16), wout_ref[...])
    o_ref[0] = x + _rms_norm(m, gpost_ref[...])


def _sc_layer(x, gpre, win, cw, wout, gpost):
    bsz, length, _ = x.shape
    tl = TIME_BLOCK
    return pl.pallas_call(
        _sc_kernel,
        name="shortconv_mixer",
        grid=(bsz, length // tl),
        in_specs=[
            _x_spec(tl),
            _const_spec((1, D_MODEL)),
            _const_spec((D_MODEL, 3 * SC_WIDTH)),
            _const_spec((SC_CONV_W, SC_WIDTH)),
            _const_spec((SC_WIDTH, D_MODEL)),
            _const_spec((1, D_MODEL)),
        ],
        out_specs=_x_spec(tl),
        out_shape=jax.ShapeDtypeStruct(x.shape, x.dtype),
        scratch_shapes=[pltpu.VMEM((SUBLANES, SC_WIDTH), F32)],
        compiler_params=_compiler_params(),
    )(x, gpre, win, cw, wout, gpost)


def _expand_heads(a, first_head, n_heads):
    lane = lax.broadcasted_iota(jnp.int32, (a.shape[0], LANES), 1)
    low = lane < SSD_HEAD_DIM
    pieces = []
    for h in range(first_head, first_head + n_heads, 2):
        pieces.append(jnp.where(low, a[:, h:h + 1], a[:, h + 1:h + 2]))
    return jnp.concatenate(pieces, axis=1)


def _ssd_kernel(x_ref, gpre_ref, wz_ref, wxbc_ref, wdt_ref, cw_ref, cb_ref, dtb_ref, alog_ref,
                dskip_ref, normw_ref, wout_ref, gpost_ref,
                o_ref, tail_ref, state_ref, xbc_ref, y_ref):
    t = pl.program_id(1)

    @pl.when(t == 0)
    def _():
        tail_ref[...] = jnp.zeros_like(tail_ref)
        state_ref[...] = jnp.zeros_like(state_ref)

    x = x_ref[0]
    rows = x.shape[0]
    hb = _rms_norm(x, gpre_ref[...]).astype(BF16)

    for c in range(SSD_CONV_DIM // MXU_COLS):
        cols = pl.ds(c * MXU_COLS, MXU_COLS)
        raw = _dot(hb, wxbc_ref[:, cols])
        conv = _causal_conv(raw, tail_ref[:, cols], cw_ref[:, cols], cb_ref[:, cols])
        tail_ref[:, cols] = raw[rows - SUBLANES:]
        xbc_ref[:, cols] = _silu(conv)

    dt = _softplus(_dot(hb, wdt_ref[...]) + dtb_ref[...])
    a = dt * -jnp.exp(alog_ref[...])

    tc = SCAN_CHUNK
    ri = lax.broadcasted_iota(jnp.int32, (tc, tc), 0)
    ci = lax.broadcasted_iota(jnp.int32, (tc, tc), 1)
    causal = ri >= ci
    tril = causal.astype(F32)
    gcol = lax.broadcasted_iota(jnp.int32, (tc, GROUP_COLS), 1) // SSD_HEAD_DIM

    for k in range(rows // tc):
        r0 = k * tc
        dt_c = dt[r0:r0 + tc]
        acs = jnp.dot(tril, a[r0:r0 + tc], precision=lax.Precision.HIGHEST,
                      preferred_element_type=F32)
        acs_t = acs.T
        dt_t = dt_c.T
        total = acs[tc - 1:tc]
        e_in = jnp.exp(acs)
        w_out = jnp.exp(total - acs) * dt_c
        e_tot = jnp.exp(total)

        for g in range(SSD_N_GROUPS):
            h0 = g * SSD_HEADS_PER_GROUP
            xcols = pl.ds(g * GROUP_COLS, GROUP_COLS)
            xg = xbc_ref[pl.ds(r0, tc), xcols]
            bg = xbc_ref[pl.ds(r0, tc), pl.ds(SSD_D_INNER + g * SSD_D_STATE, SSD_D_STATE)].astype(BF16)
            cg = xbc_ref[pl.ds(r0, tc),
                         pl.ds(SSD_D_INNER + SSD_BC_DIM + g * SSD_D_STATE, SSD_D_STATE)].astype(BF16)
            cb_mat = lax.dot_general(cg, bg, (((1,), (1,)), ((), ())),
                                     preferred_element_type=F32)
            m_parts = []
            x_parts = []
            for j in range(SSD_HEADS_PER_GROUP):
                h = h0 + j
                seg = acs[:, h:h + 1] - acs_t[h:h + 1, :]
                lmat = jnp.exp(jnp.where(causal, seg, NEG_BIG))
                m_parts.append((cb_mat * lmat * dt_t[h:h + 1, :]).astype(BF16))
                x_parts.append(jnp.where(gcol == j, xg, 0.0).astype(BF16))
            m_cat = jnp.concatenate(m_parts, axis=1)
            x_bd = jnp.concatenate(x_parts, axis=0)
            y_diag = _dot(m_cat, x_bd)

            e_in_g = _expand_heads(e_in, h0, SSD_HEADS_PER_GROUP)
            w_out_g = _expand_heads(w_out, h0, SSD_HEADS_PER_GROUP)
            e_tot_g = _expand_heads(e_tot, h0, SSD_HEADS_PER_GROUP)
            state = state_ref[g]
            y_off = _dot(cg, state.astype(BF16)) * e_in_g
            y_ref[pl.ds(r0, tc), xcols] = y_diag + y_off + xg * dskip_ref[:, xcols]
            upd = lax.dot_general(bg, (xg * w_out_g).astype(BF16), (((0,), (0,)), ((), ())),
                                  preferred_element_type=F32)
            state_ref[g] = state * e_tot_g + upd

    z = _dot(hb, wz_ref[...])
    yn = _rms_norm(y_ref[...] * _silu(z), normw_ref[...]).astype(BF16)
    m = _dot(yn, wout_ref[...])
    o_ref[0] = x + _rms_norm(m, gpost_ref[...])


def _ssd_layer(x, gpre, wz, wxbc, wdt, cw, cb, dtb, alog, dskip, normw, wout, gpost):
    bsz, length, _ = x.shape
    tl = TIME_BLOCK
    return pl.pallas_call(
        _ssd_kernel,
        name="ssd_mixer",
        grid=(bsz, length // tl),
        in_specs=[
            _x_spec(tl),
            _const_spec((1, D_MODEL)),
            _const_spec((D_MODEL, SSD_D_INNER)),
            _const_spec((D_MODEL, SSD_CONV_DIM)),
            _const_spec((D_MODEL, LANES)),
            _const_spec((SSD_CONV_W, SSD_CONV_DIM)),
            _const_spec((1, SSD_CONV_DIM)),
            _const_spec((1, LANES)),
            _const_spec((1, LANES)),
            _const_spec((1, SSD_D_INNER)),
            _const_spec((1, SSD_D_INNER)),
            _const_spec((SSD_D_INNER, D_MODEL)),
            _const_spec((1, D_MODEL)),
        ],
        out_specs=_x_spec(tl),
        out_shape=jax.ShapeDtypeStruct(x.shape, x.dtype),
        scratch_shapes=[
            pltpu.VMEM((SUBLANES, SSD_CONV_DIM), F32),
            pltpu.VMEM((SSD_N_GROUPS, SSD_D_STATE, GROUP_COLS), F32),
            pltpu.VMEM((tl, SSD_CONV_DIM), F32),
            pltpu.VMEM((tl, SSD_D_INNER), F32),
        ],
        compiler_params=_compiler_params(),
    )(x, gpre, wz, wxbc, wdt, cw, cb, dtb, alog, dskip, normw, wout, gpost)


def _pad_lanes(v):
    return jnp.pad(v, ((0, 0), (0, LANES - v.shape[-1])))


def kernel(x, mix_pre_g, mix_post_g, ffn_pre_g, ffn_post_g, ssd_w_in, ssd_conv_w, ssd_conv_b,
           ssd_dt_bias, ssd_A_log, ssd_D, ssd_norm_w, ssd_w_out, sc_w_in, sc_conv_w, sc_w_out,
           ffn_w_up, ffn_conv_w, ffn_conv_b, ffn_w_down):
    depth = mix_pre_g.shape[0]
    row = lambda v: v.reshape(1, -1)
    for i in range(depth):
        j = i // 2
        if i % 2 == 0:
            w_in = ssd_w_in[j]
            wz = w_in[:, :SSD_D_INNER].astype(BF16)
            wxbc = w_in[:, SSD_D_INNER:SSD_D_INNER + SSD_CONV_DIM].astype(BF16)
            wdt = jnp.pad(w_in[:, SSD_D_INNER + SSD_CONV_DIM:],
                          ((0, 0), (0, LANES - SSD_N_HEADS))).astype(BF16)
            dskip = jnp.repeat(ssd_D[j], SSD_HEAD_DIM).reshape(1, -1)
            x = _ssd_layer(x, row(mix_pre_g[i]), wz, wxbc, wdt, ssd_conv_w[j], row(ssd_conv_b[j]),
                           _pad_lanes(row(ssd_dt_bias[j])), _pad_lanes(row(ssd_A_log[j])), dskip,
                           row(ssd_norm_w[j]),
                           ssd_w_out[j].astype(BF16), row(mix_post_g[i]))
        else:
            x = _sc_layer(x, row(mix_pre_g[i]), sc_w_in[j].astype(BF16), sc_conv_w[j],
                          sc_w_out[j].astype(BF16), row(mix_post_g[i]))
        x = _ffn_layer(x, row(ffn_pre_g[i]), ffn_w_up[i].astype(BF16), ffn_conv_w[i],
                       row(ffn_conv_b[i]), ffn_w_down[i].astype(BF16), row(ffn_post_g[i]))
    return x
```

```python
import jax
import jax.numpy as jnp
from jax import lax
from jax.experimental import pallas as pl
from jax.experimental.pallas import tpu as pltpu

EPS = 1e-6
D_MODEL = 1024

SSD_D_INNER = 2048
SSD_HEAD_DIM = 64
SSD_N_HEADS = 32
SSD_N_GROUPS = 8
SSD_HEADS_PER_GROUP = 4
SSD_D_STATE = 128
SSD_CONV_W = 4
SSD_BC_DIM = SSD_N_GROUPS * SSD_D_STATE
SSD_CONV_DIM = SSD_D_INNER + 2 * SSD_BC_DIM
GROUP_COLS = SSD_HEADS_PER_GROUP * SSD_HEAD_DIM

SC_WIDTH = 1024
SC_CONV_W = 3
FFN_HIDDEN = 2816
FFN_CONV_W = 3

LANES = 128
SUBLANES = 8
MXU_COLS = 256
VMEM_LIMIT_BYTES = 56 * 1024 * 1024

ROW_GROUP = SUBLANES * SUBLANES
TAIL_VREG_ROWS = 3
TAIL_ROWS = TAIL_VREG_ROWS * SUBLANES

TIME_BLOCK = 512
SUB_ROWS = 256
SCAN_CHUNK = 128
NEG_BIG = -1e30

BF16 = jnp.bfloat16
F32 = jnp.float32


def _rms_norm(x, g):
    ms = jnp.mean(x * x, axis=-1, keepdims=True)
    return x * lax.rsqrt(ms + EPS) * g


def _silu(x):
    hx = 0.5 * x
    return hx * jnp.tanh(hx) + hx


def _softplus(x):
    return jnp.maximum(x, 0.0) + jnp.log1p(jnp.exp(-jnp.abs(x)))


def _dot(a, b):
    return jnp.dot(a, b, preferred_element_type=F32)


def _load_permuted(x_ref, stage_ref, r0, rows):
    d = x_ref.shape[2]
    for j in range(d // LANES):
        stage_ref[j, pl.ds(r0, rows), :] = x_ref[0, pl.ds(r0, rows), j * LANES:(j + 1) * LANES]
    cols = []
    for j in range(d // LANES):
        tiles = [stage_ref[j, pl.ds(r0 + g * ROW_GROUP + r, SUBLANES, stride=SUBLANES), :]
                 for g in range(rows // ROW_GROUP) for r in range(SUBLANES)]
        cols.append(jnp.concatenate(tiles, axis=0))
    return jnp.concatenate(cols, axis=1)


def _store_unpermuted(o_ref, stage_ref, r0, val):
    rows, d = val.shape
    for j in range(d // LANES):
        for g in range(rows // ROW_GROUP):
            for r in range(SUBLANES):
                p0 = g * ROW_GROUP + r * SUBLANES
                stage_ref[j, pl.ds(r0 + g * ROW_GROUP + r, SUBLANES, stride=SUBLANES), :] = (
                    val[p0:p0 + SUBLANES, j * LANES:(j + 1) * LANES])
    for j in range(d // LANES):
        o_ref[0, pl.ds(r0, rows), j * LANES:(j + 1) * LANES] = stage_ref[j, pl.ds(r0, rows), :]


def _time_of_position(p):
    return ((p >> 6) << 6) + ((p & 7) << 3) + ((p >> 3) & 7)


def _shifted(cur, prev_tail, k_max):
    rows = cur.shape[0]
    sub = lax.broadcasted_iota(jnp.int32, (SUBLANES, cur.shape[1]), 0)
    last = sub == SUBLANES - 1
    first_q = SUBLANES - TAIL_VREG_ROWS
    out = [[] for _ in range(k_max)]
    for g in range(rows // ROW_GROUP):
        base = g * ROW_GROUP
        prev = prev_tail if g == 0 else cur[base - TAIL_ROWS:base]
        rolled = {}
        for q in range(SUBLANES - k_max, SUBLANES):
            cur_q = cur[base + q * SUBLANES:base + (q + 1) * SUBLANES]
            prev_q = prev[(q - first_q) * SUBLANES:(q - first_q + 1) * SUBLANES]
            rolled[q] = pltpu.roll(jnp.where(last, prev_q, cur_q), 1, axis=0)
        for k in range(1, k_max + 1):
            pieces = [rolled[q] for q in range(SUBLANES - k, SUBLANES)]
            pieces.append(cur[base:base + (SUBLANES - k) * SUBLANES])
            out[k - 1].append(jnp.concatenate(pieces, axis=0))
    return [jnp.concatenate(o, axis=0) for o in out]


def _causal_conv(cur, prev_tail, w, b=None):
    k = w.shape[0]
    out = cur * w[k - 1:k]
    for s, shifted in enumerate(_shifted(cur, prev_tail, k - 1), start=1):
        out = out + shifted * w[k - 1 - s:k - s]
    if b is not None:
        out = out + b
    return out


def _pad_weight_cols(w):
    return jnp.pad(w, ((0, 0), (0, LANES)))


def _const_spec(shape):
    return pl.BlockSpec(shape, lambda b, t: (0,) * len(shape), pipeline_mode=pl.Buffered(1))


def _x_spec():
    return pl.BlockSpec((1, TIME_BLOCK, D_MODEL), lambda b, t: (b, t, 0))


def _stage_scratch():
    return pltpu.VMEM((D_MODEL // LANES, TIME_BLOCK, LANES), F32)


def _compiler_params():
    return pltpu.CompilerParams(
        dimension_semantics=("arbitrary", "arbitrary"),
        vmem_limit_bytes=VMEM_LIMIT_BYTES,
    )


def _ffn_kernel(x_ref, gpre_ref, wup_ref, cw_ref, cb_ref, wdown_ref, gpost_ref,
                o_ref, tail_ref, hid_ref, xstage_ref, ostage_ref):
    t = pl.program_id(1)

    @pl.when(t == 0)
    def _():
        tail_ref[...] = jnp.zeros_like(tail_ref)

    n_chunks = FFN_HIDDEN // MXU_COLS
    tails = [tail_ref[:, pl.ds(c * MXU_COLS, MXU_COLS)] for c in range(n_chunks)]
    for sb in range(TIME_BLOCK // SUB_ROWS):
        r0 = sb * SUB_ROWS
        x = _load_permuted(x_ref, xstage_ref, r0, SUB_ROWS)
        hb = _rms_norm(x, gpre_ref[...]).astype(BF16)
        for c in range(n_chunks):
            cols = pl.ds(c * MXU_COLS, MXU_COLS)
            gate = _dot(hb, wup_ref[:, cols])
            val = _dot(hb, wup_ref[:, pl.ds(FFN_HIDDEN + c * MXU_COLS, MXU_COLS)])
            conv = _causal_conv(gate, tails[c], cw_ref[:, cols], cb_ref[:, cols])
            tails[c] = gate[SUB_ROWS - TAIL_ROWS:]
            hid_ref[pl.ds(r0, SUB_ROWS), cols] = (_silu(conv) * val).astype(BF16)
        f = _dot(hid_ref[pl.ds(r0, SUB_ROWS), :], wdown_ref[:, pl.ds(0, D_MODEL)])
        _store_unpermuted(o_ref, ostage_ref, r0, x + _rms_norm(f, gpost_ref[...]))
    for c in range(n_chunks):
        tail_ref[:, pl.ds(c * MXU_COLS, MXU_COLS)] = tails[c]


def _ffn_layer(x, gpre, wup, cw, cb, wdown, gpost):
    bsz, length, _ = x.shape
    return pl.pallas_call(
        _ffn_kernel,
        name="conv_ffn",
        grid=(bsz, length // TIME_BLOCK),
        in_specs=[
            _x_spec(),
            _const_spec(gpre.shape),
            _const_spec(wup.shape),
            _const_spec(cw.shape),
            _const_spec(cb.shape),
            _const_spec(wdown.shape),
            _const_spec(gpost.shape),
        ],
        out_specs=_x_spec(),
        out_shape=jax.ShapeDtypeStruct(x.shape, x.dtype),
        scratch_shapes=[
            pltpu.VMEM((TAIL_ROWS, FFN_HIDDEN), F32),
            pltpu.VMEM((TIME_BLOCK, FFN_HIDDEN), BF16),
            _stage_scratch(),
            _stage_scratch(),
        ],
        compiler_params=_compiler_params(),
    )(x, gpre, wup, cw, cb, wdown, gpost)


def _sc_kernel(x_ref, gpre_ref, win_ref, cw_ref, wout_ref, gpost_ref, o_ref,
               tail_ref, xstage_ref, ostage_ref):
    t = pl.program_id(1)

    @pl.when(t == 0)
    def _():
        tail_ref[...] = jnp.zeros_like(tail_ref)

    n_chunks = SC_WIDTH // MXU_COLS
    tails = [tail_ref[:, pl.ds(c * MXU_COLS, MXU_COLS)] for c in range(n_chunks)]
    for sb in range(TIME_BLOCK // SUB_ROWS):
        r0 = sb * SUB_ROWS
        x = _load_permuted(x_ref, xstage_ref, r0, SUB_ROWS)
        hb = _rms_norm(x, gpre_ref[...]).astype(BF16)
        parts = []
        for c in range(n_chunks):
            cols = pl.ds(c * MXU_COLS, MXU_COLS)
            gb = _dot(hb, win_ref[:, cols])
            gc = _dot(hb, win_ref[:, pl.ds(SC_WIDTH + c * MXU_COLS, MXU_COLS)])
            v = _dot(hb, win_ref[:, pl.ds(2 * SC_WIDTH + c * MXU_COLS, MXU_COLS)])
            gcv = gc * v
            u = _causal_conv(gcv, tails[c], cw_ref[:, cols])
            tails[c] = gcv[SUB_ROWS - TAIL_ROWS:]
            parts.append((gb * u).astype(BF16))
        m = _dot(jnp.concatenate(parts, axis=1), wout_ref[:, pl.ds(0, D_MODEL)])
        _store_unpermuted(o_ref, ostage_ref, r0, x + _rms_norm(m, gpost_ref[...]))
    for c in range(n_chunks):
        tail_ref[:, pl.ds(c * MXU_COLS, MXU_COLS)] = tails[c]


def _sc_layer(x, gpre, win, cw, wout, gpost):
    bsz, length, _ = x.shape
    return pl.pallas_call(
        _sc_kernel,
        name="shortconv_mixer",
        grid=(bsz, length // TIME_BLOCK),
        in_specs=[
            _x_spec(),
            _const_spec(gpre.shape),
            _const_spec(win.shape),
            _const_spec(cw.shape),
            _const_spec(wout.shape),
            _const_spec(gpost.shape),
        ],
        out_specs=_x_spec(),
        out_shape=jax.ShapeDtypeStruct(x.shape, x.dtype),
        scratch_shapes=[
            pltpu.VMEM((TAIL_ROWS, SC_WIDTH), F32),
            _stage_scratch(),
            _stage_scratch(),
        ],
        compiler_params=_compiler_params(),
    )(x, gpre, win, cw, wout, gpost)


def _expand_heads(a, first_head, n_heads):
    lane = lax.broadcasted_iota(jnp.int32, (a.shape[0], LANES), 1)
    low = lane < SSD_HEAD_DIM
    pieces = []
    for h in range(first_head, first_head + n_heads, 2):
        pieces.append(jnp.where(low, a[:, h:h + 1], a[:, h + 1:h + 2]))
    return jnp.concatenate(pieces, axis=1)


def _ssd_scan_chunk(xbc_ref, y_ref, state_ref, dskip_ref, r0, dt_c, a_c, causal, tril, gcol):
    tc = SCAN_CHUNK
    acs = jnp.dot(tril, a_c, precision=lax.Precision.HIGHEST,
                  preferred_element_type=F32)
    src_t = (acs - jnp.log(dt_c)).T
    total = acs[tc - 1:tc]
    e_in = jnp.exp(acs)
    w_out = jnp.exp(total - acs) * dt_c
    e_tot = jnp.exp(total)

    for g in range(SSD_N_GROUPS):
        h0 = g * SSD_HEADS_PER_GROUP
        xcols = pl.ds(g * GROUP_COLS, GROUP_COLS)
        xg = xbc_ref[pl.ds(r0, tc), xcols]
        bg = xbc_ref[pl.ds(r0, tc), pl.ds(SSD_D_INNER + g * SSD_D_STATE, SSD_D_STATE)].astype(BF16)
        cg = xbc_ref[pl.ds(r0, tc),
                     pl.ds(SSD_D_INNER + SSD_BC_DIM + g * SSD_D_STATE, SSD_D_STATE)].astype(BF16)
        cb_mat = lax.dot_general(cg, bg, (((1,), (1,)), ((), ())),
                                 preferred_element_type=F32)
        m_parts = []
        x_parts = []
        for j in range(SSD_HEADS_PER_GROUP):
            h = h0 + j
            seg = acs[:, h:h + 1] - src_t[h:h + 1, :]
            lmat = jnp.exp(jnp.where(causal, seg, NEG_BIG))
            m_parts.append((cb_mat * lmat).astype(BF16))
            x_parts.append(jnp.where(gcol == j, xg, 0.0).astype(BF16))
        m_cat = jnp.concatenate(m_parts, axis=1)
        x_bd = jnp.concatenate(x_parts, axis=0)
        y_diag = _dot(m_cat, x_bd)

        e_in_g = _expand_heads(e_in, h0, SSD_HEADS_PER_GROUP)
        w_out_g = _expand_heads(w_out, h0, SSD_HEADS_PER_GROUP)
        e_tot_g = _expand_heads(e_tot, h0, SSD_HEADS_PER_GROUP)
        state = state_ref[g]
        y_off = _dot(cg, state.astype(BF16)) * e_in_g
        y_ref[pl.ds(r0, tc), xcols] = y_diag + y_off + xg * dskip_ref[:, xcols]
        upd = lax.dot_general(bg, (xg * w_out_g).astype(BF16), (((0,), (0,)), ((), ())),
                              preferred_element_type=F32)
        state_ref[g] = state * e_tot_g + upd
        yield


_DONE = object()


def _run(task):
    for _ in task:
        pass


def _chain(*tasks):
    for task in tasks:
        yield from task


def _interleave(primary, secondary):
    primary_live = secondary_live = True
    while primary_live or secondary_live:
        if primary_live:
            primary_live = next(primary, _DONE) is not _DONE
        if secondary_live:
            secondary_live = next(secondary, _DONE) is not _DONE


def _ssd_kernel(x_ref, gpre_ref, wz_ref, wxbc_ref, wdt_ref, cw_ref, cb_ref, dtb_ref, alog_ref,
                dskip_ref, normw_ref, wout_ref, gpost_ref,
                o_ref, tail_ref, state_ref, xbc_ref, y_ref, xstage_ref, ostage_ref):
    t = pl.program_id(1)

    @pl.when(t == 0)
    def _():
        tail_ref[...] = jnp.zeros_like(tail_ref)
        state_ref[...] = jnp.zeros_like(state_ref)

    tc = SCAN_CHUNK
    time_l = _time_of_position(lax.broadcasted_iota(jnp.int32, (tc, tc), 0))
    time_s = _time_of_position(lax.broadcasted_iota(jnp.int32, (tc, tc), 1))
    causal = time_l >= time_s
    tril = causal.astype(F32)
    gcol = lax.broadcasted_iota(jnp.int32, (tc, GROUP_COLS), 1) // SSD_HEAD_DIM

    n_chunks = SSD_CONV_DIM // MXU_COLS
    n_sub = TIME_BLOCK // SUB_ROWS
    tails = [tail_ref[:, pl.ds(c * MXU_COLS, MXU_COLS)] for c in range(n_chunks)]
    ctx = [dict() for _ in range(n_sub)]

    def in_proj(sb):
        r0 = sb * SUB_ROWS
        x = _load_permuted(x_ref, xstage_ref, r0, SUB_ROWS)
        hb = _rms_norm(x, gpre_ref[...]).astype(BF16)
        ctx[sb].update(x=x, hb=hb)
        for c in range(n_chunks):
            cols = pl.ds(c * MXU_COLS, MXU_COLS)
            raw = _dot(hb, wxbc_ref[:, cols])
            conv = _causal_conv(raw, tails[c], cw_ref[:, cols], cb_ref[:, cols])
            tails[c] = raw[SUB_ROWS - TAIL_ROWS:]
            xbc_ref[pl.ds(r0, SUB_ROWS), cols] = _silu(conv)
            yield
        dt = _softplus(_dot(hb, wdt_ref[...]) + dtb_ref[...])
        ctx[sb].update(dt=dt, a=dt * -jnp.exp(alog_ref[...]))
        yield

    def scan(sb):
        dt, a = ctx[sb]["dt"], ctx[sb]["a"]
        for k in range(SUB_ROWS // tc):
            yield from _ssd_scan_chunk(xbc_ref, y_ref, state_ref, dskip_ref, sb * SUB_ROWS + k * tc,
                                       dt[k * tc:(k + 1) * tc], a[k * tc:(k + 1) * tc],
                                       causal, tril, gcol)

    def out_proj(sb):
        r0 = sb * SUB_ROWS
        sub = pl.ds(r0, SUB_ROWS)
        x, hb = ctx[sb]["x"], ctx[sb]["hb"]
        sumsq = jnp.zeros((SUB_ROWS, 1), F32)
        for c in range(SSD_D_INNER // MXU_COLS):
            cols = pl.ds(c * MXU_COLS, MXU_COLS)
            gated = y_ref[sub, cols] * _silu(_dot(hb, wz_ref[:, cols]))
            sumsq = sumsq + jnp.sum(gated * gated, axis=-1, keepdims=True)
            y_ref[sub, cols] = gated
            yield
        scale = lax.rsqrt(sumsq * (1.0 / SSD_D_INNER) + EPS)
        yn = (y_ref[sub, :] * scale * normw_ref[...]).astype(BF16)
        parts = []
        for c in range(D_MODEL // MXU_COLS):
            parts.append(_dot(yn, wout_ref[:, pl.ds(c * MXU_COLS, MXU_COLS)]))
            yield
        m = jnp.concatenate(parts, axis=1)
        _store_unpermuted(o_ref, ostage_ref, r0, x + _rms_norm(m, gpost_ref[...]))

    _run(in_proj(0))
    for sb in range(n_sub):
        fill = []
        if sb > 0:
            fill.append(out_proj(sb - 1))
        if sb + 1 < n_sub:
            fill.append(in_proj(sb + 1))
        _interleave(scan(sb), _chain(*fill))
    _run(out_proj(n_sub - 1))
    for c in range(n_chunks):
        tail_ref[:, pl.ds(c * MXU_COLS, MXU_COLS)] = tails[c]


def _ssd_layer(x, gpre, wz, wxbc, wdt, cw, cb, dtb, alog, dskip, normw, wout, gpost):
    bsz, length, _ = x.shape
    consts = (gpre, wz, wxbc, wdt, cw, cb, dtb, alog, dskip, normw, wout, gpost)
    return pl.pallas_call(
        _ssd_kernel,
        name="ssd_mixer",
        grid=(bsz, length // TIME_BLOCK),
        in_specs=[_x_spec()] + [_const_spec(c.shape) for c in consts],
        out_specs=_x_spec(),
        out_shape=jax.ShapeDtypeStruct(x.shape, x.dtype),
        scratch_shapes=[
            pltpu.VMEM((TAIL_ROWS, SSD_CONV_DIM), F32),
            pltpu.VMEM((SSD_N_GROUPS, SSD_D_STATE, GROUP_COLS), F32),
            pltpu.VMEM((TIME_BLOCK, SSD_CONV_DIM), F32),
            pltpu.VMEM((TIME_BLOCK, SSD_D_INNER), F32),
            _stage_scratch(),
            _stage_scratch(),
        ],
        compiler_params=_compiler_params(),
    )(x, *consts)


def _pad_lanes(v):
    return jnp.pad(v, ((0, 0), (0, LANES - v.shape[-1])))


def kernel(x, mix_pre_g, mix_post_g, ffn_pre_g, ffn_post_g, ssd_w_in, ssd_conv_w, ssd_conv_b,
           ssd_dt_bias, ssd_A_log, ssd_D, ssd_norm_w, ssd_w_out, sc_w_in, sc_conv_w, sc_w_out,
           ffn_w_up, ffn_conv_w, ffn_conv_b, ffn_w_down):
    depth = mix_pre_g.shape[0]
    row = lambda v: v.reshape(1, -1)
    for i in range(depth):
        j = i // 2
        if i % 2 == 0:
            w_in = ssd_w_in[j]
            wz = _pad_weight_cols(w_in[:, :SSD_D_INNER].astype(BF16))
            wxbc = _pad_weight_cols(w_in[:, SSD_D_INNER:SSD_D_INNER + SSD_CONV_DIM].astype(BF16))
            wdt = _pad_lanes(w_in[:, SSD_D_INNER + SSD_CONV_DIM:]).astype(BF16)
            dskip = jnp.repeat(ssd_D[j], SSD_HEAD_DIM).reshape(1, -1)
            x = _ssd_layer(x, row(mix_pre_g[i]), wz, wxbc, wdt, ssd_conv_w[j], row(ssd_conv_b[j]),
                           _pad_lanes(row(ssd_dt_bias[j])), _pad_lanes(row(ssd_A_log[j])), dskip,
                           row(ssd_norm_w[j]), _pad_weight_cols(ssd_w_out[j].astype(BF16)),
                           row(mix_post_g[i]))
        else:
            x = _sc_layer(x, row(mix_pre_g[i]), _pad_weight_cols(sc_w_in[j].astype(BF16)),
                          sc_conv_w[j], _pad_weight_cols(sc_w_out[j].astype(BF16)),
                          row(mix_post_g[i]))
        x = _ffn_layer(x, row(ffn_pre_g[i]), ffn_w_up[i].astype(BF16), ffn_conv_w[i],
                       row(ffn_conv_b[i]), _pad_weight_cols(ffn_w_down[i].astype(BF16)),
                       row(ffn_post_g[i]))
    return x
```

```python
import functools

import jax
import jax.numpy as jnp
from jax import lax
from jax.experimental import pallas as pl
from jax.experimental.pallas import tpu as pltpu

EPS = 1e-6
D_MODEL = 1024

SSD_D_INNER = 2048
SSD_HEAD_DIM = 64
SSD_N_HEADS = 32
SSD_N_GROUPS = 8
SSD_HEADS_PER_GROUP = 4
SSD_D_STATE = 128
SSD_CONV_W = 4
SSD_BC_DIM = SSD_N_GROUPS * SSD_D_STATE
SSD_CONV_DIM = SSD_D_INNER + 2 * SSD_BC_DIM
GROUP_COLS = SSD_HEADS_PER_GROUP * SSD_HEAD_DIM

SC_WIDTH = 1024
SC_CONV_W = 3
FFN_HIDDEN = 2816
FFN_CONV_W = 3

LANES = 128
SUBLANES = 8
MXU_COLS = 256
VMEM_LIMIT_BYTES = 60 * 1024 * 1024

ROW_GROUP = SUBLANES * SUBLANES
TAIL_VREG_ROWS = 3
TAIL_ROWS = TAIL_VREG_ROWS * SUBLANES

TIME_BLOCK = 512
SUB_ROWS = 256
SCAN_CHUNK = 128
NEG_BIG = -1e30

BF16 = jnp.bfloat16
F32 = jnp.float32


def _rms_norm(x, g):
    ms = jnp.mean(x * x, axis=-1, keepdims=True)
    return x * lax.rsqrt(ms + EPS) * g


def _silu(x):
    hx = 0.5 * x
    return hx * jnp.tanh(hx) + hx


def _softplus(x):
    return jnp.maximum(x, 0.0) + jnp.log1p(jnp.exp(-jnp.abs(x)))


def _dot(a, b):
    return jnp.dot(a, b, preferred_element_type=F32)


def _load_permuted(x_ref, stage_ref, r0, rows):
    d = x_ref.shape[2]
    for j in range(d // LANES):
        stage_ref[j, pl.ds(r0, rows), :] = x_ref[0, pl.ds(r0, rows), j * LANES:(j + 1) * LANES]
    cols = []
    for j in range(d // LANES):
        tiles = [stage_ref[j, pl.ds(r0 + g * ROW_GROUP + r, SUBLANES, stride=SUBLANES), :]
                 for g in range(rows // ROW_GROUP) for r in range(SUBLANES)]
        cols.append(jnp.concatenate(tiles, axis=0))
    return jnp.concatenate(cols, axis=1)


def _store_unpermuted(o_ref, stage_ref, r0, val):
    rows, d = val.shape
    for j in range(d // LANES):
        for g in range(rows // ROW_GROUP):
            for r in range(SUBLANES):
                p0 = g * ROW_GROUP + r * SUBLANES
                stage_ref[j, pl.ds(r0 + g * ROW_GROUP + r, SUBLANES, stride=SUBLANES), :] = (
                    val[p0:p0 + SUBLANES, j * LANES:(j + 1) * LANES])
    for j in range(d // LANES):
        o_ref[0, pl.ds(r0, rows), j * LANES:(j + 1) * LANES] = stage_ref[j, pl.ds(r0, rows), :]


def _time_of_position(p):
    return ((p >> 6) << 6) + ((p & 7) << 3) + ((p >> 3) & 7)


def _shifted(cur, prev_tail, k_max):
    rows = cur.shape[0]
    sub = lax.broadcasted_iota(jnp.int32, (SUBLANES, cur.shape[1]), 0)
    last = sub == SUBLANES - 1
    first_q = SUBLANES - TAIL_VREG_ROWS
    out = [[] for _ in range(k_max)]
    for g in range(rows // ROW_GROUP):
        base = g * ROW_GROUP
        prev = prev_tail if g == 0 else cur[base - TAIL_ROWS:base]
        rolled = {}
        for q in range(SUBLANES - k_max, SUBLANES):
            cur_q = cur[base + q * SUBLANES:base + (q + 1) * SUBLANES]
            prev_q = prev[(q - first_q) * SUBLANES:(q - first_q + 1) * SUBLANES]
            rolled[q] = pltpu.roll(jnp.where(last, prev_q, cur_q), 1, axis=0)
        for k in range(1, k_max + 1):
            pieces = [rolled[q] for q in range(SUBLANES - k, SUBLANES)]
            pieces.append(cur[base:base + (SUBLANES - k) * SUBLANES])
            out[k - 1].append(jnp.concatenate(pieces, axis=0))
    return [jnp.concatenate(o, axis=0) for o in out]


def _causal_conv(cur, prev_tail, w, b=None):
    k = w.shape[0]
    out = cur * w[k - 1:k]
    for s, shifted in enumerate(_shifted(cur, prev_tail, k - 1), start=1):
        out = out + shifted * w[k - 1 - s:k - s]
    if b is not None:
        out = out + b
    return out


def _weight_scratch(rows, cols):
    return pltpu.VMEM((rows, cols + LANES), BF16)


def _weight_stage(row_chunk, cols):
    return pltpu.VMEM((2, row_chunk, cols), F32)


def _stream_weight(src, dst_ref, stage_ref, sem):
    rows, cols = src.shape
    row_chunk = stage_ref.shape[1]
    n = rows // row_chunk

    def copy(i):
        return pltpu.make_async_copy(src.at[pl.ds(i * row_chunk, row_chunk), :],
                                     stage_ref.at[i % 2], sem.at[i % 2])

    copy(0).start()
    for i in range(n):
        if i + 1 < n:
            copy(i + 1).start()
        copy(i).wait()
        dst_ref[pl.ds(i * row_chunk, row_chunk), pl.ds(0, cols)] = stage_ref[i % 2].astype(BF16)


def _is_first_step():
    return jnp.logical_and(pl.program_id(0) == 0, pl.program_id(1) == 0)


_HBM_SPEC = pl.BlockSpec(memory_space=pl.ANY)
_DMA_SEMS = pltpu.SemaphoreType.DMA((2,))


def _const_spec(shape):
    return pl.BlockSpec(shape, lambda b, t: (0,) * len(shape), pipeline_mode=pl.Buffered(1))


def _x_spec():
    return pl.BlockSpec((1, TIME_BLOCK, D_MODEL), lambda b, t: (b, t, 0))


def _stage_scratch():
    return pltpu.VMEM((D_MODEL // LANES, TIME_BLOCK, LANES), F32)


def _compiler_params():
    return pltpu.CompilerParams(
        dimension_semantics=("arbitrary", "arbitrary"),
        vmem_limit_bytes=VMEM_LIMIT_BYTES,
    )


def _ffn_kernel(layer, x_ref, gpre_ref, wup_hbm, cw_ref, cb_ref, wdown_hbm, gpost_ref,
                o_ref, wup_ref, wdown_ref, upstage_ref, downstage_ref, wsem,
                tail_ref, hid_ref, xstage_ref, ostage_ref):
    t = pl.program_id(1)

    @pl.when(_is_first_step())
    def _():
        _stream_weight(wup_hbm.at[layer], wup_ref, upstage_ref, wsem)
        _stream_weight(wdown_hbm.at[layer], wdown_ref, downstage_ref, wsem)

    @pl.when(t == 0)
    def _():
        tail_ref[...] = jnp.zeros_like(tail_ref)

    n_chunks = FFN_HIDDEN // MXU_COLS
    tails = [tail_ref[:, pl.ds(c * MXU_COLS, MXU_COLS)] for c in range(n_chunks)]
    for sb in range(TIME_BLOCK // SUB_ROWS):
        r0 = sb * SUB_ROWS
        x = _load_permuted(x_ref, xstage_ref, r0, SUB_ROWS)
        hb = _rms_norm(x, gpre_ref[...]).astype(BF16)
        for c in range(n_chunks):
            cols = pl.ds(c * MXU_COLS, MXU_COLS)
            gate = _dot(hb, wup_ref[:, cols])
            val = _dot(hb, wup_ref[:, pl.ds(FFN_HIDDEN + c * MXU_COLS, MXU_COLS)])
            conv = _causal_conv(gate, tails[c], cw_ref[:, cols], cb_ref[:, cols])
            tails[c] = gate[SUB_ROWS - TAIL_ROWS:]
            hid_ref[pl.ds(r0, SUB_ROWS), cols] = (_silu(conv) * val).astype(BF16)
        f = _dot(hid_ref[pl.ds(r0, SUB_ROWS), :], wdown_ref[:, pl.ds(0, D_MODEL)])
        _store_unpermuted(o_ref, ostage_ref, r0, x + _rms_norm(f, gpost_ref[...]))
    for c in range(n_chunks):
        tail_ref[:, pl.ds(c * MXU_COLS, MXU_COLS)] = tails[c]


def _ffn_layer(layer, x, gpre, wup, cw, cb, wdown, gpost):
    bsz, length, _ = x.shape
    return pl.pallas_call(
        functools.partial(_ffn_kernel, layer),
        name="conv_ffn",
        grid=(bsz, length // TIME_BLOCK),
        in_specs=[
            _x_spec(),
            _const_spec(gpre.shape),
            _HBM_SPEC,
            _const_spec(cw.shape),
            _const_spec(cb.shape),
            _HBM_SPEC,
            _const_spec(gpost.shape),
        ],
        out_specs=_x_spec(),
        out_shape=jax.ShapeDtypeStruct(x.shape, x.dtype),
        scratch_shapes=[
            pltpu.VMEM((D_MODEL, 2 * FFN_HIDDEN), BF16),
            _weight_scratch(FFN_HIDDEN, D_MODEL),
            _weight_stage(64, 2 * FFN_HIDDEN),
            _weight_stage(256, D_MODEL),
            _DMA_SEMS,
            pltpu.VMEM((TAIL_ROWS, FFN_HIDDEN), F32),
            pltpu.VMEM((TIME_BLOCK, FFN_HIDDEN), BF16),
            _stage_scratch(),
            _stage_scratch(),
        ],
        compiler_params=_compiler_params(),
    )(x, gpre, wup, cw, cb, wdown, gpost)


def _sc_kernel(layer, x_ref, gpre_ref, win_hbm, cw_ref, wout_hbm, gpost_ref, o_ref,
               win_ref, wout_ref, instage_ref, outstage_ref, wsem,
               tail_ref, xstage_ref, ostage_ref):
    t = pl.program_id(1)

    @pl.when(_is_first_step())
    def _():
        _stream_weight(win_hbm.at[layer], win_ref, instage_ref, wsem)
        _stream_weight(wout_hbm.at[layer], wout_ref, outstage_ref, wsem)

    @pl.when(t == 0)
    def _():
        tail_ref[...] = jnp.zeros_like(tail_ref)

    n_chunks = SC_WIDTH // MXU_COLS
    tails = [tail_ref[:, pl.ds(c * MXU_COLS, MXU_COLS)] for c in range(n_chunks)]
    for sb in range(TIME_BLOCK // SUB_ROWS):
        r0 = sb * SUB_ROWS
        x = _load_permuted(x_ref, xstage_ref, r0, SUB_ROWS)
        hb = _rms_norm(x, gpre_ref[...]).astype(BF16)
        parts = []
        for c in range(n_chunks):
            cols = pl.ds(c * MXU_COLS, MXU_COLS)
            gb = _dot(hb, win_ref[:, cols])
            gc = _dot(hb, win_ref[:, pl.ds(SC_WIDTH + c * MXU_COLS, MXU_COLS)])
            v = _dot(hb, win_ref[:, pl.ds(2 * SC_WIDTH + c * MXU_COLS, MXU_COLS)])
            gcv = gc * v
            u = _causal_conv(gcv, tails[c], cw_ref[:, cols])
            tails[c] = gcv[SUB_ROWS - TAIL_ROWS:]
            parts.append((gb * u).astype(BF16))
        m = _dot(jnp.concatenate(parts, axis=1), wout_ref[:, pl.ds(0, D_MODEL)])
        _store_unpermuted(o_ref, ostage_ref, r0, x + _rms_norm(m, gpost_ref[...]))
    for c in range(n_chunks):
        tail_ref[:, pl.ds(c * MXU_COLS, MXU_COLS)] = tails[c]


def _sc_layer(layer, x, gpre, win, cw, wout, gpost):
    bsz, length, _ = x.shape
    return pl.pallas_call(
        functools.partial(_sc_kernel, layer),
        name="shortconv_mixer",
        grid=(bsz, length // TIME_BLOCK),
        in_specs=[
            _x_spec(),
            _const_spec(gpre.shape),
            _HBM_SPEC,
            _const_spec(cw.shape),
            _HBM_SPEC,
            _const_spec(gpost.shape),
        ],
        out_specs=_x_spec(),
        out_shape=jax.ShapeDtypeStruct(x.shape, x.dtype),
        scratch_shapes=[
            _weight_scratch(D_MODEL, 3 * SC_WIDTH),
            _weight_scratch(SC_WIDTH, D_MODEL),
            _weight_stage(128, 3 * SC_WIDTH),
            _weight_stage(256, D_MODEL),
            _DMA_SEMS,
            pltpu.VMEM((TAIL_ROWS, SC_WIDTH), F32),
            _stage_scratch(),
            _stage_scratch(),
        ],
        compiler_params=_compiler_params(),
    )(x, gpre, win, cw, wout, gpost)


def _expand_heads(a, first_head, n_heads):
    lane = lax.broadcasted_iota(jnp.int32, (a.shape[0], LANES), 1)
    low = lane < SSD_HEAD_DIM
    pieces = []
    for h in range(first_head, first_head + n_heads, 2):
        pieces.append(jnp.where(low, a[:, h:h + 1], a[:, h + 1:h + 2]))
    return jnp.concatenate(pieces, axis=1)


def _ssd_scan_chunk(xbc_ref, y_ref, state_ref, dskip_ref, r0, dt_c, a_c, causal, tril, gcol):
    tc = SCAN_CHUNK
    acs = jnp.dot(tril, a_c, precision=lax.Precision.HIGHEST,
                  preferred_element_type=F32)
    src_t = (acs - jnp.log(dt_c)).T
    total = acs[tc - 1:tc]
    e_in = jnp.exp(acs)
    w_out = jnp.exp(total - acs) * dt_c
    e_tot = jnp.exp(total)

    for g in range(SSD_N_GROUPS):
        h0 = g * SSD_HEADS_PER_GROUP
        xcols = pl.ds(g * GROUP_COLS, GROUP_COLS)
        xg = xbc_ref[pl.ds(r0, tc), xcols]
        bg = xbc_ref[pl.ds(r0, tc), pl.ds(SSD_D_INNER + g * SSD_D_STATE, SSD_D_STATE)].astype(BF16)
        cg = xbc_ref[pl.ds(r0, tc),
                     pl.ds(SSD_D_INNER + SSD_BC_DIM + g * SSD_D_STATE, SSD_D_STATE)].astype(BF16)
        cb_mat = lax.dot_general(cg, bg, (((1,), (1,)), ((), ())),
                                 preferred_element_type=F32)
        m_parts = []
        x_parts = []
        for j in range(SSD_HEADS_PER_GROUP):
            h = h0 + j
            seg = acs[:, h:h + 1] - src_t[h:h + 1, :]
            lmat = jnp.exp(jnp.where(causal, seg, NEG_BIG))
            m_parts.append((cb_mat * lmat).astype(BF16))
            x_parts.append(jnp.where(gcol == j, xg, 0.0).astype(BF16))
        m_cat = jnp.concatenate(m_parts, axis=1)
        x_bd = jnp.concatenate(x_parts, axis=0)
        y_diag = _dot(m_cat, x_bd)

        e_in_g = _expand_heads(e_in, h0, SSD_HEADS_PER_GROUP)
        w_out_g = _expand_heads(w_out, h0, SSD_HEADS_PER_GROUP)
        e_tot_g = _expand_heads(e_tot, h0, SSD_HEADS_PER_GROUP)
        state = state_ref[g]
        y_off = _dot(cg, state.astype(BF16)) * e_in_g
        y_ref[pl.ds(r0, tc), xcols] = y_diag + y_off + xg * dskip_ref[:, xcols]
        upd = lax.dot_general(bg, (xg * w_out_g).astype(BF16), (((0,), (0,)), ((), ())),
                              preferred_element_type=F32)
        state_ref[g] = state * e_tot_g + upd
        yield


_DONE = object()


def _run(task):
    for _ in task:
        pass


def _chain(*tasks):
    for task in tasks:
        yield from task


def _interleave(primary, secondary):
    primary_live = secondary_live = True
    while primary_live or secondary_live:
        if primary_live:
            primary_live = next(primary, _DONE) is not _DONE
        if secondary_live:
            secondary_live = next(secondary, _DONE) is not _DONE


def _ssd_kernel(layer, x_ref, gpre_ref, win_hbm, wdt_ref, cw_ref, cb_ref, dtb_ref, alog_ref,
                dskip_ref, normw_ref, wout_hbm, gpost_ref,
                o_ref, wz_ref, wxbc_ref, wout_ref, zstage_ref, xbcstage_ref, outstage_ref, wsem,
                tail_ref, state_ref, xbc_ref, y_ref, xstage_ref, ostage_ref):
    t = pl.program_id(1)

    @pl.when(_is_first_step())
    def _():
        w_in = win_hbm.at[layer]
        _stream_weight(w_in.at[:, pl.ds(SSD_D_INNER, SSD_CONV_DIM)], wxbc_ref, xbcstage_ref, wsem)
        _stream_weight(w_in.at[:, pl.ds(0, SSD_D_INNER)], wz_ref, zstage_ref, wsem)
        _stream_weight(wout_hbm.at[layer], wout_ref, outstage_ref, wsem)

    @pl.when(t == 0)
    def _():
        tail_ref[...] = jnp.zeros_like(tail_ref)
        state_ref[...] = jnp.zeros_like(state_ref)

    tc = SCAN_CHUNK
    time_l = _time_of_position(lax.broadcasted_iota(jnp.int32, (tc, tc), 0))
    time_s = _time_of_position(lax.broadcasted_iota(jnp.int32, (tc, tc), 1))
    causal = time_l >= time_s
    tril = causal.astype(F32)
    gcol = lax.broadcasted_iota(jnp.int32, (tc, GROUP_COLS), 1) // SSD_HEAD_DIM

    n_chunks = SSD_CONV_DIM // MXU_COLS
    n_sub = TIME_BLOCK // SUB_ROWS
    tails = [tail_ref[:, pl.ds(c * MXU_COLS, MXU_COLS)] for c in range(n_chunks)]
    ctx = [dict() for _ in range(n_sub)]

    def in_proj(sb):
        r0 = sb * SUB_ROWS
        x = _load_permuted(x_ref, xstage_ref, r0, SUB_ROWS)
        hb = _rms_norm(x, gpre_ref[...]).astype(BF16)
        ctx[sb].update(x=x, hb=hb)
        for c in range(n_chunks):
            cols = pl.ds(c * MXU_COLS, MXU_COLS)
            raw = _dot(hb, wxbc_ref[:, cols])
            conv = _causal_conv(raw, tails[c], cw_ref[:, cols], cb_ref[:, cols])
            tails[c] = raw[SUB_ROWS - TAIL_ROWS:]
            xbc_ref[pl.ds(r0, SUB_ROWS), cols] = _silu(conv)
            yield
        dt = _softplus(_dot(hb, wdt_ref[...]) + dtb_ref[...])
        ctx[sb].update(dt=dt, a=dt * -jnp.exp(alog_ref[...]))
        yield

    def scan(sb):
        dt, a = ctx[sb]["dt"], ctx[sb]["a"]
        for k in range(SUB_ROWS // tc):
            yield from _ssd_scan_chunk(xbc_ref, y_ref, state_ref, dskip_ref, sb * SUB_ROWS + k * tc,
                                       dt[k * tc:(k + 1) * tc], a[k * tc:(k + 1) * tc],
                                       causal, tril, gcol)

    def out_proj(sb):
        r0 = sb * SUB_ROWS
        sub = pl.ds(r0, SUB_ROWS)
        x, hb = ctx[sb]["x"], ctx[sb]["hb"]
        sumsq = jnp.zeros((SUB_ROWS, 1), F32)
        for c in range(SSD_D_INNER // MXU_COLS):
            cols = pl.ds(c * MXU_COLS, MXU_COLS)
            gated = y_ref[sub, cols] * _silu(_dot(hb, wz_ref[:, cols]))
            sumsq = sumsq + jnp.sum(gated * gated, axis=-1, keepdims=True)
            y_ref[sub, cols] = gated
            yield
        scale = lax.rsqrt(sumsq * (1.0 / SSD_D_INNER) + EPS)
        yn = (y_ref[sub, :] * scale * normw_ref[...]).astype(BF16)
        parts = []
        for c in range(D_MODEL // MXU_COLS):
            parts.append(_dot(yn, wout_ref[:, pl.ds(c * MXU_COLS, MXU_COLS)]))
            yield
        m = jnp.concatenate(parts, axis=1)
        _store_unpermuted(o_ref, ostage_ref, r0, x + _rms_norm(m, gpost_ref[...]))

    _run(in_proj(0))
    for sb in range(n_sub):
        fill = []
        if sb > 0:
            fill.append(out_proj(sb - 1))
        if sb + 1 < n_sub:
            fill.append(in_proj(sb + 1))
        _interleave(scan(sb), _chain(*fill))
    _run(out_proj(n_sub - 1))
    for c in range(n_chunks):
        tail_ref[:, pl.ds(c * MXU_COLS, MXU_COLS)] = tails[c]


def _ssd_layer(layer, x, gpre, win, wdt, cw, cb, dtb, alog, dskip, normw, wout, gpost):
    bsz, length, _ = x.shape
    small = (wdt, cw, cb, dtb, alog, dskip, normw)
    return pl.pallas_call(
        functools.partial(_ssd_kernel, layer),
        name="ssd_mixer",
        grid=(bsz, length // TIME_BLOCK),
        in_specs=([_x_spec(), _const_spec(gpre.shape), _HBM_SPEC]
                  + [_const_spec(c.shape) for c in small] + [_HBM_SPEC, _const_spec(gpost.shape)]),
        out_specs=_x_spec(),
        out_shape=jax.ShapeDtypeStruct(x.shape, x.dtype),
        scratch_shapes=[
            _weight_scratch(D_MODEL, SSD_D_INNER),
            _weight_scratch(D_MODEL, SSD_CONV_DIM),
            _weight_scratch(SSD_D_INNER, D_MODEL),
            _weight_stage(128, SSD_D_INNER),
            _weight_stage(64, SSD_CONV_DIM),
            _weight_stage(256, D_MODEL),
            _DMA_SEMS,
            pltpu.VMEM((TAIL_ROWS, SSD_CONV_DIM), F32),
            pltpu.VMEM((SSD_N_GROUPS, SSD_D_STATE, GROUP_COLS), F32),
            pltpu.VMEM((TIME_BLOCK, SSD_CONV_DIM), F32),
            pltpu.VMEM((TIME_BLOCK, SSD_D_INNER), F32),
            _stage_scratch(),
            _stage_scratch(),
        ],
        compiler_params=_compiler_params(),
    )(x, gpre, win, *small, wout, gpost)


def _pad_lanes(v):
    return jnp.pad(v, ((0, 0), (0, LANES - v.shape[-1])))


def kernel(x, mix_pre_g, mix_post_g, ffn_pre_g, ffn_post_g, ssd_w_in, ssd_conv_w, ssd_conv_b,
           ssd_dt_bias, ssd_A_log, ssd_D, ssd_norm_w, ssd_w_out, sc_w_in, sc_conv_w, sc_w_out,
           ffn_w_up, ffn_conv_w, ffn_conv_b, ffn_w_down):
    depth = mix_pre_g.shape[0]
    row = lambda v: v.reshape(1, -1)
    for i in range(depth):
        j = i // 2
        if i % 2 == 0:
            wdt = _pad_lanes(ssd_w_in[j, :, SSD_D_INNER + SSD_CONV_DIM:]).astype(BF16)
            dskip = jnp.repeat(ssd_D[j], SSD_HEAD_DIM).reshape(1, -1)
            x = _ssd_layer(j, x, row(mix_pre_g[i]), ssd_w_in, wdt, ssd_conv_w[j], row(ssd_conv_b[j]),
                           _pad_lanes(row(ssd_dt_bias[j])), _pad_lanes(row(ssd_A_log[j])), dskip,
                           row(ssd_norm_w[j]), ssd_w_out, row(mix_post_g[i]))
        else:
            x = _sc_layer(j, x, row(mix_pre_g[i]), sc_w_in, sc_conv_w[j], sc_w_out,
                          row(mix_post_g[i]))
        x = _ffn_layer(i, x, row(ffn_pre_g[i]), ffn_w_up, ffn_conv_w[i], row(ffn_conv_b[i]),
                       ffn_w_down, row(ffn_post_g[i]))
    return x
```

```python
import functools

import jax
import jax.numpy as jnp
from jax import lax
from jax.experimental import pallas as pl
from jax.experimental.pallas import tpu as pltpu

EPS = 1e-6
D_MODEL = 1024

SSD_D_INNER = 2048
SSD_HEAD_DIM = 64
SSD_N_HEADS = 32
SSD_N_GROUPS = 8
SSD_HEADS_PER_GROUP = 4
SSD_D_STATE = 128
SSD_CONV_W = 4
SSD_BC_DIM = SSD_N_GROUPS * SSD_D_STATE
SSD_CONV_DIM = SSD_D_INNER + 2 * SSD_BC_DIM
GROUP_COLS = SSD_HEADS_PER_GROUP * SSD_HEAD_DIM

SC_WIDTH = 1024
SC_CONV_W = 3
FFN_HIDDEN = 2816
FFN_CONV_W = 3

LANES = 128
SUBLANES = 8
MXU_COLS = 256
VMEM_LIMIT_BYTES = 60 * 1024 * 1024

ROW_GROUP = SUBLANES * SUBLANES
TAIL_VREG_ROWS = 3
TAIL_ROWS = TAIL_VREG_ROWS * SUBLANES

TIME_BLOCK = 512
SC_TIME_BLOCK = 1024
SUB_ROWS = 256
SCAN_CHUNK = 128
NEG_BIG = -1e30

BF16 = jnp.bfloat16
F32 = jnp.float32


def _rms_norm(x, g):
    ms = jnp.mean(x * x, axis=-1, keepdims=True)
    return x * lax.rsqrt(ms + EPS) * g


def _silu(x):
    hx = 0.5 * x
    return hx * jnp.tanh(hx) + hx


def _softplus(x):
    return jnp.maximum(x, 0.0) + jnp.log1p(jnp.exp(-jnp.abs(x)))


def _dot(a, b):
    return jnp.dot(a, b, preferred_element_type=F32)


def _load_permuted(x_ref, stage_ref, r0, rows):
    d = x_ref.shape[2]
    for j in range(d // LANES):
        stage_ref[j, pl.ds(r0, rows), :] = x_ref[0, pl.ds(r0, rows), j * LANES:(j + 1) * LANES]
    cols = []
    for j in range(d // LANES):
        tiles = [stage_ref[j, pl.ds(r0 + g * ROW_GROUP + r, SUBLANES, stride=SUBLANES), :]
                 for g in range(rows // ROW_GROUP) for r in range(SUBLANES)]
        cols.append(jnp.concatenate(tiles, axis=0))
    return jnp.concatenate(cols, axis=1)


def _store_unpermuted(o_ref, stage_ref, r0, val):
    rows, d = val.shape
    for j in range(d // LANES):
        for g in range(rows // ROW_GROUP):
            for r in range(SUBLANES):
                p0 = g * ROW_GROUP + r * SUBLANES
                stage_ref[j, pl.ds(r0 + g * ROW_GROUP + r, SUBLANES, stride=SUBLANES), :] = (
                    val[p0:p0 + SUBLANES, j * LANES:(j + 1) * LANES])
    for j in range(d // LANES):
        o_ref[0, pl.ds(r0, rows), j * LANES:(j + 1) * LANES] = stage_ref[j, pl.ds(r0, rows), :]


def _time_of_position(p):
    return ((p >> 6) << 6) + ((p & 7) << 3) + ((p >> 3) & 7)


def _shifted(cur, prev_tail, k_max):
    rows = cur.shape[0]
    sub = lax.broadcasted_iota(jnp.int32, (SUBLANES, cur.shape[1]), 0)
    last = sub == SUBLANES - 1
    first_q = SUBLANES - TAIL_VREG_ROWS
    out = [[] for _ in range(k_max)]
    for g in range(rows // ROW_GROUP):
        base = g * ROW_GROUP
        prev = prev_tail if g == 0 else cur[base - TAIL_ROWS:base]
        rolled = {}
        for q in range(SUBLANES - k_max, SUBLANES):
            cur_q = cur[base + q * SUBLANES:base + (q + 1) * SUBLANES]
            prev_q = prev[(q - first_q) * SUBLANES:(q - first_q + 1) * SUBLANES]
            rolled[q] = pltpu.roll(jnp.where(last, prev_q, cur_q), 1, axis=0)
        for k in range(1, k_max + 1):
            pieces = [rolled[q] for q in range(SUBLANES - k, SUBLANES)]
            pieces.append(cur[base:base + (SUBLANES - k) * SUBLANES])
            out[k - 1].append(jnp.concatenate(pieces, axis=0))
    return [jnp.concatenate(o, axis=0) for o in out]


def _causal_conv(cur, prev_tail, w, b=None):
    k = w.shape[0]
    out = cur * w[k - 1:k]
    for s, shifted in enumerate(_shifted(cur, prev_tail, k - 1), start=1):
        out = out + shifted * w[k - 1 - s:k - s]
    if b is not None:
        out = out + b
    return out


def _weight_scratch(rows, cols):
    return pltpu.VMEM((rows, cols + LANES), BF16)


def _weight_stage(row_chunk, cols):
    return pltpu.VMEM((2, row_chunk, cols), F32)


def _stream_weight(src, dst_ref, stage_ref, sem):
    rows, cols = src.shape
    row_chunk = stage_ref.shape[1]
    n = rows // row_chunk
    if cols % LANES:
        last_tile = pl.ds(cols // LANES * LANES, LANES)
        dst_ref[:, last_tile] = jnp.zeros((rows, LANES), BF16)

    def copy(i):
        return pltpu.make_async_copy(src.at[pl.ds(i * row_chunk, row_chunk), :],
                                     stage_ref.at[i % 2], sem.at[i % 2])

    copy(0).start()
    for i in range(n):
        if i + 1 < n:
            copy(i + 1).start()
        copy(i).wait()
        dst_ref[pl.ds(i * row_chunk, row_chunk), pl.ds(0, cols)] = stage_ref[i % 2].astype(BF16)


def _is_first_step():
    return jnp.logical_and(pl.program_id(0) == 0, pl.program_id(1) == 0)


_HBM_SPEC = pl.BlockSpec(memory_space=pl.ANY)
_DMA_SEMS = pltpu.SemaphoreType.DMA((2,))


def _const_spec(shape):
    return pl.BlockSpec(shape, lambda b, t: (0,) * len(shape), pipeline_mode=pl.Buffered(1))


def _x_spec(time_block=TIME_BLOCK):
    return pl.BlockSpec((1, time_block, D_MODEL), lambda b, t: (b, t, 0))


def _stage_scratch(time_block=TIME_BLOCK):
    return pltpu.VMEM((D_MODEL // LANES, time_block, LANES), F32)


def _compiler_params():
    return pltpu.CompilerParams(
        dimension_semantics=("arbitrary", "arbitrary"),
        vmem_limit_bytes=VMEM_LIMIT_BYTES,
    )


def _ffn_kernel(layer, x_ref, gpre_ref, wup_hbm, cw_ref, cb_ref, wdown_hbm, gpost_ref,
                o_ref, wup_ref, wdown_ref, upstage_ref, downstage_ref, wsem,
                tail_ref, hid_ref, xstage_ref, ostage_ref):
    t = pl.program_id(1)

    @pl.when(_is_first_step())
    def _():
        _stream_weight(wup_hbm.at[layer], wup_ref, upstage_ref, wsem)
        _stream_weight(wdown_hbm.at[layer], wdown_ref, downstage_ref, wsem)

    @pl.when(t == 0)
    def _():
        tail_ref[...] = jnp.zeros_like(tail_ref)

    n_chunks = FFN_HIDDEN // MXU_COLS
    tails = [tail_ref[:, pl.ds(c * MXU_COLS, MXU_COLS)] for c in range(n_chunks)]
    for sb in range(TIME_BLOCK // SUB_ROWS):
        r0 = sb * SUB_ROWS
        x = _load_permuted(x_ref, xstage_ref, r0, SUB_ROWS)
        hb = _rms_norm(x, gpre_ref[...]).astype(BF16)
        for c in range(n_chunks):
            cols = pl.ds(c * MXU_COLS, MXU_COLS)
            gate = _dot(hb, wup_ref[:, cols])
            val = _dot(hb, wup_ref[:, pl.ds(FFN_HIDDEN + c * MXU_COLS, MXU_COLS)])
            conv = _causal_conv(gate, tails[c], cw_ref[:, cols], cb_ref[:, cols])
            tails[c] = gate[SUB_ROWS - TAIL_ROWS:]
            hid_ref[pl.ds(r0, SUB_ROWS), cols] = (_silu(conv) * val).astype(BF16)
        f = _dot(hid_ref[pl.ds(r0, SUB_ROWS), :], wdown_ref[:, pl.ds(0, D_MODEL)])
        _store_unpermuted(o_ref, ostage_ref, r0, x + _rms_norm(f, gpost_ref[...]))
    for c in range(n_chunks):
        tail_ref[:, pl.ds(c * MXU_COLS, MXU_COLS)] = tails[c]


def _ffn_layer(layer, x, gpre, wup, cw, cb, wdown, gpost):
    bsz, length, _ = x.shape
    return pl.pallas_call(
        functools.partial(_ffn_kernel, layer),
        name="conv_ffn",
        grid=(bsz, length // TIME_BLOCK),
        in_specs=[
            _x_spec(),
            _const_spec(gpre.shape),
            _HBM_SPEC,
            _const_spec(cw.shape),
            _const_spec(cb.shape),
            _HBM_SPEC,
            _const_spec(gpost.shape),
        ],
        out_specs=_x_spec(),
        out_shape=jax.ShapeDtypeStruct(x.shape, x.dtype),
        scratch_shapes=[
            pltpu.VMEM((D_MODEL, 2 * FFN_HIDDEN), BF16),
            _weight_scratch(FFN_HIDDEN, D_MODEL),
            _weight_stage(64, 2 * FFN_HIDDEN),
            _weight_stage(256, D_MODEL),
            _DMA_SEMS,
            pltpu.VMEM((TAIL_ROWS, FFN_HIDDEN), F32),
            pltpu.VMEM((TIME_BLOCK, FFN_HIDDEN), BF16),
            _stage_scratch(),
            _stage_scratch(),
        ],
        compiler_params=_compiler_params(),
    )(x, gpre, wup, cw, cb, wdown, gpost)


def _sc_kernel(layer, x_ref, gpre_ref, win_hbm, cw_ref, wout_hbm, gpost_ref, o_ref,
               win_ref, wout_ref, instage_ref, outstage_ref, wsem,
               tail_ref, xstage_ref, ostage_ref):
    t = pl.program_id(1)

    @pl.when(_is_first_step())
    def _():
        _stream_weight(win_hbm.at[layer], win_ref, instage_ref, wsem)
        _stream_weight(wout_hbm.at[layer], wout_ref, outstage_ref, wsem)

    @pl.when(t == 0)
    def _():
        tail_ref[...] = jnp.zeros_like(tail_ref)

    n_chunks = SC_WIDTH // MXU_COLS
    tails = [tail_ref[:, pl.ds(c * MXU_COLS, MXU_COLS)] for c in range(n_chunks)]
    for sb in range(SC_TIME_BLOCK // SUB_ROWS):
        r0 = sb * SUB_ROWS
        x = _load_permuted(x_ref, xstage_ref, r0, SUB_ROWS)
        hb = _rms_norm(x, gpre_ref[...]).astype(BF16)
        parts = []
        for c in range(n_chunks):
            cols = pl.ds(c * MXU_COLS, MXU_COLS)
            gb = _dot(hb, win_ref[:, cols])
            gc = _dot(hb, win_ref[:, pl.ds(SC_WIDTH + c * MXU_COLS, MXU_COLS)])
            v = _dot(hb, win_ref[:, pl.ds(2 * SC_WIDTH + c * MXU_COLS, MXU_COLS)])
            gcv = gc * v
            u = _causal_conv(gcv, tails[c], cw_ref[:, cols])
            tails[c] = gcv[SUB_ROWS - TAIL_ROWS:]
            parts.append((gb * u).astype(BF16))
        m = _dot(jnp.concatenate(parts, axis=1), wout_ref[:, pl.ds(0, D_MODEL)])
        _store_unpermuted(o_ref, ostage_ref, r0, x + _rms_norm(m, gpost_ref[...]))
    for c in range(n_chunks):
        tail_ref[:, pl.ds(c * MXU_COLS, MXU_COLS)] = tails[c]


def _sc_layer(layer, x, gpre, win, cw, wout, gpost):
    bsz, length, _ = x.shape
    return pl.pallas_call(
        functools.partial(_sc_kernel, layer),
        name="shortconv_mixer",
        grid=(bsz, length // SC_TIME_BLOCK),
        in_specs=[
            _x_spec(SC_TIME_BLOCK),
            _const_spec(gpre.shape),
            _HBM_SPEC,
            _const_spec(cw.shape),
            _HBM_SPEC,
            _const_spec(gpost.shape),
        ],
        out_specs=_x_spec(SC_TIME_BLOCK),
        out_shape=jax.ShapeDtypeStruct(x.shape, x.dtype),
        scratch_shapes=[
            _weight_scratch(D_MODEL, 3 * SC_WIDTH),
            _weight_scratch(SC_WIDTH, D_MODEL),
            _weight_stage(128, 3 * SC_WIDTH),
            _weight_stage(256, D_MODEL),
            _DMA_SEMS,
            pltpu.VMEM((TAIL_ROWS, SC_WIDTH), F32),
            _stage_scratch(SC_TIME_BLOCK),
            _stage_scratch(SC_TIME_BLOCK),
        ],
        compiler_params=_compiler_params(),
    )(x, gpre, win, cw, wout, gpost)


def _expand_heads(a, first_head, n_heads):
    lane = lax.broadcasted_iota(jnp.int32, (a.shape[0], LANES), 1)
    low = lane < SSD_HEAD_DIM
    pieces = []
    for h in range(first_head, first_head + n_heads, 2):
        pieces.append(jnp.where(low, a[:, h:h + 1], a[:, h + 1:h + 2]))
    return jnp.concatenate(pieces, axis=1)


def _ssd_scan_chunk(xbc_ref, y_ref, state_ref, dskip_ref, r0, dt_c, a_c, causal, tril, gcol):
    tc = SCAN_CHUNK
    acs = jnp.dot(tril, a_c, precision=lax.Precision.HIGHEST,
                  preferred_element_type=F32)
    src_t = (acs - jnp.log(dt_c)).T
    total = acs[tc - 1:tc]
    e_in = jnp.exp(acs)
    w_out = jnp.exp(total - acs) * dt_c
    e_tot = jnp.exp(total)

    for g in range(SSD_N_GROUPS):
        h0 = g * SSD_HEADS_PER_GROUP
        xcols = pl.ds(g * GROUP_COLS, GROUP_COLS)
        xg = xbc_ref[pl.ds(r0, tc), xcols]
        bg = xbc_ref[pl.ds(r0, tc), pl.ds(SSD_D_INNER + g * SSD_D_STATE, SSD_D_STATE)].astype(BF16)
        cg = xbc_ref[pl.ds(r0, tc),
                     pl.ds(SSD_D_INNER + SSD_BC_DIM + g * SSD_D_STATE, SSD_D_STATE)].astype(BF16)
        cb_mat = lax.dot_general(cg, bg, (((1,), (1,)), ((), ())),
                                 preferred_element_type=F32)
        m_parts = []
        x_parts = []
        for j in range(SSD_HEADS_PER_GROUP):
            h = h0 + j
            seg = acs[:, h:h + 1] - src_t[h:h + 1, :]
            lmat = jnp.exp(jnp.where(causal, seg, NEG_BIG))
            m_parts.append((cb_mat * lmat).astype(BF16))
            x_parts.append(jnp.where(gcol == j, xg, 0.0).astype(BF16))
        m_cat = jnp.concatenate(m_parts, axis=1)
        x_bd = jnp.concatenate(x_parts, axis=0)
        y_diag = _dot(m_cat, x_bd)

        e_in_g = _expand_heads(e_in, h0, SSD_HEADS_PER_GROUP)
        w_out_g = _expand_heads(w_out, h0, SSD_HEADS_PER_GROUP)
        e_tot_g = _expand_heads(e_tot, h0, SSD_HEADS_PER_GROUP)
        state = state_ref[g]
        y_off = _dot(cg, state.astype(BF16)) * e_in_g
        y_ref[pl.ds(r0, tc), xcols] = y_diag + y_off + xg * dskip_ref[:, xcols]
        upd = lax.dot_general(bg, (xg * w_out_g).astype(BF16), (((0,), (0,)), ((), ())),
                              preferred_element_type=F32)
        state_ref[g] = state * e_tot_g + upd
        yield


_DONE = object()


def _run(task):
    for _ in task:
        pass


def _chain(*tasks):
    for task in tasks:
        yield from task


def _interleave(primary, secondary):
    primary_live = secondary_live = True
    while primary_live or secondary_live:
        if primary_live:
            primary_live = next(primary, _DONE) is not _DONE
        if secondary_live:
            secondary_live = next(secondary, _DONE) is not _DONE


def _ssd_kernel(layer, x_ref, gpre_ref, win_hbm, cw_ref, cb_ref, dtb_ref, alog_ref,
                dskip_ref, normw_ref, wout_hbm, gpost_ref,
                o_ref, wz_ref, wxbc_ref, wout_ref, zstage_ref, xbcstage_ref, outstage_ref, wsem,
                tail_ref, state_ref, xbc_ref, y_ref, xstage_ref, ostage_ref):
    t = pl.program_id(1)

    @pl.when(_is_first_step())
    def _():
        w_in = win_hbm.at[layer]
        _stream_weight(w_in.at[:, pl.ds(SSD_D_INNER, SSD_CONV_DIM + SSD_N_HEADS)], wxbc_ref,
                       xbcstage_ref, wsem)
        _stream_weight(w_in.at[:, pl.ds(0, SSD_D_INNER)], wz_ref, zstage_ref, wsem)
        _stream_weight(wout_hbm.at[layer], wout_ref, outstage_ref, wsem)

    @pl.when(t == 0)
    def _():
        tail_ref[...] = jnp.zeros_like(tail_ref)
        state_ref[...] = jnp.zeros_like(state_ref)

    tc = SCAN_CHUNK
    time_l = _time_of_position(lax.broadcasted_iota(jnp.int32, (tc, tc), 0))
    time_s = _time_of_position(lax.broadcasted_iota(jnp.int32, (tc, tc), 1))
    causal = time_l >= time_s
    tril = causal.astype(F32)
    gcol = lax.broadcasted_iota(jnp.int32, (tc, GROUP_COLS), 1) // SSD_HEAD_DIM

    n_chunks = SSD_CONV_DIM // MXU_COLS
    n_sub = TIME_BLOCK // SUB_ROWS
    tails = [tail_ref[:, pl.ds(c * MXU_COLS, MXU_COLS)] for c in range(n_chunks)]
    ctx = [dict() for _ in range(n_sub)]

    def in_proj(sb):
        r0 = sb * SUB_ROWS
        x = _load_permuted(x_ref, xstage_ref, r0, SUB_ROWS)
        hb = _rms_norm(x, gpre_ref[...]).astype(BF16)
        ctx[sb].update(x=x, hb=hb)
        for c in range(n_chunks):
            cols = pl.ds(c * MXU_COLS, MXU_COLS)
            raw = _dot(hb, wxbc_ref[:, cols])
            conv = _causal_conv(raw, tails[c], cw_ref[:, cols], cb_ref[:, cols])
            tails[c] = raw[SUB_ROWS - TAIL_ROWS:]
            xbc_ref[pl.ds(r0, SUB_ROWS), cols] = _silu(conv)
            yield
        dt_raw = _dot(hb, wxbc_ref[:, pl.ds(SSD_CONV_DIM, LANES)])
        dt = _softplus(dt_raw + dtb_ref[...])
        ctx[sb].update(dt=dt, a=dt * -jnp.exp(alog_ref[...]))
        yield

    def scan(sb):
        dt, a = ctx[sb]["dt"], ctx[sb]["a"]
        for k in range(SUB_ROWS // tc):
            yield from _ssd_scan_chunk(xbc_ref, y_ref, state_ref, dskip_ref, sb * SUB_ROWS + k * tc,
                                       dt[k * tc:(k + 1) * tc], a[k * tc:(k + 1) * tc],
                                       causal, tril, gcol)

    def out_proj(sb):
        r0 = sb * SUB_ROWS
        sub = pl.ds(r0, SUB_ROWS)
        x, hb = ctx[sb]["x"], ctx[sb]["hb"]
        sumsq = jnp.zeros((SUB_ROWS, 1), F32)
        for c in range(SSD_D_INNER // MXU_COLS):
            cols = pl.ds(c * MXU_COLS, MXU_COLS)
            gated = y_ref[sub, cols] * _silu(_dot(hb, wz_ref[:, cols]))
            sumsq = sumsq + jnp.sum(gated * gated, axis=-1, keepdims=True)
            y_ref[sub, cols] = gated
            yield
        scale = lax.rsqrt(sumsq * (1.0 / SSD_D_INNER) + EPS)
        yn = (y_ref[sub, :] * scale * normw_ref[...]).astype(BF16)
        parts = []
        for c in range(D_MODEL // MXU_COLS):
            parts.append(_dot(yn, wout_ref[:, pl.ds(c * MXU_COLS, MXU_COLS)]))
            yield
        m = jnp.concatenate(parts, axis=1)
        _store_unpermuted(o_ref, ostage_ref, r0, x + _rms_norm(m, gpost_ref[...]))

    _run(in_proj(0))
    for sb in range(n_sub):
        fill = []
        if sb > 0:
            fill.append(out_proj(sb - 1))
        if sb + 1 < n_sub:
            fill.append(in_proj(sb + 1))
        _interleave(scan(sb), _chain(*fill))
    _run(out_proj(n_sub - 1))
    for c in range(n_chunks):
        tail_ref[:, pl.ds(c * MXU_COLS, MXU_COLS)] = tails[c]


def _ssd_layer(layer, x, gpre, win, cw, cb, dtb, alog, dskip, normw, wout, gpost):
    bsz, length, _ = x.shape
    small = (cw, cb, dtb, alog, dskip, normw)
    return pl.pallas_call(
        functools.partial(_ssd_kernel, layer),
        name="ssd_mixer",
        grid=(bsz, length // TIME_BLOCK),
        in_specs=([_x_spec(), _const_spec(gpre.shape), _HBM_SPEC]
                  + [_const_spec(c.shape) for c in small] + [_HBM_SPEC, _const_spec(gpost.shape)]),
        out_specs=_x_spec(),
        out_shape=jax.ShapeDtypeStruct(x.shape, x.dtype),
        scratch_shapes=[
            _weight_scratch(D_MODEL, SSD_D_INNER),
            _weight_scratch(D_MODEL, SSD_CONV_DIM),
            _weight_scratch(SSD_D_INNER, D_MODEL),
            _weight_stage(128, SSD_D_INNER),
            _weight_stage(64, SSD_CONV_DIM + SSD_N_HEADS),
            _weight_stage(256, D_MODEL),
            _DMA_SEMS,
            pltpu.VMEM((TAIL_ROWS, SSD_CONV_DIM), F32),
            pltpu.VMEM((SSD_N_GROUPS, SSD_D_STATE, GROUP_COLS), F32),
            pltpu.VMEM((TIME_BLOCK, SSD_CONV_DIM), F32),
            pltpu.VMEM((TIME_BLOCK, SSD_D_INNER), F32),
            _stage_scratch(),
            _stage_scratch(),
        ],
        compiler_params=_compiler_params(),
    )(x, gpre, win, *small, wout, gpost)


def _pad_lanes(v):
    return jnp.pad(v, ((0, 0), (0, LANES - v.shape[-1])))


def kernel(x, mix_pre_g, mix_post_g, ffn_pre_g, ffn_post_g, ssd_w_in, ssd_conv_w, ssd_conv_b,
           ssd_dt_bias, ssd_A_log, ssd_D, ssd_norm_w, ssd_w_out, sc_w_in, sc_conv_w, sc_w_out,
           ffn_w_up, ffn_conv_w, ffn_conv_b, ffn_w_down):
    depth = mix_pre_g.shape[0]
    row = lambda v: v.reshape(1, -1)
    for i in range(depth):
        j = i // 2
        if i % 2 == 0:
            dskip = jnp.repeat(ssd_D[j], SSD_HEAD_DIM).reshape(1, -1)
            x = _ssd_layer(j, x, row(mix_pre_g[i]), ssd_w_in, ssd_conv_w[j], row(ssd_conv_b[j]),
                           _pad_lanes(row(ssd_dt_bias[j])), _pad_lanes(row(ssd_A_log[j])), dskip,
                           row(ssd_norm_w[j]), ssd_w_out, row(mix_post_g[i]))
        else:
            x = _sc_layer(j, x, row(mix_pre_g[i]), sc_w_in, sc_conv_w[j], sc_w_out,
                          row(mix_post_g[i]))
        x = _ffn_layer(i, x, row(ffn_pre_g[i]), ffn_w_up, ffn_conv_w[i], row(ffn_conv_b[i]),
                       ffn_w_down, row(ffn_post_g[i]))
    return x
```

```python
import functools

import jax
import jax.numpy as jnp
from jax import lax
from jax.experimental import pallas as pl
from jax.experimental.pallas import tpu as pltpu

EPS = 1e-6
D_MODEL = 1024

SSD_D_INNER = 2048
SSD_HEAD_DIM = 64
SSD_N_HEADS = 32
SSD_N_GROUPS = 8
SSD_HEADS_PER_GROUP = 4
SSD_D_STATE = 128
SSD_CONV_W = 4
SSD_BC_DIM = SSD_N_GROUPS * SSD_D_STATE
SSD_CONV_DIM = SSD_D_INNER + 2 * SSD_BC_DIM
GROUP_COLS = SSD_HEADS_PER_GROUP * SSD_HEAD_DIM

SC_WIDTH = 1024
SC_CONV_W = 3
FFN_HIDDEN = 2816
FFN_CONV_W = 3

LANES = 128
SUBLANES = 8
MXU_COLS = 256
VMEM_LIMIT_BYTES = 60 * 1024 * 1024

ROW_GROUP = SUBLANES * SUBLANES
TAIL_VREG_ROWS = 3
TAIL_ROWS = TAIL_VREG_ROWS * SUBLANES

TIME_BLOCK = 512
SC_TIME_BLOCK = 1024
SUB_ROWS = 256
SCAN_CHUNK = 128
NEG_BIG = -1e30

BF16 = jnp.bfloat16
F32 = jnp.float32


def _rms_norm(x, g):
    ms = jnp.mean(x * x, axis=-1, keepdims=True)
    return x * lax.rsqrt(ms + EPS) * g


def _silu(x):
    hx = 0.5 * x
    return hx * jnp.tanh(hx) + hx


def _softplus(x):
    return jnp.maximum(x, 0.0) + jnp.log1p(jnp.exp(-jnp.abs(x)))


def _dot(a, b):
    return jnp.dot(a, b, preferred_element_type=F32)


def _load_permuted(x_ref, stage_ref, r0, rows, from_time_order):
    if not from_time_order:
        return x_ref[0, pl.ds(r0, rows), :]
    d = x_ref.shape[2]
    for j in range(d // LANES):
        stage_ref[j, pl.ds(r0, rows), :] = x_ref[0, pl.ds(r0, rows), j * LANES:(j + 1) * LANES]
    cols = []
    for j in range(d // LANES):
        tiles = [stage_ref[j, pl.ds(r0 + g * ROW_GROUP + r, SUBLANES, stride=SUBLANES), :]
                 for g in range(rows // ROW_GROUP) for r in range(SUBLANES)]
        cols.append(jnp.concatenate(tiles, axis=0))
    return jnp.concatenate(cols, axis=1)


def _store_rows(o_ref, stage_ref, r0, val, to_time_order):
    rows, d = val.shape
    if not to_time_order:
        o_ref[0, pl.ds(r0, rows), :] = val
        return
    for j in range(d // LANES):
        for g in range(rows // ROW_GROUP):
            for r in range(SUBLANES):
                p0 = g * ROW_GROUP + r * SUBLANES
                stage_ref[j, pl.ds(r0 + g * ROW_GROUP + r, SUBLANES, stride=SUBLANES), :] = (
                    val[p0:p0 + SUBLANES, j * LANES:(j + 1) * LANES])
    for j in range(d // LANES):
        o_ref[0, pl.ds(r0, rows), j * LANES:(j + 1) * LANES] = stage_ref[j, pl.ds(r0, rows), :]


def _time_of_position(p):
    return ((p >> 6) << 6) + ((p & 7) << 3) + ((p >> 3) & 7)


def _shifted(cur, prev_tail, k_max):
    rows = cur.shape[0]
    sub = lax.broadcasted_iota(jnp.int32, (SUBLANES, cur.shape[1]), 0)
    last = sub == SUBLANES - 1
    first_q = SUBLANES - TAIL_VREG_ROWS
    out = [[] for _ in range(k_max)]
    for g in range(rows // ROW_GROUP):
        base = g * ROW_GROUP
        prev = prev_tail if g == 0 else cur[base - TAIL_ROWS:base]
        rolled = {}
        for q in range(SUBLANES - k_max, SUBLANES):
            cur_q = cur[base + q * SUBLANES:base + (q + 1) * SUBLANES]
            prev_q = prev[(q - first_q) * SUBLANES:(q - first_q + 1) * SUBLANES]
            rolled[q] = pltpu.roll(jnp.where(last, prev_q, cur_q), 1, axis=0)
        for k in range(1, k_max + 1):
            pieces = [rolled[q] for q in range(SUBLANES - k, SUBLANES)]
            pieces.append(cur[base:base + (SUBLANES - k) * SUBLANES])
            out[k - 1].append(jnp.concatenate(pieces, axis=0))
    return [jnp.concatenate(o, axis=0) for o in out]


def _causal_conv(cur, prev_tail, w, b=None):
    k = w.shape[0]
    out = cur * w[k - 1:k]
    for s, shifted in enumerate(_shifted(cur, prev_tail, k - 1), start=1):
        out = out + shifted * w[k - 1 - s:k - s]
    if b is not None:
        out = out + b
    return out


def _weight_scratch(rows, cols):
    return pltpu.VMEM((rows, cols + LANES), BF16)


def _weight_stage(row_chunk, cols):
    return pltpu.VMEM((2, row_chunk, cols), F32)


def _stream_weight(src, dst_ref, stage_ref, sem):
    rows, cols = src.shape
    row_chunk = stage_ref.shape[1]
    n = rows // row_chunk
    if cols % LANES:
        last_tile = pl.ds(cols // LANES * LANES, LANES)
        dst_ref[:, last_tile] = jnp.zeros((rows, LANES), BF16)

    def copy(i):
        return pltpu.make_async_copy(src.at[pl.ds(i * row_chunk, row_chunk), :],
                                     stage_ref.at[i % 2], sem.at[i % 2])

    copy(0).start()
    for i in range(n):
        if i + 1 < n:
            copy(i + 1).start()
        copy(i).wait()
        dst_ref[pl.ds(i * row_chunk, row_chunk), pl.ds(0, cols)] = stage_ref[i % 2].astype(BF16)


def _is_first_step():
    return jnp.logical_and(pl.program_id(0) == 0, pl.program_id(1) == 0)


_HBM_SPEC = pl.BlockSpec(memory_space=pl.ANY)
_DMA_SEMS = pltpu.SemaphoreType.DMA((2,))


def _const_spec(shape):
    return pl.BlockSpec(shape, lambda b, t: (0,) * len(shape), pipeline_mode=pl.Buffered(1))


def _x_spec(time_block=TIME_BLOCK):
    return pl.BlockSpec((1, time_block, D_MODEL), lambda b, t: (b, t, 0))


def _stage_scratch(time_block=TIME_BLOCK):
    return pltpu.VMEM((D_MODEL // LANES, time_block, LANES), F32)


def _compiler_params():
    return pltpu.CompilerParams(
        dimension_semantics=("arbitrary", "arbitrary"),
        vmem_limit_bytes=VMEM_LIMIT_BYTES,
    )


def _ffn_kernel(layer, is_last, x_ref, gpre_ref, wup_hbm, cw_ref, cb_ref, wdown_hbm, gpost_ref,
                o_ref, wup_ref, wdown_ref, upstage_ref, downstage_ref, wsem,
                tail_ref, hid_ref, xstage_ref, ostage_ref):
    t = pl.program_id(1)

    @pl.when(_is_first_step())
    def _():
        _stream_weight(wup_hbm.at[layer], wup_ref, upstage_ref, wsem)
        _stream_weight(wdown_hbm.at[layer], wdown_ref, downstage_ref, wsem)

    @pl.when(t == 0)
    def _():
        tail_ref[...] = jnp.zeros_like(tail_ref)

    n_chunks = FFN_HIDDEN // MXU_COLS
    tails = [tail_ref[:, pl.ds(c * MXU_COLS, MXU_COLS)] for c in range(n_chunks)]
    for sb in range(TIME_BLOCK // SUB_ROWS):
        r0 = sb * SUB_ROWS
        x = _load_permuted(x_ref, xstage_ref, r0, SUB_ROWS, False)
        hb = _rms_norm(x, gpre_ref[...]).astype(BF16)
        for c in range(n_chunks):
            cols = pl.ds(c * MXU_COLS, MXU_COLS)
            gate = _dot(hb, wup_ref[:, cols])
            val = _dot(hb, wup_ref[:, pl.ds(FFN_HIDDEN + c * MXU_COLS, MXU_COLS)])
            conv = _causal_conv(gate, tails[c], cw_ref[:, cols], cb_ref[:, cols])
            tails[c] = gate[SUB_ROWS - TAIL_ROWS:]
            hid_ref[pl.ds(r0, SUB_ROWS), cols] = (_silu(conv) * val).astype(BF16)
        f = _dot(hid_ref[pl.ds(r0, SUB_ROWS), :], wdown_ref[:, pl.ds(0, D_MODEL)])
        _store_rows(o_ref, ostage_ref, r0, x + _rms_norm(f, gpost_ref[...]), is_last)
    for c in range(n_chunks):
        tail_ref[:, pl.ds(c * MXU_COLS, MXU_COLS)] = tails[c]


def _ffn_layer(layer, is_last, x, gpre, wup, cw, cb, wdown, gpost):
    bsz, length, _ = x.shape
    return pl.pallas_call(
        functools.partial(_ffn_kernel, layer, is_last),
        name="conv_ffn",
        grid=(bsz, length // TIME_BLOCK),
        in_specs=[
            _x_spec(),
            _const_spec(gpre.shape),
            _HBM_SPEC,
            _const_spec(cw.shape),
            _const_spec(cb.shape),
            _HBM_SPEC,
            _const_spec(gpost.shape),
        ],
        out_specs=_x_spec(),
        out_shape=jax.ShapeDtypeStruct(x.shape, x.dtype),
        scratch_shapes=[
            pltpu.VMEM((D_MODEL, 2 * FFN_HIDDEN), BF16),
            _weight_scratch(FFN_HIDDEN, D_MODEL),
            _weight_stage(64, 2 * FFN_HIDDEN),
            _weight_stage(256, D_MODEL),
            _DMA_SEMS,
            pltpu.VMEM((TAIL_ROWS, FFN_HIDDEN), F32),
            pltpu.VMEM((TIME_BLOCK, FFN_HIDDEN), BF16),
            _stage_scratch(),
            _stage_scratch(),
        ],
        compiler_params=_compiler_params(),
    )(x, gpre, wup, cw, cb, wdown, gpost)


def _sc_kernel(layer, x_ref, gpre_ref, win_hbm, cw_ref, wout_hbm, gpost_ref, o_ref,
               win_ref, wout_ref, instage_ref, outstage_ref, wsem,
               tail_ref, xstage_ref, ostage_ref):
    t = pl.program_id(1)

    @pl.when(_is_first_step())
    def _():
        _stream_weight(win_hbm.at[layer], win_ref, instage_ref, wsem)
        _stream_weight(wout_hbm.at[layer], wout_ref, outstage_ref, wsem)

    @pl.when(t == 0)
    def _():
        tail_ref[...] = jnp.zeros_like(tail_ref)

    n_chunks = SC_WIDTH // MXU_COLS
    tails = [tail_ref[:, pl.ds(c * MXU_COLS, MXU_COLS)] for c in range(n_chunks)]
    for sb in range(SC_TIME_BLOCK // SUB_ROWS):
        r0 = sb * SUB_ROWS
        x = _load_permuted(x_ref, xstage_ref, r0, SUB_ROWS, False)
        hb = _rms_norm(x, gpre_ref[...]).astype(BF16)
        parts = []
        for c in range(n_chunks):
            cols = pl.ds(c * MXU_COLS, MXU_COLS)
            gb = _dot(hb, win_ref[:, cols])
            gc = _dot(hb, win_ref[:, pl.ds(SC_WIDTH + c * MXU_COLS, MXU_COLS)])
            v = _dot(hb, win_ref[:, pl.ds(2 * SC_WIDTH + c * MXU_COLS, MXU_COLS)])
            gcv = gc * v
            u = _causal_conv(gcv, tails[c], cw_ref[:, cols])
            tails[c] = gcv[SUB_ROWS - TAIL_ROWS:]
            parts.append((gb * u).astype(BF16))
        m = _dot(jnp.concatenate(parts, axis=1), wout_ref[:, pl.ds(0, D_MODEL)])
        _store_rows(o_ref, ostage_ref, r0, x + _rms_norm(m, gpost_ref[...]), False)
    for c in range(n_chunks):
        tail_ref[:, pl.ds(c * MXU_COLS, MXU_COLS)] = tails[c]


def _sc_layer(layer, x, gpre, win, cw, wout, gpost):
    bsz, length, _ = x.shape
    return pl.pallas_call(
        functools.partial(_sc_kernel, layer),
        name="shortconv_mixer",
        grid=(bsz, length // SC_TIME_BLOCK),
        in_specs=[
            _x_spec(SC_TIME_BLOCK),
            _const_spec(gpre.shape),
            _HBM_SPEC,
            _const_spec(cw.shape),
            _HBM_SPEC,
            _const_spec(gpost.shape),
        ],
        out_specs=_x_spec(SC_TIME_BLOCK),
        out_shape=jax.ShapeDtypeStruct(x.shape, x.dtype),
        scratch_shapes=[
            _weight_scratch(D_MODEL, 3 * SC_WIDTH),
            _weight_scratch(SC_WIDTH, D_MODEL),
            _weight_stage(128, 3 * SC_WIDTH),
            _weight_stage(256, D_MODEL),
            _DMA_SEMS,
            pltpu.VMEM((TAIL_ROWS, SC_WIDTH), F32),
            _stage_scratch(SC_TIME_BLOCK),
            _stage_scratch(SC_TIME_BLOCK),
        ],
        compiler_params=_compiler_params(),
    )(x, gpre, win, cw, wout, gpost)


def _split_bf16(v):
    hi = v.astype(BF16)
    lo = (v - hi.astype(F32)).astype(BF16)
    return jnp.concatenate([hi, lo], axis=1)


def _head_expander():
    k = lax.broadcasted_iota(jnp.int32, (2 * LANES, SSD_D_INNER), 0) % LANES
    c = lax.broadcasted_iota(jnp.int32, (2 * LANES, SSD_D_INNER), 1) // SSD_HEAD_DIM
    return jnp.where(k == c, 1.0, 0.0).astype(BF16)


def _ssd_chunk_prelude(expand_ref, dt_c, a_c, tril):
    tc = SCAN_CHUNK
    acs = jnp.dot(tril, a_c, precision=lax.Precision.HIGHEST,
                  preferred_element_type=F32)
    src_t = (acs - jnp.log(dt_c)).T
    total = acs[tc - 1:tc]
    e_in = jnp.exp(acs)
    w_out = jnp.exp(total - acs) * dt_c
    e_in_x = _dot(_split_bf16(e_in), expand_ref[...])
    w_out_x = _dot(_split_bf16(w_out), expand_ref[...])
    return acs, src_t, e_in_x, w_out_x


def _ssd_scan_chunk(xbc_ref, y_ref, state_ref, dskip_ref, r0, prelude, causal, gcol, mid_chunk):
    tc = SCAN_CHUNK
    acs, src_t, e_in_x, w_out_x = prelude

    def start(g):
        xg = xbc_ref[pl.ds(r0, tc), pl.ds(g * GROUP_COLS, GROUP_COLS)]
        bg = xbc_ref[pl.ds(r0, tc), pl.ds(SSD_D_INNER + g * SSD_D_STATE, SSD_D_STATE)].astype(BF16)
        cg = xbc_ref[pl.ds(r0, tc),
                     pl.ds(SSD_D_INNER + SSD_BC_DIM + g * SSD_D_STATE, SSD_D_STATE)].astype(BF16)
        cb_mat = lax.dot_general(cg, bg, (((1,), (1,)), ((), ())),
                                 preferred_element_type=F32)
        return xg, bg, cg, cb_mat

    started = start(0)
    for g in range(SSD_N_GROUPS):
        xg, bg, cg, cb_mat = started
        if g + 1 < SSD_N_GROUPS:
            started = start(g + 1)
        if g == SSD_N_GROUPS // 2:
            mid_chunk()
        h0 = g * SSD_HEADS_PER_GROUP
        xcols = pl.ds(g * GROUP_COLS, GROUP_COLS)
        m_parts = []
        x_parts = []
        for j in range(SSD_HEADS_PER_GROUP):
            h = h0 + j
            seg = acs[:, h:h + 1] - src_t[h:h + 1, :]
            lmat = jnp.exp(jnp.where(causal, seg, NEG_BIG))
            m_parts.append((cb_mat * lmat).astype(BF16))
            x_parts.append(jnp.where(gcol == j, xg, 0.0).astype(BF16))
        m_cat = jnp.concatenate(m_parts, axis=1)
        x_bd = jnp.concatenate(x_parts, axis=0)
        y_diag = _dot(m_cat, x_bd)

        e_in_g = e_in_x[:, g * GROUP_COLS:(g + 1) * GROUP_COLS]
        w_out_g = w_out_x[:, g * GROUP_COLS:(g + 1) * GROUP_COLS]
        e_tot_g = e_in_g[tc - 1:tc]
        state = state_ref[g]
        y_off = _dot(cg, state.astype(BF16)) * e_in_g
        y_ref[pl.ds(r0, tc), xcols] = y_diag + y_off + xg * dskip_ref[:, xcols]
        upd = lax.dot_general(bg, (xg * w_out_g).astype(BF16), (((0,), (0,)), ((), ())),
                              preferred_element_type=F32)
        state_ref[g] = state * e_tot_g + upd
        yield


_DONE = object()


def _run(task):
    for _ in task:
        pass


def _chain(*tasks):
    for task in tasks:
        yield from task


def _interleave(primary, secondary):
    primary_live = secondary_live = True
    while primary_live or secondary_live:
        if primary_live:
            primary_live = next(primary, _DONE) is not _DONE
        if secondary_live:
            secondary_live = next(secondary, _DONE) is not _DONE


def _ssd_kernel(layer, is_first, x_ref, gpre_ref, win_hbm, cw_ref, cb_ref, dtb_ref, alog_ref,
                dskip_ref, normw_ref, wout_hbm, gpost_ref,
                o_ref, wz_ref, wxbc_ref, wout_ref, zstage_ref, xbcstage_ref, outstage_ref, wsem,
                expand_ref, tail_ref, state_ref, xbc_ref, y_ref, xstage_ref, ostage_ref):
    t = pl.program_id(1)

    @pl.when(_is_first_step())
    def _():
        w_in = win_hbm.at[layer]
        _stream_weight(w_in.at[:, pl.ds(SSD_D_INNER, SSD_CONV_DIM + SSD_N_HEADS)], wxbc_ref,
                       xbcstage_ref, wsem)
        _stream_weight(w_in.at[:, pl.ds(0, SSD_D_INNER)], wz_ref, zstage_ref, wsem)
        _stream_weight(wout_hbm.at[layer], wout_ref, outstage_ref, wsem)
        expand_ref[...] = _head_expander()

    @pl.when(t == 0)
    def _():
        tail_ref[...] = jnp.zeros_like(tail_ref)
        state_ref[...] = jnp.zeros_like(state_ref)

    tc = SCAN_CHUNK
    time_l = _time_of_position(lax.broadcasted_iota(jnp.int32, (tc, tc), 0))
    time_s = _time_of_position(lax.broadcasted_iota(jnp.int32, (tc, tc), 1))
    causal = time_l >= time_s
    tril = causal.astype(F32)
    gcol = lax.broadcasted_iota(jnp.int32, (tc, GROUP_COLS), 1) // SSD_HEAD_DIM

    n_chunks = SSD_CONV_DIM // MXU_COLS
    n_sub = TIME_BLOCK // SUB_ROWS
    tails = [tail_ref[:, pl.ds(c * MXU_COLS, MXU_COLS)] for c in range(n_chunks)]
    ctx = [dict() for _ in range(n_sub)]

    def in_proj(sb):
        r0 = sb * SUB_ROWS
        x = _load_permuted(x_ref, xstage_ref, r0, SUB_ROWS, is_first)
        hb = _rms_norm(x, gpre_ref[...]).astype(BF16)
        ctx[sb].update(x=x, hb=hb)
        for c in range(n_chunks):
            cols = pl.ds(c * MXU_COLS, MXU_COLS)
            raw = _dot(hb, wxbc_ref[:, cols])
            conv = _causal_conv(raw, tails[c], cw_ref[:, cols], cb_ref[:, cols])
            tails[c] = raw[SUB_ROWS - TAIL_ROWS:]
            xbc_ref[pl.ds(r0, SUB_ROWS), cols] = _silu(conv)
            yield
        dt_raw = _dot(hb, wxbc_ref[:, pl.ds(SSD_CONV_DIM, LANES)])
        dt = _softplus(dt_raw + dtb_ref[...])
        ctx[sb].update(dt=dt, a=dt * -jnp.exp(alog_ref[...]))
        yield

    def scan(sb):
        dt, a = ctx[sb]["dt"], ctx[sb]["a"]
        n = SUB_ROWS // tc
        preludes = {0: _ssd_chunk_prelude(expand_ref, dt[:tc], a[:tc], tril)}
        for k in range(n):
            def mid_chunk(k=k):
                if k + 1 < n:
                    rows = slice((k + 1) * tc, (k + 2) * tc)
                    preludes[k + 1] = _ssd_chunk_prelude(expand_ref, dt[rows], a[rows], tril)
            yield from _ssd_scan_chunk(xbc_ref, y_ref, state_ref, dskip_ref, sb * SUB_ROWS + k * tc,
                                       preludes.pop(k), causal, gcol, mid_chunk)

    def out_proj(sb):
        r0 = sb * SUB_ROWS
        sub = pl.ds(r0, SUB_ROWS)
        x, hb = ctx[sb]["x"], ctx[sb]["hb"]
        sumsq = jnp.zeros((SUB_ROWS, 1), F32)
        for c in range(SSD_D_INNER // MXU_COLS):
            cols = pl.ds(c * MXU_COLS, MXU_COLS)
            gated = y_ref[sub, cols] * _silu(_dot(hb, wz_ref[:, cols]))
            sumsq = sumsq + jnp.sum(gated * gated, axis=-1, keepdims=True)
            y_ref[sub, cols] = gated
            yield
        scale = lax.rsqrt(sumsq * (1.0 / SSD_D_INNER) + EPS)
        yn = (y_ref[sub, :] * scale * normw_ref[...]).astype(BF16)
        parts = []
        for c in range(D_MODEL // MXU_COLS):
            parts.append(_dot(yn, wout_ref[:, pl.ds(c * MXU_COLS, MXU_COLS)]))
            yield
        m = jnp.concatenate(parts, axis=1)
        _store_rows(o_ref, ostage_ref, r0, x + _rms_norm(m, gpost_ref[...]), False)

    _run(in_proj(0))
    for sb in range(n_sub):
        fill = []
        if sb > 0:
            fill.append(out_proj(sb - 1))
        if sb + 1 < n_sub:
            fill.append(in_proj(sb + 1))
        _interleave(scan(sb), _chain(*fill))
    _run(out_proj(n_sub - 1))
    for c in range(n_chunks):
        tail_ref[:, pl.ds(c * MXU_COLS, MXU_COLS)] = tails[c]


def _ssd_layer(layer, is_first, x, gpre, win, cw, cb, dtb, alog, dskip, normw, wout, gpost):
    bsz, length, _ = x.shape
    small = (cw, cb, dtb, alog, dskip, normw)
    return pl.pallas_call(
        functools.partial(_ssd_kernel, layer, is_first),
        name="ssd_mixer",
        grid=(bsz, length // TIME_BLOCK),
        in_specs=([_x_spec(), _const_spec(gpre.shape), _HBM_SPEC]
                  + [_const_spec(c.shape) for c in small] + [_HBM_SPEC, _const_spec(gpost.shape)]),
        out_specs=_x_spec(),
        out_shape=jax.ShapeDtypeStruct(x.shape, x.dtype),
        scratch_shapes=[
            _weight_scratch(D_MODEL, SSD_D_INNER),
            _weight_scratch(D_MODEL, SSD_CONV_DIM),
            _weight_scratch(SSD_D_INNER, D_MODEL),
            _weight_stage(128, SSD_D_INNER),
            _weight_stage(64, SSD_CONV_DIM + SSD_N_HEADS),
            _weight_stage(256, D_MODEL),
            _DMA_SEMS,
            pltpu.VMEM((2 * LANES, SSD_D_INNER), BF16),
            pltpu.VMEM((TAIL_ROWS, SSD_CONV_DIM), F32),
            pltpu.VMEM((SSD_N_GROUPS, SSD_D_STATE, GROUP_COLS), F32),
            pltpu.VMEM((TIME_BLOCK, SSD_CONV_DIM), F32),
            pltpu.VMEM((TIME_BLOCK, SSD_D_INNER), F32),
            _stage_scratch(),
            _stage_scratch(),
        ],
        compiler_params=_compiler_params(),
    )(x, gpre, win, *small, wout, gpost)


def _pad_lanes(v):
    return jnp.pad(v, ((0, 0), (0, LANES - v.shape[-1])))


def kernel(x, mix_pre_g, mix_post_g, ffn_pre_g, ffn_post_g, ssd_w_in, ssd_conv_w, ssd_conv_b,
           ssd_dt_bias, ssd_A_log, ssd_D, ssd_norm_w, ssd_w_out, sc_w_in, sc_conv_w, sc_w_out,
           ffn_w_up, ffn_conv_w, ffn_conv_b, ffn_w_down):
    depth = mix_pre_g.shape[0]
    row = lambda v: v.reshape(1, -1)
    for i in range(depth):
        j = i // 2
        if i % 2 == 0:
            dskip = jnp.repeat(ssd_D[j], SSD_HEAD_DIM).reshape(1, -1)
            x = _ssd_layer(j, i == 0, x, row(mix_pre_g[i]), ssd_w_in, ssd_conv_w[j], row(ssd_conv_b[j]),
                           _pad_lanes(row(ssd_dt_bias[j])), _pad_lanes(row(ssd_A_log[j])), dskip,
                           row(ssd_norm_w[j]), ssd_w_out, row(mix_post_g[i]))
        else:
            x = _sc_layer(j, x, row(mix_pre_g[i]), sc_w_in, sc_conv_w[j], sc_w_out,
                          row(mix_post_g[i]))
        x = _ffn_layer(i, i == depth - 1, x, row(ffn_pre_g[i]), ffn_w_up, ffn_conv_w[i], row(ffn_conv_b[i]),
                       ffn_w_down, row(ffn_post_g[i]))
    return x
```

```python
import functools

import jax
import jax.numpy as jnp
from jax import lax
from jax.experimental import pallas as pl
from jax.experimental.pallas import tpu as pltpu

EPS = 1e-6
D_MODEL = 1024

SSD_D_INNER = 2048
SSD_HEAD_DIM = 64
SSD_N_HEADS = 32
SSD_N_GROUPS = 8
SSD_HEADS_PER_GROUP = 4
SSD_D_STATE = 128
SSD_CONV_W = 4
SSD_BC_DIM = SSD_N_GROUPS * SSD_D_STATE
SSD_CONV_DIM = SSD_D_INNER + 2 * SSD_BC_DIM
GROUP_COLS = SSD_HEADS_PER_GROUP * SSD_HEAD_DIM

SC_WIDTH = 1024
SC_CONV_W = 3
FFN_HIDDEN = 2816
FFN_CONV_W = 3

LANES = 128
SUBLANES = 8
MXU_COLS = 256
VMEM_LIMIT_BYTES = 60 * 1024 * 1024

ROW_GROUP = SUBLANES * SUBLANES
TAIL_VREG_ROWS = 3
TAIL_ROWS = TAIL_VREG_ROWS * SUBLANES

TIME_BLOCK = 512
SC_TIME_BLOCK = 1024
SUB_ROWS = 256
SCAN_CHUNK = 128
NEG_BIG = -1e30

BF16 = jnp.bfloat16
F32 = jnp.float32


def _rms_norm(x, g):
    ms = jnp.mean(x * x, axis=-1, keepdims=True)
    return x * lax.rsqrt(ms + EPS) * g


def _silu(x):
    hx = 0.5 * x
    return hx * jnp.tanh(hx) + hx


def _softplus(x):
    return jnp.maximum(x, 0.0) + jnp.log1p(jnp.exp(-jnp.abs(x)))


def _dot(a, b):
    return jnp.dot(a, b, preferred_element_type=F32)


def _load_permuted(x_ref, stage_ref, r0, rows, from_time_order):
    if not from_time_order:
        return x_ref[0, pl.ds(r0, rows), :]
    d = x_ref.shape[2]
    for j in range(d // LANES):
        stage_ref[j, pl.ds(r0, rows), :] = x_ref[0, pl.ds(r0, rows), j * LANES:(j + 1) * LANES]
    cols = []
    for j in range(d // LANES):
        tiles = [stage_ref[j, pl.ds(r0 + g * ROW_GROUP + r, SUBLANES, stride=SUBLANES), :]
                 for g in range(rows // ROW_GROUP) for r in range(SUBLANES)]
        cols.append(jnp.concatenate(tiles, axis=0))
    return jnp.concatenate(cols, axis=1)


def _store_rows(o_ref, stage_ref, r0, val, to_time_order):
    rows, d = val.shape
    if not to_time_order:
        o_ref[0, pl.ds(r0, rows), :] = val
        return
    for j in range(d // LANES):
        for g in range(rows // ROW_GROUP):
            for r in range(SUBLANES):
                p0 = g * ROW_GROUP + r * SUBLANES
                stage_ref[j, pl.ds(r0 + g * ROW_GROUP + r, SUBLANES, stride=SUBLANES), :] = (
                    val[p0:p0 + SUBLANES, j * LANES:(j + 1) * LANES])
    for j in range(d // LANES):
        o_ref[0, pl.ds(r0, rows), j * LANES:(j + 1) * LANES] = stage_ref[j, pl.ds(r0, rows), :]


def _time_of_position(p):
    return ((p >> 6) << 6) + ((p & 7) << 3) + ((p >> 3) & 7)


def _shifted(cur, prev_tail, k_max):
    rows = cur.shape[0]
    sub = lax.broadcasted_iota(jnp.int32, (SUBLANES, cur.shape[1]), 0)
    last = sub == SUBLANES - 1
    first_q = SUBLANES - TAIL_VREG_ROWS
    out = [[] for _ in range(k_max)]
    for g in range(rows // ROW_GROUP):
        base = g * ROW_GROUP
        prev = prev_tail if g == 0 else cur[base - TAIL_ROWS:base]
        rolled = {}
        for q in range(SUBLANES - k_max, SUBLANES):
            cur_q = cur[base + q * SUBLANES:base + (q + 1) * SUBLANES]
            prev_q = prev[(q - first_q) * SUBLANES:(q - first_q + 1) * SUBLANES]
            rolled[q] = pltpu.roll(jnp.where(last, prev_q, cur_q), 1, axis=0)
        for k in range(1, k_max + 1):
            pieces = [rolled[q] for q in range(SUBLANES - k, SUBLANES)]
            pieces.append(cur[base:base + (SUBLANES - k) * SUBLANES])
            out[k - 1].append(jnp.concatenate(pieces, axis=0))
    return [jnp.concatenate(o, axis=0) for o in out]


def _causal_conv(cur, prev_tail, w, b=None):
    k = w.shape[0]
    out = cur * w[k - 1:k]
    for s, shifted in enumerate(_shifted(cur, prev_tail, k - 1), start=1):
        out = out + shifted * w[k - 1 - s:k - s]
    if b is not None:
        out = out + b
    return out


def _weight_scratch(rows, cols):
    return pltpu.VMEM((rows, cols + LANES), BF16)


def _weight_stage(row_chunk, cols):
    return pltpu.VMEM((2, row_chunk, cols), F32)


def _stream_weight(src, dst_ref, stage_ref, sem):
    rows, cols = src.shape
    row_chunk = stage_ref.shape[1]
    n = rows // row_chunk
    if cols % LANES:
        last_tile = pl.ds(cols // LANES * LANES, LANES)
        dst_ref[:, last_tile] = jnp.zeros((rows, LANES), BF16)

    def copy(i):
        return pltpu.make_async_copy(src.at[pl.ds(i * row_chunk, row_chunk), :],
                                     stage_ref.at[i % 2], sem.at[i % 2])

    copy(0).start()
    for i in range(n):
        if i + 1 < n:
            copy(i + 1).start()
        copy(i).wait()
        dst_ref[pl.ds(i * row_chunk, row_chunk), pl.ds(0, cols)] = stage_ref[i % 2].astype(BF16)


def _stream_weight_transposed(src, dst_ref, stage_ref, sem):
    n_out, _ = src.shape
    chunk = stage_ref.shape[1]
    chunks = [(start, min(chunk, n_out - start)) for start in range(0, n_out, chunk)]

    def padded(size):
        return -(-size // LANES) * LANES

    def copy(i):
        start, size = chunks[i]
        return pltpu.make_async_copy(src.at[pl.ds(start, size), :],
                                     stage_ref.at[i % 2, pl.ds(0, size), :], sem.at[i % 2])

    def start_copy(i):
        size = chunks[i][1]
        if size < padded(size):
            stage_ref[i % 2, pl.ds(size, padded(size) - size), :] = jnp.zeros(
                (padded(size) - size, stage_ref.shape[2]), F32)
        copy(i).start()

    start_copy(0)
    for i in range(len(chunks)):
        if i + 1 < len(chunks):
            start_copy(i + 1)
        copy(i).wait()
        start, size = chunks[i]
        cols = padded(size)
        dst_ref[:, pl.ds(start, cols)] = stage_ref[i % 2, pl.ds(0, cols), :].T.astype(BF16)


def _is_first_step():
    return jnp.logical_and(pl.program_id(0) == 0, pl.program_id(1) == 0)


_HBM_SPEC = pl.BlockSpec(memory_space=pl.ANY)
_DMA_SEMS = pltpu.SemaphoreType.DMA((2,))


def _const_spec(shape):
    return pl.BlockSpec(shape, lambda b, t: (0,) * len(shape), pipeline_mode=pl.Buffered(1))


def _x_spec(time_block=TIME_BLOCK):
    return pl.BlockSpec((1, time_block, D_MODEL), lambda b, t: (b, t, 0))


def _stage_scratch(time_block=TIME_BLOCK):
    return pltpu.VMEM((D_MODEL // LANES, time_block, LANES), F32)


def _compiler_params():
    return pltpu.CompilerParams(
        dimension_semantics=("arbitrary", "arbitrary"),
        vmem_limit_bytes=VMEM_LIMIT_BYTES,
    )


def _ffn_kernel(layer, is_last, x_ref, gpre_ref, wup_hbm, cw_ref, cb_ref, wdown_hbm, gpost_ref,
                o_ref, wup_ref, wdown_ref, upstage_ref, downstage_ref, wsem,
                tail_ref, hid_ref, xstage_ref, ostage_ref):
    t = pl.program_id(1)

    @pl.when(_is_first_step())
    def _():
        _stream_weight(wup_hbm.at[layer], wup_ref, upstage_ref, wsem)
        _stream_weight(wdown_hbm.at[layer], wdown_ref, downstage_ref, wsem)

    @pl.when(t == 0)
    def _():
        tail_ref[...] = jnp.zeros_like(tail_ref)

    n_chunks = FFN_HIDDEN // MXU_COLS
    tails = [tail_ref[:, pl.ds(c * MXU_COLS, MXU_COLS)] for c in range(n_chunks)]
    for sb in range(TIME_BLOCK // SUB_ROWS):
        r0 = sb * SUB_ROWS
        x = _load_permuted(x_ref, xstage_ref, r0, SUB_ROWS, False)
        hb = _rms_norm(x, gpre_ref[...]).astype(BF16)
        for c in range(n_chunks):
            cols = pl.ds(c * MXU_COLS, MXU_COLS)
            gate = _dot(hb, wup_ref[:, cols])
            val = _dot(hb, wup_ref[:, pl.ds(FFN_HIDDEN + c * MXU_COLS, MXU_COLS)])
            conv = _causal_conv(gate, tails[c], cw_ref[:, cols], cb_ref[:, cols])
            tails[c] = gate[SUB_ROWS - TAIL_ROWS:]
            hid_ref[pl.ds(r0, SUB_ROWS), cols] = (_silu(conv) * val).astype(BF16)
        f = _dot(hid_ref[pl.ds(r0, SUB_ROWS), :], wdown_ref[:, pl.ds(0, D_MODEL)])
        _store_rows(o_ref, ostage_ref, r0, x + _rms_norm(f, gpost_ref[...]), is_last)
    for c in range(n_chunks):
        tail_ref[:, pl.ds(c * MXU_COLS, MXU_COLS)] = tails[c]


def _ffn_layer(layer, is_last, x, gpre, wup, cw, cb, wdown, gpost):
    bsz, length, _ = x.shape
    return pl.pallas_call(
        functools.partial(_ffn_kernel, layer, is_last),
        name="conv_ffn",
        grid=(bsz, length // TIME_BLOCK),
        in_specs=[
            _x_spec(),
            _const_spec(gpre.shape),
            _HBM_SPEC,
            _const_spec(cw.shape),
            _const_spec(cb.shape),
            _HBM_SPEC,
            _const_spec(gpost.shape),
        ],
        out_specs=_x_spec(),
        out_shape=jax.ShapeDtypeStruct(x.shape, x.dtype),
        scratch_shapes=[
            pltpu.VMEM((D_MODEL, 2 * FFN_HIDDEN), BF16),
            _weight_scratch(FFN_HIDDEN, D_MODEL),
            _weight_stage(64, 2 * FFN_HIDDEN),
            _weight_stage(256, D_MODEL),
            _DMA_SEMS,
            pltpu.VMEM((TAIL_ROWS, FFN_HIDDEN), F32),
            pltpu.VMEM((TIME_BLOCK, FFN_HIDDEN), BF16),
            _stage_scratch(),
            _stage_scratch(),
        ],
        compiler_params=_compiler_params(),
    )(x, gpre, wup, cw, cb, wdown, gpost)


def _sc_kernel(layer, x_ref, gpre_ref, win_hbm, cw_ref, wout_hbm, gpost_ref, o_ref,
               win_ref, wout_ref, instage_ref, outstage_ref, wsem,
               tail_ref, xstage_ref, ostage_ref):
    t = pl.program_id(1)

    @pl.when(_is_first_step())
    def _():
        _stream_weight(win_hbm.at[layer], win_ref, instage_ref, wsem)
        _stream_weight(wout_hbm.at[layer], wout_ref, outstage_ref, wsem)

    @pl.when(t == 0)
    def _():
        tail_ref[...] = jnp.zeros_like(tail_ref)

    n_chunks = SC_WIDTH // MXU_COLS
    tails = [tail_ref[:, pl.ds(c * MXU_COLS, MXU_COLS)] for c in range(n_chunks)]
    for sb in range(SC_TIME_BLOCK // SUB_ROWS):
        r0 = sb * SUB_ROWS
        x = _load_permuted(x_ref, xstage_ref, r0, SUB_ROWS, False)
        hb = _rms_norm(x, gpre_ref[...]).astype(BF16)
        parts = []
        for c in range(n_chunks):
            cols = pl.ds(c * MXU_COLS, MXU_COLS)
            gb = _dot(hb, win_ref[:, cols])
            gc = _dot(hb, win_ref[:, pl.ds(SC_WIDTH + c * MXU_COLS, MXU_COLS)])
            v = _dot(hb, win_ref[:, pl.ds(2 * SC_WIDTH + c * MXU_COLS, MXU_COLS)])
            gcv = gc * v
            u = _causal_conv(gcv, tails[c], cw_ref[:, cols])
            tails[c] = gcv[SUB_ROWS - TAIL_ROWS:]
            parts.append((gb * u).astype(BF16))
        m = _dot(jnp.concatenate(parts, axis=1), wout_ref[:, pl.ds(0, D_MODEL)])
        _store_rows(o_ref, ostage_ref, r0, x + _rms_norm(m, gpost_ref[...]), False)
    for c in range(n_chunks):
        tail_ref[:, pl.ds(c * MXU_COLS, MXU_COLS)] = tails[c]


def _sc_layer(layer, x, gpre, win, cw, wout, gpost):
    bsz, length, _ = x.shape
    return pl.pallas_call(
        functools.partial(_sc_kernel, layer),
        name="shortconv_mixer",
        grid=(bsz, length // SC_TIME_BLOCK),
        in_specs=[
            _x_spec(SC_TIME_BLOCK),
            _const_spec(gpre.shape),
            _HBM_SPEC,
            _const_spec(cw.shape),
            _HBM_SPEC,
            _const_spec(gpost.shape),
        ],
        out_specs=_x_spec(SC_TIME_BLOCK),
        out_shape=jax.ShapeDtypeStruct(x.shape, x.dtype),
        scratch_shapes=[
            _weight_scratch(D_MODEL, 3 * SC_WIDTH),
            _weight_scratch(SC_WIDTH, D_MODEL),
            _weight_stage(128, 3 * SC_WIDTH),
            _weight_stage(256, D_MODEL),
            _DMA_SEMS,
            pltpu.VMEM((TAIL_ROWS, SC_WIDTH), F32),
            _stage_scratch(SC_TIME_BLOCK),
            _stage_scratch(SC_TIME_BLOCK),
        ],
        compiler_params=_compiler_params(),
    )(x, gpre, win, cw, wout, gpost)


def _split_bf16(v):
    hi = v.astype(BF16)
    lo = (v - hi.astype(F32)).astype(BF16)
    return jnp.concatenate([hi, lo], axis=1)


def _head_expander():
    k = lax.broadcasted_iota(jnp.int32, (2 * LANES, SSD_D_INNER), 0) % LANES
    c = lax.broadcasted_iota(jnp.int32, (2 * LANES, SSD_D_INNER), 1) // SSD_HEAD_DIM
    return jnp.where(k == c, 1.0, 0.0).astype(BF16)


def _ssd_chunk_prelude(expand_ref, dt_c, a_c, tril):
    tc = SCAN_CHUNK
    acs = jnp.dot(tril, a_c, precision=lax.Precision.HIGHEST,
                  preferred_element_type=F32)
    src_t = (acs - jnp.log(dt_c)).T
    total = acs[tc - 1:tc]
    e_in = jnp.exp(acs)
    w_out = jnp.exp(total - acs) * dt_c
    e_in_x = _dot(_split_bf16(e_in), expand_ref[...])
    w_out_x = _dot(_split_bf16(w_out), expand_ref[...])
    return acs, src_t, e_in_x, w_out_x


def _ssd_scan_chunk(xbc_ref, y_ref, state_ref, dskip_ref, r0, prelude, causal, gcol, mid_chunk):
    tc = SCAN_CHUNK
    acs, src_t, e_in_x, w_out_x = prelude

    def start_pair(p):
        rows = pl.ds(r0, tc)
        b_pair = [xbc_ref[rows, pl.ds(SSD_D_INNER + (2 * p + i) * SSD_D_STATE, SSD_D_STATE)]
                  .astype(BF16) for i in range(2)]
        c_pair = xbc_ref[rows, pl.ds(SSD_D_INNER + SSD_BC_DIM + 2 * p * SSD_D_STATE,
                                     2 * SSD_D_STATE)].astype(BF16)
        zero = jnp.zeros_like(b_pair[0])
        b_diag = jnp.concatenate([jnp.concatenate([b_pair[0], zero], axis=1),
                                  jnp.concatenate([zero, b_pair[1]], axis=1)], axis=0)
        cb_pair = lax.dot_general(c_pair, b_diag, (((1,), (1,)), ((), ())),
                                  preferred_element_type=F32)
        return b_pair, c_pair, cb_pair

    started = start_pair(0)
    for g in range(SSD_N_GROUPS):
        if g % 2 == 0:
            b_pair, c_pair, cb_pair = started
            if g + 2 < SSD_N_GROUPS:
                started = start_pair(g // 2 + 1)
        i = g % 2
        bg = b_pair[i]
        cg = c_pair[:, i * SSD_D_STATE:(i + 1) * SSD_D_STATE]
        cb_mat = cb_pair[:, i * tc:(i + 1) * tc]
        xg = xbc_ref[pl.ds(r0, tc), pl.ds(g * GROUP_COLS, GROUP_COLS)]
        if g == SSD_N_GROUPS // 2:
            mid_chunk()
        h0 = g * SSD_HEADS_PER_GROUP
        xcols = pl.ds(g * GROUP_COLS, GROUP_COLS)
        m_parts = []
        x_parts = []
        for j in range(SSD_HEADS_PER_GROUP):
            h = h0 + j
            seg = acs[:, h:h + 1] - src_t[h:h + 1, :]
            lmat = jnp.exp(jnp.where(causal, seg, NEG_BIG))
            m_parts.append((cb_mat * lmat).astype(BF16))
            x_parts.append(jnp.where(gcol == j, xg, 0.0).astype(BF16))
        m_cat = jnp.concatenate(m_parts, axis=1)
        x_bd = jnp.concatenate(x_parts, axis=0)
        y_diag = _dot(m_cat, x_bd)

        e_in_g = e_in_x[:, g * GROUP_COLS:(g + 1) * GROUP_COLS]
        w_out_g = w_out_x[:, g * GROUP_COLS:(g + 1) * GROUP_COLS]
        e_tot_g = e_in_g[tc - 1:tc]
        state = state_ref[g]
        y_off = _dot(cg, state.astype(BF16)) * e_in_g
        y_ref[pl.ds(r0, tc), xcols] = y_diag + y_off + xg * dskip_ref[:, xcols]
        upd = lax.dot_general(bg, (xg * w_out_g).astype(BF16), (((0,), (0,)), ((), ())),
                              preferred_element_type=F32)
        state_ref[g] = state * e_tot_g + upd
        yield


_DONE = object()


def _run(task):
    for _ in task:
        pass


def _chain(*tasks):
    for task in tasks:
        yield from task


def _interleave(primary, secondary):
    primary_live = secondary_live = True
    while primary_live or secondary_live:
        if primary_live:
            primary_live = next(primary, _DONE) is not _DONE
        if secondary_live:
            secondary_live = next(secondary, _DONE) is not _DONE


def _ssd_kernel(layer, is_first, x_ref, gpre_ref, win_hbm, cw_ref, cb_ref, dtb_ref, alog_ref,
                dskip_ref, normw_ref, wout_hbm, gpost_ref,
                o_ref, wz_ref, wxbc_ref, wout_ref, wstage_ref, wsem,
                expand_ref, tail_ref, state_ref, xbc_ref, y_ref, xstage_ref, ostage_ref):
    t = pl.program_id(1)

    @pl.when(_is_first_step())
    def _():
        w_in_t = win_hbm.at[layer]
        _stream_weight_transposed(w_in_t.at[pl.ds(SSD_D_INNER, SSD_CONV_DIM + SSD_N_HEADS), :],
                                  wxbc_ref, wstage_ref, wsem)
        _stream_weight_transposed(w_in_t.at[pl.ds(0, SSD_D_INNER), :], wz_ref, wstage_ref, wsem)
        _stream_weight(wout_hbm.at[layer], wout_ref, wstage_ref, wsem)
        expand_ref[...] = _head_expander()

    @pl.when(t == 0)
    def _():
        tail_ref[...] = jnp.zeros_like(tail_ref)
        state_ref[...] = jnp.zeros_like(state_ref)

    tc = SCAN_CHUNK
    time_l = _time_of_position(lax.broadcasted_iota(jnp.int32, (tc, tc), 0))
    time_s = _time_of_position(lax.broadcasted_iota(jnp.int32, (tc, tc), 1))
    causal = time_l >= time_s
    tril = causal.astype(F32)
    gcol = lax.broadcasted_iota(jnp.int32, (tc, GROUP_COLS), 1) // SSD_HEAD_DIM

    n_chunks = SSD_CONV_DIM // MXU_COLS
    n_sub = TIME_BLOCK // SUB_ROWS
    tails = [tail_ref[:, pl.ds(c * MXU_COLS, MXU_COLS)] for c in range(n_chunks)]
    ctx = [dict() for _ in range(n_sub)]

    def in_proj(sb):
        r0 = sb * SUB_ROWS
        x = _load_permuted(x_ref, xstage_ref, r0, SUB_ROWS, is_first)
        hb = _rms_norm(x, gpre_ref[...]).astype(BF16)
        ctx[sb].update(x=x, hb=hb)
        for c in range(n_chunks):
            cols = pl.ds(c * MXU_COLS, MXU_COLS)
            raw = _dot(hb, wxbc_ref[:, cols])
            conv = _causal_conv(raw, tails[c], cw_ref[:, cols], cb_ref[:, cols])
            tails[c] = raw[SUB_ROWS - TAIL_ROWS:]
            xbc_ref[pl.ds(r0, SUB_ROWS), cols] = _silu(conv)
            yield
        dt_raw = _dot(hb, wxbc_ref[:, pl.ds(SSD_CONV_DIM, LANES)])
        dt = _softplus(dt_raw + dtb_ref[...])
        ctx[sb].update(dt=dt, a=dt * -jnp.exp(alog_ref[...]))
        yield

    def scan(sb):
        dt, a = ctx[sb]["dt"], ctx[sb]["a"]
        n = SUB_ROWS // tc
        preludes = {0: _ssd_chunk_prelude(expand_ref, dt[:tc], a[:tc], tril)}
        for k in range(n):
            def mid_chunk(k=k):
                if k + 1 < n:
                    rows = slice((k + 1) * tc, (k + 2) * tc)
                    preludes[k + 1] = _ssd_chunk_prelude(expand_ref, dt[rows], a[rows], tril)
            yield from _ssd_scan_chunk(xbc_ref, y_ref, state_ref, dskip_ref, sb * SUB_ROWS + k * tc,
                                       preludes.pop(k), causal, gcol, mid_chunk)

    def out_proj(sb):
        r0 = sb * SUB_ROWS
        sub = pl.ds(r0, SUB_ROWS)
        x, hb = ctx[sb]["x"], ctx[sb]["hb"]
        sumsq = jnp.zeros((SUB_ROWS, 1), F32)
        for c in range(SSD_D_INNER // MXU_COLS):
            cols = pl.ds(c * MXU_COLS, MXU_COLS)
            gated = y_ref[sub, cols] * _silu(_dot(hb, wz_ref[:, cols]))
            sumsq = sumsq + jnp.sum(gated * gated, axis=-1, keepdims=True)
            y_ref[sub, cols] = gated
            yield
        scale = lax.rsqrt(sumsq * (1.0 / SSD_D_INNER) + EPS)
        yn = (y_ref[sub, :] * scale * normw_ref[...]).astype(BF16)
        parts = []
        for c in range(D_MODEL // MXU_COLS):
            parts.append(_dot(yn, wout_ref[:, pl.ds(c * MXU_COLS, MXU_COLS)]))
            yield
        m = jnp.concatenate(parts, axis=1)
        _store_rows(o_ref, ostage_ref, r0, x + _rms_norm(m, gpost_ref[...]), False)

    _run(in_proj(0))
    for sb in range(n_sub):
        fill = []
        if sb > 0:
            fill.append(out_proj(sb - 1))
        if sb + 1 < n_sub:
            fill.append(in_proj(sb + 1))
        _interleave(scan(sb), _chain(*fill))
    _run(out_proj(n_sub - 1))
    for c in range(n_chunks):
        tail_ref[:, pl.ds(c * MXU_COLS, MXU_COLS)] = tails[c]


def _ssd_layer(layer, is_first, x, gpre, win, cw, cb, dtb, alog, dskip, normw, wout, gpost):
    bsz, length, _ = x.shape
    small = (cw, cb, dtb, alog, dskip, normw)
    return pl.pallas_call(
        functools.partial(_ssd_kernel, layer, is_first),
        name="ssd_mixer",
        grid=(bsz, length // TIME_BLOCK),
        in_specs=([_x_spec(), _const_spec(gpre.shape), _HBM_SPEC]
                  + [_const_spec(c.shape) for c in small] + [_HBM_SPEC, _const_spec(gpost.shape)]),
        out_specs=_x_spec(),
        out_shape=jax.ShapeDtypeStruct(x.shape, x.dtype),
        scratch_shapes=[
            _weight_scratch(D_MODEL, SSD_D_INNER),
            _weight_scratch(D_MODEL, SSD_CONV_DIM),
            _weight_scratch(SSD_D_INNER, D_MODEL),
            _weight_stage(256, D_MODEL),
            _DMA_SEMS,
            pltpu.VMEM((2 * LANES, SSD_D_INNER), BF16),
            pltpu.VMEM((TAIL_ROWS, SSD_CONV_DIM), F32),
            pltpu.VMEM((SSD_N_GROUPS, SSD_D_STATE, GROUP_COLS), F32),
            pltpu.VMEM((TIME_BLOCK, SSD_CONV_DIM), F32),
            pltpu.VMEM((TIME_BLOCK, SSD_D_INNER), F32),
            _stage_scratch(),
            _stage_scratch(),
        ],
        compiler_params=_compiler_params(),
    )(x, gpre, win, *small, wout, gpost)


def _pad_lanes(v):
    return jnp.pad(v, ((0, 0), (0, LANES - v.shape[-1])))


def kernel(x, mix_pre_g, mix_post_g, ffn_pre_g, ffn_post_g, ssd_w_in, ssd_conv_w, ssd_conv_b,
           ssd_dt_bias, ssd_A_log, ssd_D, ssd_norm_w, ssd_w_out, sc_w_in, sc_conv_w, sc_w_out,
           ffn_w_up, ffn_conv_w, ffn_conv_b, ffn_w_down):
    depth = mix_pre_g.shape[0]
    w_in_t = jnp.swapaxes(ssd_w_in, 1, 2)
    row = lambda v: v.reshape(1, -1)
    for i in range(depth):
        j = i // 2
        if i % 2 == 0:
            dskip = jnp.repeat(ssd_D[j], SSD_HEAD_DIM).reshape(1, -1)
            x = _ssd_layer(j, i == 0, x, row(mix_pre_g[i]), w_in_t, ssd_conv_w[j], row(ssd_conv_b[j]),
                           _pad_lanes(row(ssd_dt_bias[j])), _pad_lanes(row(ssd_A_log[j])), dskip,
                           row(ssd_norm_w[j]), ssd_w_out, row(mix_post_g[i]))
        else:
            x = _sc_layer(j, x, row(mix_pre_g[i]), sc_w_in, sc_conv_w[j], sc_w_out,
                          row(mix_post_g[i]))
        x = _ffn_layer(i, i == depth - 1, x, row(ffn_pre_g[i]), ffn_w_up, ffn_conv_w[i], row(ffn_conv_b[i]),
                       ffn_w_down, row(ffn_post_g[i]))
    return x
```

```python
import functools

import jax
import jax.numpy as jnp
from jax import lax
from jax.experimental import pallas as pl
from jax.experimental.pallas import tpu as pltpu

EPS = 1e-6
D_MODEL = 1024

SSD_D_INNER = 2048
SSD_HEAD_DIM = 64
SSD_N_HEADS = 32
SSD_N_GROUPS = 8
SSD_HEADS_PER_GROUP = 4
SSD_D_STATE = 128
SSD_CONV_W = 4
SSD_BC_DIM = SSD_N_GROUPS * SSD_D_STATE
SSD_CONV_DIM = SSD_D_INNER + 2 * SSD_BC_DIM
GROUP_COLS = SSD_HEADS_PER_GROUP * SSD_HEAD_DIM

SC_WIDTH = 1024
SC_CONV_W = 3
FFN_HIDDEN = 2816
FFN_CONV_W = 3

LANES = 128
SUBLANES = 8
MXU_COLS = 256
VMEM_LIMIT_BYTES = 60 * 1024 * 1024

ROW_GROUP = SUBLANES * SUBLANES
TAIL_VREG_ROWS = 3
TAIL_ROWS = TAIL_VREG_ROWS * SUBLANES

TIME_BLOCK = 512
SC_TIME_BLOCK = 1024
SUB_ROWS = 256
SCAN_CHUNK = 128
NEG_BIG = -1e30

BF16 = jnp.bfloat16
F32 = jnp.float32


def _rms_norm(x, g):
    ms = jnp.mean(x * x, axis=-1, keepdims=True)
    return x * lax.rsqrt(ms + EPS) * g


def _silu(x):
    hx = 0.5 * x
    return hx * jnp.tanh(hx) + hx


def _softplus(x):
    return jnp.maximum(x, 0.0) + jnp.log1p(jnp.exp(-jnp.abs(x)))


def _dot(a, b):
    return jnp.dot(a, b, preferred_element_type=F32)


def _load_permuted(x_ref, stage_ref, r0, rows, from_time_order):
    if not from_time_order:
        return x_ref[0, pl.ds(r0, rows), :]
    d = x_ref.shape[2]
    for j in range(d // LANES):
        stage_ref[j, pl.ds(r0, rows), :] = x_ref[0, pl.ds(r0, rows), j * LANES:(j + 1) * LANES]
    cols = []
    for j in range(d // LANES):
        tiles = [stage_ref[j, pl.ds(r0 + g * ROW_GROUP + r, SUBLANES, stride=SUBLANES), :]
                 for g in range(rows // ROW_GROUP) for r in range(SUBLANES)]
        cols.append(jnp.concatenate(tiles, axis=0))
    return jnp.concatenate(cols, axis=1)


def _store_rows(o_ref, stage_ref, r0, val, to_time_order):
    rows, d = val.shape
    if not to_time_order:
        o_ref[0, pl.ds(r0, rows), :] = val
        return
    for j in range(d // LANES):
        for g in range(rows // ROW_GROUP):
            for r in range(SUBLANES):
                p0 = g * ROW_GROUP + r * SUBLANES
                stage_ref[j, pl.ds(r0 + g * ROW_GROUP + r, SUBLANES, stride=SUBLANES), :] = (
                    val[p0:p0 + SUBLANES, j * LANES:(j + 1) * LANES])
    for j in range(d // LANES):
        o_ref[0, pl.ds(r0, rows), j * LANES:(j + 1) * LANES] = stage_ref[j, pl.ds(r0, rows), :]


def _time_of_position(p):
    return ((p >> 6) << 6) + ((p & 7) << 3) + ((p >> 3) & 7)


def _shifted(cur, prev_tail, k_max):
    rows = cur.shape[0]
    sub = lax.broadcasted_iota(jnp.int32, (SUBLANES, cur.shape[1]), 0)
    last = sub == SUBLANES - 1
    first_q = SUBLANES - TAIL_VREG_ROWS
    out = [[] for _ in range(k_max)]
    for g in range(rows // ROW_GROUP):
        base = g * ROW_GROUP
        prev = prev_tail if g == 0 else cur[base - TAIL_ROWS:base]
        rolled = {}
        for q in range(SUBLANES - k_max, SUBLANES):
            cur_q = cur[base + q * SUBLANES:base + (q + 1) * SUBLANES]
            prev_q = prev[(q - first_q) * SUBLANES:(q - first_q + 1) * SUBLANES]
            rolled[q] = pltpu.roll(jnp.where(last, prev_q, cur_q), 1, axis=0)
        for k in range(1, k_max + 1):
            pieces = [rolled[q] for q in range(SUBLANES - k, SUBLANES)]
            pieces.append(cur[base:base + (SUBLANES - k) * SUBLANES])
            out[k - 1].append(jnp.concatenate(pieces, axis=0))
    return [jnp.concatenate(o, axis=0) for o in out]


def _causal_conv(cur, prev_tail, w, b=None):
    k = w.shape[0]
    out = cur * w[k - 1:k]
    for s, shifted in enumerate(_shifted(cur, prev_tail, k - 1), start=1):
        out = out + shifted * w[k - 1 - s:k - s]
    if b is not None:
        out = out + b
    return out


def _weight_scratch(rows, cols):
    return pltpu.VMEM((rows, cols + LANES), BF16)


def _weight_stage(row_chunk, cols):
    return pltpu.VMEM((2, row_chunk, cols), F32)


def _stream_weight(src, dst_ref, stage_ref, sem):
    rows, cols = src.shape
    row_chunk = stage_ref.shape[1]
    n = rows // row_chunk
    if cols % LANES:
        last_tile = pl.ds(cols // LANES * LANES, LANES)
        dst_ref[:, last_tile] = jnp.zeros((rows, LANES), BF16)

    def copy(i):
        return pltpu.make_async_copy(src.at[pl.ds(i * row_chunk, row_chunk), :],
                                     stage_ref.at[i % 2], sem.at[i % 2])

    copy(0).start()
    for i in range(n):
        if i + 1 < n:
            copy(i + 1).start()
        copy(i).wait()
        dst_ref[pl.ds(i * row_chunk, row_chunk), pl.ds(0, cols)] = stage_ref[i % 2].astype(BF16)


def _stream_weight_transposed(src, dst_ref, stage_ref, sem):
    n_out, _ = src.shape
    chunk = stage_ref.shape[1]
    chunks = [(start, min(chunk, n_out - start)) for start in range(0, n_out, chunk)]

    def padded(size):
        return -(-size // LANES) * LANES

    def copy(i):
        start, size = chunks[i]
        return pltpu.make_async_copy(src.at[pl.ds(start, size), :],
                                     stage_ref.at[i % 2, pl.ds(0, size), :], sem.at[i % 2])

    def start_copy(i):
        size = chunks[i][1]
        if size < padded(size):
            stage_ref[i % 2, pl.ds(size, padded(size) - size), :] = jnp.zeros(
                (padded(size) - size, stage_ref.shape[2]), F32)
        copy(i).start()

    start_copy(0)
    for i in range(len(chunks)):
        if i + 1 < len(chunks):
            start_copy(i + 1)
        copy(i).wait()
        start, size = chunks[i]
        cols = padded(size)
        dst_ref[:, pl.ds(start, cols)] = stage_ref[i % 2, pl.ds(0, cols), :].T.astype(BF16)


def _is_first_step():
    return jnp.logical_and(pl.program_id(0) == 0, pl.program_id(1) == 0)


_HBM_SPEC = pl.BlockSpec(memory_space=pl.ANY)
_DMA_SEMS = pltpu.SemaphoreType.DMA((2,))


def _const_spec(shape):
    return pl.BlockSpec(shape, lambda b, t: (0,) * len(shape), pipeline_mode=pl.Buffered(1))


def _x_spec(time_block=TIME_BLOCK):
    return pl.BlockSpec((1, time_block, D_MODEL), lambda b, t: (b, t, 0))


def _stage_scratch(time_block=TIME_BLOCK):
    return pltpu.VMEM((D_MODEL // LANES, time_block, LANES), F32)


def _compiler_params():
    return pltpu.CompilerParams(
        dimension_semantics=("arbitrary", "arbitrary"),
        vmem_limit_bytes=VMEM_LIMIT_BYTES,
    )


def _ffn_kernel(layer, is_last, x_ref, gpre_ref, wup_hbm, cw_ref, cb_ref, wdown_hbm, gpost_ref,
                o_ref, wup_ref, wdown_ref, upstage_ref, downstage_ref, wsem,
                tail_ref, hid_ref, xstage_ref, ostage_ref):
    t = pl.program_id(1)

    @pl.when(_is_first_step())
    def _():
        _stream_weight(wup_hbm.at[layer], wup_ref, upstage_ref, wsem)
        _stream_weight(wdown_hbm.at[layer], wdown_ref, downstage_ref, wsem)

    @pl.when(t == 0)
    def _():
        tail_ref[...] = jnp.zeros_like(tail_ref)

    n_chunks = FFN_HIDDEN // MXU_COLS
    tails = [tail_ref[:, pl.ds(c * MXU_COLS, MXU_COLS)] for c in range(n_chunks)]
    for sb in range(TIME_BLOCK // SUB_ROWS):
        r0 = sb * SUB_ROWS
        x = _load_permuted(x_ref, xstage_ref, r0, SUB_ROWS, False)
        hb = _rms_norm(x, gpre_ref[...]).astype(BF16)
        for c in range(n_chunks):
            cols = pl.ds(c * MXU_COLS, MXU_COLS)
            gate = _dot(hb, wup_ref[:, cols])
            val = _dot(hb, wup_ref[:, pl.ds(FFN_HIDDEN + c * MXU_COLS, MXU_COLS)])
            conv = _causal_conv(gate, tails[c], cw_ref[:, cols], cb_ref[:, cols])
            tails[c] = gate[SUB_ROWS - TAIL_ROWS:]
            hid_ref[pl.ds(r0, SUB_ROWS), cols] = (_silu(conv) * val).astype(BF16)
        f = _dot(hid_ref[pl.ds(r0, SUB_ROWS), :], wdown_ref[:, pl.ds(0, D_MODEL)])
        _store_rows(o_ref, ostage_ref, r0, x + _rms_norm(f, gpost_ref[...]), is_last)
    for c in range(n_chunks):
        tail_ref[:, pl.ds(c * MXU_COLS, MXU_COLS)] = tails[c]


def _ffn_layer(layer, is_last, x, gpre, wup, cw, cb, wdown, gpost):
    bsz, length, _ = x.shape
    return pl.pallas_call(
        functools.partial(_ffn_kernel, layer, is_last),
        name="conv_ffn",
        grid=(bsz, length // TIME_BLOCK),
        in_specs=[
            _x_spec(),
            _const_spec(gpre.shape),
            _HBM_SPEC,
            _const_spec(cw.shape),
            _const_spec(cb.shape),
            _HBM_SPEC,
            _const_spec(gpost.shape),
        ],
        out_specs=_x_spec(),
        out_shape=jax.ShapeDtypeStruct(x.shape, x.dtype),
        scratch_shapes=[
            pltpu.VMEM((D_MODEL, 2 * FFN_HIDDEN), BF16),
            _weight_scratch(FFN_HIDDEN, D_MODEL),
            _weight_stage(64, 2 * FFN_HIDDEN),
            _weight_stage(256, D_MODEL),
            _DMA_SEMS,
            pltpu.VMEM((TAIL_ROWS, FFN_HIDDEN), F32),
            pltpu.VMEM((TIME_BLOCK, FFN_HIDDEN), BF16),
            _stage_scratch(),
            _stage_scratch(),
        ],
        compiler_params=_compiler_params(),
    )(x, gpre, wup, cw, cb, wdown, gpost)


def _sc_kernel(layer, x_ref, gpre_ref, win_hbm, cw_ref, wout_hbm, gpost_ref, o_ref,
               win_ref, wout_ref, instage_ref, outstage_ref, wsem,
               tail_ref, xstage_ref, ostage_ref):
    t = pl.program_id(1)

    @pl.when(_is_first_step())
    def _():
        _stream_weight(win_hbm.at[layer], win_ref, instage_ref, wsem)
        _stream_weight(wout_hbm.at[layer], wout_ref, outstage_ref, wsem)

    @pl.when(t == 0)
    def _():
        tail_ref[...] = jnp.zeros_like(tail_ref)

    n_chunks = SC_WIDTH // MXU_COLS
    tails = [tail_ref[:, pl.ds(c * MXU_COLS, MXU_COLS)] for c in range(n_chunks)]
    for sb in range(SC_TIME_BLOCK // SUB_ROWS):
        r0 = sb * SUB_ROWS
        x = _load_permuted(x_ref, xstage_ref, r0, SUB_ROWS, False)
        hb = _rms_norm(x, gpre_ref[...]).astype(BF16)
        parts = []
        for c in range(n_chunks):
            cols = pl.ds(c * MXU_COLS, MXU_COLS)
            gb = _dot(hb, win_ref[:, cols])
            gc = _dot(hb, win_ref[:, pl.ds(SC_WIDTH + c * MXU_COLS, MXU_COLS)])
            v = _dot(hb, win_ref[:, pl.ds(2 * SC_WIDTH + c * MXU_COLS, MXU_COLS)])
            gcv = gc * v
            u = _causal_conv(gcv, tails[c], cw_ref[:, cols])
            tails[c] = gcv[SUB_ROWS - TAIL_ROWS:]
            parts.append((gb * u).astype(BF16))
        m = _dot(jnp.concatenate(parts, axis=1), wout_ref[:, pl.ds(0, D_MODEL)])
        _store_rows(o_ref, ostage_ref, r0, x + _rms_norm(m, gpost_ref[...]), False)
    for c in range(n_chunks):
        tail_ref[:, pl.ds(c * MXU_COLS, MXU_COLS)] = tails[c]


def _sc_layer(layer, x, gpre, win, cw, wout, gpost):
    bsz, length, _ = x.shape
    return pl.pallas_call(
        functools.partial(_sc_kernel, layer),
        name="shortconv_mixer",
        grid=(bsz, length // SC_TIME_BLOCK),
        in_specs=[
            _x_spec(SC_TIME_BLOCK),
            _const_spec(gpre.shape),
            _HBM_SPEC,
            _const_spec(cw.shape),
            _HBM_SPEC,
            _const_spec(gpost.shape),
        ],
        out_specs=_x_spec(SC_TIME_BLOCK),
        out_shape=jax.ShapeDtypeStruct(x.shape, x.dtype),
        scratch_shapes=[
            _weight_scratch(D_MODEL, 3 * SC_WIDTH),
            _weight_scratch(SC_WIDTH, D_MODEL),
            _weight_stage(128, 3 * SC_WIDTH),
            _weight_stage(256, D_MODEL),
            _DMA_SEMS,
            pltpu.VMEM((TAIL_ROWS, SC_WIDTH), F32),
            _stage_scratch(SC_TIME_BLOCK),
            _stage_scratch(SC_TIME_BLOCK),
        ],
        compiler_params=_compiler_params(),
    )(x, gpre, win, cw, wout, gpost)


def _split_bf16(v):
    hi = v.astype(BF16)
    lo = (v - hi.astype(F32)).astype(BF16)
    return jnp.concatenate([hi, lo], axis=1)


def _head_expander():
    k = lax.broadcasted_iota(jnp.int32, (2 * LANES, SSD_D_INNER), 0) % LANES
    c = lax.broadcasted_iota(jnp.int32, (2 * LANES, SSD_D_INNER), 1) // SSD_HEAD_DIM
    return jnp.where(k == c, 1.0, 0.0).astype(BF16)


def _cumsum_time(tril_b, a_c):
    hi = a_c.astype(BF16)
    rest = a_c - hi.astype(F32)
    mid = rest.astype(BF16)
    lo = (rest - mid.astype(F32)).astype(BF16)
    return _dot(jnp.concatenate([tril_b, tril_b, tril_b], axis=1),
                jnp.concatenate([hi, mid, lo], axis=0))


def _ssd_decay_terms(dt_c, acs):
    tc = SCAN_CHUNK
    src_t = (acs - jnp.log(dt_c)).T
    total = acs[tc - 1:tc]
    e_in = jnp.exp(acs)
    w_out = jnp.exp(total - acs) * dt_c
    return src_t, _split_bf16(e_in), _split_bf16(w_out)


def _ssd_scan_chunk(xbc_ref, y_ref, state_ref, dskip_ref, r0, prelude, causal, gcol):
    tc = SCAN_CHUNK
    acs, src_t, e_in_x, w_out_x = prelude

    def start_pair(p):
        rows = pl.ds(r0, tc)
        b_pair = [xbc_ref[rows, pl.ds(SSD_D_INNER + (2 * p + i) * SSD_D_STATE, SSD_D_STATE)]
                  .astype(BF16) for i in range(2)]
        c_pair = xbc_ref[rows, pl.ds(SSD_D_INNER + SSD_BC_DIM + 2 * p * SSD_D_STATE,
                                     2 * SSD_D_STATE)].astype(BF16)
        zero = jnp.zeros_like(b_pair[0])
        b_diag = jnp.concatenate([jnp.concatenate([b_pair[0], zero], axis=1),
                                  jnp.concatenate([zero, b_pair[1]], axis=1)], axis=0)
        cb_pair = lax.dot_general(c_pair, b_diag, (((1,), (1,)), ((), ())),
                                  preferred_element_type=F32)
        return b_pair, c_pair, cb_pair

    started = start_pair(0)
    for g in range(SSD_N_GROUPS):
        if g % 2 == 0:
            b_pair, c_pair, cb_pair = started
            if g + 2 < SSD_N_GROUPS:
                started = start_pair(g // 2 + 1)
        i = g % 2
        bg = b_pair[i]
        cg = c_pair[:, i * SSD_D_STATE:(i + 1) * SSD_D_STATE]
        cb_mat = cb_pair[:, i * tc:(i + 1) * tc]
        xg = xbc_ref[pl.ds(r0, tc), pl.ds(g * GROUP_COLS, GROUP_COLS)]
        h0 = g * SSD_HEADS_PER_GROUP
        xcols = pl.ds(g * GROUP_COLS, GROUP_COLS)
        m_parts = []
        x_parts = []
        for j in range(SSD_HEADS_PER_GROUP):
            h = h0 + j
            seg = acs[:, h:h + 1] - src_t[h:h + 1, :]
            lmat = jnp.exp(jnp.where(causal, seg, NEG_BIG))
            m_parts.append((cb_mat * lmat).astype(BF16))
            x_parts.append(jnp.where(gcol == j, xg, 0.0).astype(BF16))
        m_cat = jnp.concatenate(m_parts, axis=1)
        x_bd = jnp.concatenate(x_parts, axis=0)
        y_diag = _dot(m_cat, x_bd)

        e_in_g = e_in_x[:, g * GROUP_COLS:(g + 1) * GROUP_COLS]
        w_out_g = w_out_x[:, g * GROUP_COLS:(g + 1) * GROUP_COLS]
        e_tot_g = e_in_g[tc - 1:tc]
        state = state_ref[g]
        y_off = _dot(cg, state.astype(BF16)) * e_in_g
        y_ref[pl.ds(r0, tc), xcols] = y_diag + y_off + xg * dskip_ref[:, xcols]
        upd = lax.dot_general(bg, (xg * w_out_g).astype(BF16), (((0,), (0,)), ((), ())),
                              preferred_element_type=F32)
        state_ref[g] = state * e_tot_g + upd
        yield


_DONE = object()


def _run(task):
    for _ in task:
        pass


def _chain(*tasks):
    for task in tasks:
        yield from task


def _interleave(primary, secondary):
    primary_live = secondary_live = True
    while primary_live or secondary_live:
        if primary_live:
            primary_live = next(primary, _DONE) is not _DONE
        if secondary_live:
            secondary_live = next(secondary, _DONE) is not _DONE


def _ssd_kernel(layer, is_first, x_ref, gpre_ref, win_hbm, cw_ref, cb_ref, dtb_ref, alog_ref,
                dskip_ref, normw_ref, wout_hbm, gpost_ref,
                o_ref, wz_ref, wxbc_ref, wout_ref, wstage_ref, wsem,
                expand_ref, tail_ref, state_ref, xbc_ref, y_ref, xstage_ref, ostage_ref):
    t = pl.program_id(1)

    @pl.when(_is_first_step())
    def _():
        w_in_t = win_hbm.at[layer]
        _stream_weight_transposed(w_in_t.at[pl.ds(SSD_D_INNER, SSD_CONV_DIM + SSD_N_HEADS), :],
                                  wxbc_ref, wstage_ref, wsem)
        _stream_weight_transposed(w_in_t.at[pl.ds(0, SSD_D_INNER), :], wz_ref, wstage_ref, wsem)
        _stream_weight(wout_hbm.at[layer], wout_ref, wstage_ref, wsem)
        expand_ref[...] = _head_expander()

    @pl.when(t == 0)
    def _():
        tail_ref[...] = jnp.zeros_like(tail_ref)
        state_ref[...] = jnp.zeros_like(state_ref)

    tc = SCAN_CHUNK
    time_l = _time_of_position(lax.broadcasted_iota(jnp.int32, (tc, tc), 0))
    time_s = _time_of_position(lax.broadcasted_iota(jnp.int32, (tc, tc), 1))
    causal = time_l >= time_s
    tril_b = jnp.where(causal, 1.0, 0.0).astype(BF16)
    gcol = lax.broadcasted_iota(jnp.int32, (tc, GROUP_COLS), 1) // SSD_HEAD_DIM

    n_chunks = SSD_CONV_DIM // MXU_COLS
    n_sub = TIME_BLOCK // SUB_ROWS
    tails = [tail_ref[:, pl.ds(c * MXU_COLS, MXU_COLS)] for c in range(n_chunks)]
    ctx = [dict() for _ in range(n_sub)]

    def in_proj(sb):
        r0 = sb * SUB_ROWS
        x = _load_permuted(x_ref, xstage_ref, r0, SUB_ROWS, is_first)
        hb = _rms_norm(x, gpre_ref[...]).astype(BF16)
        ctx[sb].update(x=x, hb=hb)
        dt_raw = _dot(hb, wxbc_ref[:, pl.ds(SSD_CONV_DIM, LANES)])
        dt = _softplus(dt_raw + dtb_ref[...])
        a = dt * -jnp.exp(alog_ref[...])
        chunk_rows = [slice(k * tc, (k + 1) * tc) for k in range(SUB_ROWS // tc)]
        acs, terms, preludes = [], [], []
        for c in range(n_chunks):
            cols = pl.ds(c * MXU_COLS, MXU_COLS)
            raw = _dot(hb, wxbc_ref[:, cols])
            conv = _causal_conv(raw, tails[c], cw_ref[:, cols], cb_ref[:, cols])
            tails[c] = raw[SUB_ROWS - TAIL_ROWS:]
            xbc_ref[pl.ds(r0, SUB_ROWS), cols] = _silu(conv)
            if c == 1:
                acs = [_cumsum_time(tril_b, a[rows]) for rows in chunk_rows]
            if c == 5:
                terms = [_ssd_decay_terms(dt[rows], acs_k) for rows, acs_k in zip(chunk_rows, acs)]
            if c == 6:
                preludes = [(acs_k, src_t, _dot(e_in, expand_ref[...]), _dot(w_out, expand_ref[...]))
                            for acs_k, (src_t, e_in, w_out) in zip(acs, terms)]
            yield
        ctx[sb].update(preludes=preludes)

    def scan(sb):
        for k, prelude in enumerate(ctx[sb]["preludes"]):
            yield from _ssd_scan_chunk(xbc_ref, y_ref, state_ref, dskip_ref, sb * SUB_ROWS + k * tc,
                                       prelude, causal, gcol)

    def out_proj(sb):
        r0 = sb * SUB_ROWS
        sub = pl.ds(r0, SUB_ROWS)
        x, hb = ctx[sb]["x"], ctx[sb]["hb"]
        sumsq = jnp.zeros((SUB_ROWS, 1), F32)
        for c in range(SSD_D_INNER // MXU_COLS):
            cols = pl.ds(c * MXU_COLS, MXU_COLS)
            gated = y_ref[sub, cols] * _silu(_dot(hb, wz_ref[:, cols]))
            sumsq = sumsq + jnp.sum(gated * gated, axis=-1, keepdims=True)
            y_ref[sub, cols] = gated
            yield
        yw = (y_ref[sub, :] * normw_ref[...]).astype(BF16)
        parts = []
        for c in range(D_MODEL // MXU_COLS):
            parts.append(_dot(yw, wout_ref[:, pl.ds(c * MXU_COLS, MXU_COLS)]))
            yield
        m = jnp.concatenate(parts, axis=1) * lax.rsqrt(sumsq * (1.0 / SSD_D_INNER) + EPS)
        _store_rows(o_ref, ostage_ref, r0, x + _rms_norm(m, gpost_ref[...]), False)

    _run(in_proj(0))
    for sb in range(n_sub):
        fill = []
        if sb > 0:
            fill.append(out_proj(sb - 1))
        if sb + 1 < n_sub:
            fill.append(in_proj(sb + 1))
        _interleave(scan(sb), _chain(*fill))
    _run(out_proj(n_sub - 1))
    for c in range(n_chunks):
        tail_ref[:, pl.ds(c * MXU_COLS, MXU_COLS)] = tails[c]


def _ssd_layer(layer, is_first, x, gpre, win, cw, cb, dtb, alog, dskip, normw, wout, gpost):
    bsz, length, _ = x.shape
    small = (cw, cb, dtb, alog, dskip, normw)
    return pl.pallas_call(
        functools.partial(_ssd_kernel, layer, is_first),
        name="ssd_mixer",
        grid=(bsz, length // TIME_BLOCK),
        in_specs=([_x_spec(), _const_spec(gpre.shape), _HBM_SPEC]
                  + [_const_spec(c.shape) for c in small] + [_HBM_SPEC, _const_spec(gpost.shape)]),
        out_specs=_x_spec(),
        out_shape=jax.ShapeDtypeStruct(x.shape, x.dtype),
        scratch_shapes=[
            _weight_scratch(D_MODEL, SSD_D_INNER),
            _weight_scratch(D_MODEL, SSD_CONV_DIM),
            _weight_scratch(SSD_D_INNER, D_MODEL),
            _weight_stage(256, D_MODEL),
            _DMA_SEMS,
            pltpu.VMEM((2 * LANES, SSD_D_INNER), BF16),
            pltpu.VMEM((TAIL_ROWS, SSD_CONV_DIM), F32),
            pltpu.VMEM((SSD_N_GROUPS, SSD_D_STATE, GROUP_COLS), F32),
            pltpu.VMEM((TIME_BLOCK, SSD_CONV_DIM), F32),
            pltpu.VMEM((TIME_BLOCK, SSD_D_INNER), F32),
            _stage_scratch(),
            _stage_scratch(),
        ],
        compiler_params=_compiler_params(),
    )(x, gpre, win, *small, wout, gpost)


def _pad_lanes(v):
    return jnp.pad(v, ((0, 0), (0, LANES - v.shape[-1])))


def kernel(x, mix_pre_g, mix_post_g, ffn_pre_g, ffn_post_g, ssd_w_in, ssd_conv_w, ssd_conv_b,
           ssd_dt_bias, ssd_A_log, ssd_D, ssd_norm_w, ssd_w_out, sc_w_in, sc_conv_w, sc_w_out,
           ffn_w_up, ffn_conv_w, ffn_conv_b, ffn_w_down):
    depth = mix_pre_g.shape[0]
    w_in_t = jnp.swapaxes(ssd_w_in, 1, 2)
    row = lambda v: v.reshape(1, -1)
    for i in range(depth):
        j = i // 2
        if i % 2 == 0:
            dskip = jnp.repeat(ssd_D[j], SSD_HEAD_DIM).reshape(1, -1)
            x = _ssd_layer(j, i == 0, x, row(mix_pre_g[i]), w_in_t, ssd_conv_w[j], row(ssd_conv_b[j]),
                           _pad_lanes(row(ssd_dt_bias[j])), _pad_lanes(row(ssd_A_log[j])), dskip,
                           row(ssd_norm_w[j]), ssd_w_out, row(mix_post_g[i]))
        else:
            x = _sc_layer(j, x, row(mix_pre_g[i]), sc_w_in, sc_conv_w[j], sc_w_out,
                          row(mix_post_g[i]))
        x = _ffn_layer(i, i == depth - 1, x, row(ffn_pre_g[i]), ffn_w_up, ffn_conv_w[i], row(ffn_conv_b[i]),
                       ffn_w_down, row(ffn_post_g[i]))
    return x
```

```python
import functools

import jax
import jax.numpy as jnp
from jax import lax
from jax.experimental import pallas as pl
from jax.experimental.pallas import tpu as pltpu

EPS = 1e-6
D_MODEL = 1024

SSD_D_INNER = 2048
SSD_HEAD_DIM = 64
SSD_N_HEADS = 32
SSD_N_GROUPS = 8
SSD_HEADS_PER_GROUP = 4
SSD_D_STATE = 128
SSD_CONV_W = 4
SSD_BC_DIM = SSD_N_GROUPS * SSD_D_STATE
SSD_CONV_DIM = SSD_D_INNER + 2 * SSD_BC_DIM
GROUP_COLS = SSD_HEADS_PER_GROUP * SSD_HEAD_DIM

SC_WIDTH = 1024
SC_CONV_W = 3
FFN_HIDDEN = 2816
FFN_CONV_W = 3

LANES = 128
SUBLANES = 8
MXU_COLS = 256
VMEM_LIMIT_BYTES = 60 * 1024 * 1024

ROW_GROUP = SUBLANES * SUBLANES
TAIL_VREG_ROWS = 3
TAIL_ROWS = TAIL_VREG_ROWS * SUBLANES

TIME_BLOCK = 512
SC_TIME_BLOCK = 1024
SUB_ROWS = 256
SCAN_CHUNK = 128
NEG_BIG = -1e30

BF16 = jnp.bfloat16
F32 = jnp.float32


def _rms_norm(x, g):
    ms = jnp.mean(x * x, axis=-1, keepdims=True)
    return x * lax.rsqrt(ms + EPS) * g


def _silu(x):
    hx = 0.5 * x
    return hx * jnp.tanh(hx) + hx


def _softplus(x):
    return jnp.maximum(x, 0.0) + jnp.log1p(jnp.exp(-jnp.abs(x)))


def _dot(a, b):
    return jnp.dot(a, b, preferred_element_type=F32)


def _load_permuted(x_ref, stage_ref, r0, rows, from_time_order):
    if not from_time_order:
        return x_ref[0, pl.ds(r0, rows), :]
    d = x_ref.shape[2]
    for j in range(d // LANES):
        stage_ref[j, pl.ds(r0, rows), :] = x_ref[0, pl.ds(r0, rows), j * LANES:(j + 1) * LANES]
    cols = []
    for j in range(d // LANES):
        tiles = [stage_ref[j, pl.ds(r0 + g * ROW_GROUP + r, SUBLANES, stride=SUBLANES), :]
                 for g in range(rows // ROW_GROUP) for r in range(SUBLANES)]
        cols.append(jnp.concatenate(tiles, axis=0))
    return jnp.concatenate(cols, axis=1)


def _store_rows(o_ref, stage_ref, r0, val, to_time_order):
    rows, d = val.shape
    if not to_time_order:
        o_ref[0, pl.ds(r0, rows), :] = val
        return
    for j in range(d // LANES):
        for g in range(rows // ROW_GROUP):
            for r in range(SUBLANES):
                p0 = g * ROW_GROUP + r * SUBLANES
                stage_ref[j, pl.ds(r0 + g * ROW_GROUP + r, SUBLANES, stride=SUBLANES), :] = (
                    val[p0:p0 + SUBLANES, j * LANES:(j + 1) * LANES])
    for j in range(d // LANES):
        o_ref[0, pl.ds(r0, rows), j * LANES:(j + 1) * LANES] = stage_ref[j, pl.ds(r0, rows), :]


def _time_of_position(p):
    return ((p >> 6) << 6) + ((p & 7) << 3) + ((p >> 3) & 7)


def _shifted(cur, prev_tail, k_max):
    rows = cur.shape[0]
    sub = lax.broadcasted_iota(jnp.int32, (SUBLANES, cur.shape[1]), 0)
    last = sub == SUBLANES - 1
    first_q = SUBLANES - TAIL_VREG_ROWS
    out = [[] for _ in range(k_max)]
    for g in range(rows // ROW_GROUP):
        base = g * ROW_GROUP
        prev = prev_tail if g == 0 else cur[base - TAIL_ROWS:base]
        rolled = {}
        for q in range(SUBLANES - k_max, SUBLANES):
            cur_q = cur[base + q * SUBLANES:base + (q + 1) * SUBLANES]
            prev_q = prev[(q - first_q) * SUBLANES:(q - first_q + 1) * SUBLANES]
            rolled[q] = pltpu.roll(jnp.where(last, prev_q, cur_q), 1, axis=0)
        for k in range(1, k_max + 1):
            pieces = [rolled[q] for q in range(SUBLANES - k, SUBLANES)]
            pieces.append(cur[base:base + (SUBLANES - k) * SUBLANES])
            out[k - 1].append(jnp.concatenate(pieces, axis=0))
    return [jnp.concatenate(o, axis=0) for o in out]


def _causal_conv(cur, prev_tail, w, b=None):
    k = w.shape[0]
    out = cur * w[k - 1:k]
    for s, shifted in enumerate(_shifted(cur, prev_tail, k - 1), start=1):
        out = out + shifted * w[k - 1 - s:k - s]
    if b is not None:
        out = out + b
    return out


def _conv_silu(cur, prev_tail, w, b):
    h = _causal_conv(cur, prev_tail, 0.5 * w, 0.5 * b)
    return h * jnp.tanh(h) + h


def _weight_scratch(rows, cols):
    return pltpu.VMEM((rows, cols + LANES), BF16)


def _weight_stage(row_chunk, cols):
    return pltpu.VMEM((2, row_chunk, cols), F32)


def _stream_weight(src, dst_ref, stage_ref, sem):
    rows, cols = src.shape
    row_chunk = stage_ref.shape[1]
    n = rows // row_chunk
    if cols % LANES:
        last_tile = pl.ds(cols // LANES * LANES, LANES)
        dst_ref[:, last_tile] = jnp.zeros((rows, LANES), BF16)

    def copy(i):
        return pltpu.make_async_copy(src.at[pl.ds(i * row_chunk, row_chunk), :],
                                     stage_ref.at[i % 2], sem.at[i % 2])

    copy(0).start()
    for i in range(n):
        if i + 1 < n:
            copy(i + 1).start()
        copy(i).wait()
        dst_ref[pl.ds(i * row_chunk, row_chunk), pl.ds(0, cols)] = stage_ref[i % 2].astype(BF16)


def _stream_weight_transposed(src, dst_ref, stage_ref, sem):
    n_out, _ = src.shape
    chunk = stage_ref.shape[1]
    chunks = [(start, min(chunk, n_out - start)) for start in range(0, n_out, chunk)]

    def padded(size):
        return -(-size // LANES) * LANES

    def copy(i):
        start, size = chunks[i]
        return pltpu.make_async_copy(src.at[pl.ds(start, size), :],
                                     stage_ref.at[i % 2, pl.ds(0, size), :], sem.at[i % 2])

    def start_copy(i):
        size = chunks[i][1]
        if size < padded(size):
            stage_ref[i % 2, pl.ds(size, padded(size) - size), :] = jnp.zeros(
                (padded(size) - size, stage_ref.shape[2]), F32)
        copy(i).start()

    start_copy(0)
    for i in range(len(chunks)):
        if i + 1 < len(chunks):
            start_copy(i + 1)
        copy(i).wait()
        start, size = chunks[i]
        cols = padded(size)
        dst_ref[:, pl.ds(start, cols)] = stage_ref[i % 2, pl.ds(0, cols), :].T.astype(BF16)


def _is_first_step():
    return jnp.logical_and(pl.program_id(0) == 0, pl.program_id(1) == 0)


_HBM_SPEC = pl.BlockSpec(memory_space=pl.ANY)
_DMA_SEMS = pltpu.SemaphoreType.DMA((2,))


def _const_spec(shape):
    return pl.BlockSpec(shape, lambda b, t: (0,) * len(shape), pipeline_mode=pl.Buffered(1))


def _x_spec(time_block=TIME_BLOCK):
    return pl.BlockSpec((1, time_block, D_MODEL), lambda b, t: (b, t, 0))


def _stage_scratch(time_block=TIME_BLOCK):
    return pltpu.VMEM((D_MODEL // LANES, time_block, LANES), F32)


def _compiler_params():
    return pltpu.CompilerParams(
        dimension_semantics=("arbitrary", "arbitrary"),
        vmem_limit_bytes=VMEM_LIMIT_BYTES,
    )


def _ffn_kernel(layer, is_last, x_ref, gpre_ref, wup_hbm, cw_ref, cb_ref, wdown_hbm, gpost_ref,
                o_ref, wup_ref, wdown_ref, upstage_ref, downstage_ref, wsem,
                tail_ref, hid_ref, xstage_ref, ostage_ref):
    t = pl.program_id(1)

    @pl.when(_is_first_step())
    def _():
        _stream_weight(wup_hbm.at[layer], wup_ref, upstage_ref, wsem)
        _stream_weight(wdown_hbm.at[layer], wdown_ref, downstage_ref, wsem)

    @pl.when(t == 0)
    def _():
        tail_ref[...] = jnp.zeros_like(tail_ref)

    n_chunks = FFN_HIDDEN // MXU_COLS
    tails = [tail_ref[:, pl.ds(c * MXU_COLS, MXU_COLS)] for c in range(n_chunks)]
    for sb in range(TIME_BLOCK // SUB_ROWS):
        r0 = sb * SUB_ROWS
        x = _load_permuted(x_ref, xstage_ref, r0, SUB_ROWS, False)
        hb = _rms_norm(x, gpre_ref[...]).astype(BF16)
        for c in range(n_chunks):
            cols = pl.ds(c * MXU_COLS, MXU_COLS)
            gate = _dot(hb, wup_ref[:, cols])
            val = _dot(hb, wup_ref[:, pl.ds(FFN_HIDDEN + c * MXU_COLS, MXU_COLS)])
            act = _conv_silu(gate, tails[c], cw_ref[:, cols], cb_ref[:, cols])
            tails[c] = gate[SUB_ROWS - TAIL_ROWS:]
            hid_ref[pl.ds(r0, SUB_ROWS), cols] = (act * val).astype(BF16)
        f = _dot(hid_ref[pl.ds(r0, SUB_ROWS), :], wdown_ref[:, pl.ds(0, D_MODEL)])
        _store_rows(o_ref, ostage_ref, r0, x + _rms_norm(f, gpost_ref[...]), is_last)
    for c in range(n_chunks):
        tail_ref[:, pl.ds(c * MXU_COLS, MXU_COLS)] = tails[c]


def _ffn_layer(layer, is_last, x, gpre, wup, cw, cb, wdown, gpost):
    bsz, length, _ = x.shape
    return pl.pallas_call(
        functools.partial(_ffn_kernel, layer, is_last),
        name="conv_ffn",
        grid=(bsz, length // TIME_BLOCK),
        in_specs=[
            _x_spec(),
            _const_spec(gpre.shape),
            _HBM_SPEC,
            _const_spec(cw.shape),
            _const_spec(cb.shape),
            _HBM_SPEC,
            _const_spec(gpost.shape),
        ],
        out_specs=_x_spec(),
        out_shape=jax.ShapeDtypeStruct(x.shape, x.dtype),
        scratch_shapes=[
            pltpu.VMEM((D_MODEL, 2 * FFN_HIDDEN), BF16),
            _weight_scratch(FFN_HIDDEN, D_MODEL),
            _weight_stage(64, 2 * FFN_HIDDEN),
            _weight_stage(256, D_MODEL),
            _DMA_SEMS,
            pltpu.VMEM((TAIL_ROWS, FFN_HIDDEN), F32),
            pltpu.VMEM((TIME_BLOCK, FFN_HIDDEN), BF16),
            _stage_scratch(),
            _stage_scratch(),
        ],
        compiler_params=_compiler_params(),
    )(x, gpre, wup, cw, cb, wdown, gpost)


def _sc_kernel(layer, x_ref, gpre_ref, win_hbm, cw_ref, wout_hbm, gpost_ref, o_ref,
               win_ref, wout_ref, instage_ref, outstage_ref, wsem,
               tail_ref, xstage_ref, ostage_ref):
    t = pl.program_id(1)

    @pl.when(_is_first_step())
    def _():
        _stream_weight(win_hbm.at[layer], win_ref, instage_ref, wsem)
        _stream_weight(wout_hbm.at[layer], wout_ref, outstage_ref, wsem)

    @pl.when(t == 0)
    def _():
        tail_ref[...] = jnp.zeros_like(tail_ref)

    n_chunks = SC_WIDTH // MXU_COLS
    tails = [tail_ref[:, pl.ds(c * MXU_COLS, MXU_COLS)] for c in range(n_chunks)]
    for sb in range(SC_TIME_BLOCK // SUB_ROWS):
        r0 = sb * SUB_ROWS
        x = _load_permuted(x_ref, xstage_ref, r0, SUB_ROWS, False)
        hb = _rms_norm(x, gpre_ref[...]).astype(BF16)
        parts = []
        for c in range(n_chunks):
            cols = pl.ds(c * MXU_COLS, MXU_COLS)
            gb = _dot(hb, win_ref[:, cols])
            gc = _dot(hb, win_ref[:, pl.ds(SC_WIDTH + c * MXU_COLS, MXU_COLS)])
            v = _dot(hb, win_ref[:, pl.ds(2 * SC_WIDTH + c * MXU_COLS, MXU_COLS)])
            gcv = gc * v
            u = _causal_conv(gcv, tails[c], cw_ref[:, cols])
            tails[c] = gcv[SUB_ROWS - TAIL_ROWS:]
            parts.append((gb * u).astype(BF16))
        m = _dot(jnp.concatenate(parts, axis=1), wout_ref[:, pl.ds(0, D_MODEL)])
        _store_rows(o_ref, ostage_ref, r0, x + _rms_norm(m, gpost_ref[...]), False)
    for c in range(n_chunks):
        tail_ref[:, pl.ds(c * MXU_COLS, MXU_COLS)] = tails[c]


def _sc_layer(layer, x, gpre, win, cw, wout, gpost):
    bsz, length, _ = x.shape
    return pl.pallas_call(
        functools.partial(_sc_kernel, layer),
        name="shortconv_mixer",
        grid=(bsz, length // SC_TIME_BLOCK),
        in_specs=[
            _x_spec(SC_TIME_BLOCK),
            _const_spec(gpre.shape),
            _HBM_SPEC,
            _const_spec(cw.shape),
            _HBM_SPEC,
            _const_spec(gpost.shape),
        ],
        out_specs=_x_spec(SC_TIME_BLOCK),
        out_shape=jax.ShapeDtypeStruct(x.shape, x.dtype),
        scratch_shapes=[
            _weight_scratch(D_MODEL, 3 * SC_WIDTH),
            _weight_scratch(SC_WIDTH, D_MODEL),
            _weight_stage(128, 3 * SC_WIDTH),
            _weight_stage(256, D_MODEL),
            _DMA_SEMS,
            pltpu.VMEM((TAIL_ROWS, SC_WIDTH), F32),
            _stage_scratch(SC_TIME_BLOCK),
            _stage_scratch(SC_TIME_BLOCK),
        ],
        compiler_params=_compiler_params(),
    )(x, gpre, win, cw, wout, gpost)


def _split_bf16(v):
    hi = v.astype(BF16)
    lo = (v - hi.astype(F32)).astype(BF16)
    return jnp.concatenate([hi, lo], axis=1)


def _head_expander():
    k = lax.broadcasted_iota(jnp.int32, (2 * LANES, SSD_D_INNER), 0) % LANES
    c = lax.broadcasted_iota(jnp.int32, (2 * LANES, SSD_D_INNER), 1) // SSD_HEAD_DIM
    return jnp.where(k == c, 1.0, 0.0).astype(BF16)


def _cumsum_time(tril_b, a_c):
    hi = a_c.astype(BF16)
    rest = a_c - hi.astype(F32)
    mid = rest.astype(BF16)
    lo = (rest - mid.astype(F32)).astype(BF16)
    return _dot(jnp.concatenate([tril_b, tril_b, tril_b], axis=1),
                jnp.concatenate([hi, mid, lo], axis=0))


def _ssd_decay_terms(dt_c, acs):
    tc = SCAN_CHUNK
    src_t = (acs - jnp.log(dt_c)).T
    total = acs[tc - 1:tc]
    e_in = jnp.exp(acs)
    w_out = jnp.exp(total - acs) * dt_c
    return src_t, _split_bf16(e_in), _split_bf16(w_out)


def _ssd_scan_chunk(xbc_ref, y_ref, state_ref, dskip_ref, r0, prelude, causal, gcol):
    tc = SCAN_CHUNK
    acs, src_t, e_in_x, w_out_x = prelude

    def start_pair(p):
        rows = pl.ds(r0, tc)
        b_pair = [xbc_ref[rows, pl.ds(SSD_D_INNER + (2 * p + i) * SSD_D_STATE, SSD_D_STATE)]
                  .astype(BF16) for i in range(2)]
        c_pair = xbc_ref[rows, pl.ds(SSD_D_INNER + SSD_BC_DIM + 2 * p * SSD_D_STATE,
                                     2 * SSD_D_STATE)].astype(BF16)
        zero = jnp.zeros_like(b_pair[0])
        b_diag = jnp.concatenate([jnp.concatenate([b_pair[0], zero], axis=1),
                                  jnp.concatenate([zero, b_pair[1]], axis=1)], axis=0)
        cb_pair = lax.dot_general(c_pair, b_diag, (((1,), (1,)), ((), ())),
                                  preferred_element_type=F32)
        return b_pair, c_pair, cb_pair

    def prepare(g, pair):
        b_pair, c_pair, cb_pair = pair
        i = g % 2
        cb_mat = cb_pair[:, i * tc:(i + 1) * tc]
        xg = xbc_ref[pl.ds(r0, tc), pl.ds(g * GROUP_COLS, GROUP_COLS)]
        m_parts = []
        x_parts = []
        for j in range(SSD_HEADS_PER_GROUP):
            h = g * SSD_HEADS_PER_GROUP + j
            seg = acs[:, h:h + 1] - src_t[h:h + 1, :]
            lmat = jnp.exp(jnp.where(causal, seg, NEG_BIG))
            m_parts.append((cb_mat * lmat).astype(BF16))
            x_parts.append(jnp.where(gcol == j, xg, 0.0).astype(BF16))
        m_cat = jnp.concatenate(m_parts, axis=1)
        x_bd = jnp.concatenate(x_parts, axis=0)
        return b_pair[i], c_pair[:, i * SSD_D_STATE:(i + 1) * SSD_D_STATE], xg, m_cat, x_bd

    pairs = {0: start_pair(0)}
    prepared = prepare(0, pairs[0])
    for g in range(SSD_N_GROUPS):
        if g % 2 == 0 and g + 2 < SSD_N_GROUPS:
            pairs[g // 2 + 1] = start_pair(g // 2 + 1)
        bg, cg, xg, m_cat, x_bd = prepared
        if g + 1 < SSD_N_GROUPS:
            prepared = prepare(g + 1, pairs[(g + 1) // 2])
        xcols = pl.ds(g * GROUP_COLS, GROUP_COLS)
        y_diag = _dot(m_cat, x_bd)

        e_in_g = e_in_x[:, g * GROUP_COLS:(g + 1) * GROUP_COLS]
        w_out_g = w_out_x[:, g * GROUP_COLS:(g + 1) * GROUP_COLS]
        e_tot_g = e_in_g[tc - 1:tc]
        state = state_ref[g]
        y_off = _dot(cg, state.astype(BF16)) * e_in_g
        y_ref[pl.ds(r0, tc), xcols] = y_diag + y_off + xg * dskip_ref[:, xcols]
        upd = lax.dot_general(bg, (xg * w_out_g).astype(BF16), (((0,), (0,)), ((), ())),
                              preferred_element_type=F32)
        state_ref[g] = state * e_tot_g + upd
        yield


_DONE = object()


def _run(task):
    for _ in task:
        pass


def _chain(*tasks):
    for task in tasks:
        yield from task


def _interleave(primary, secondary):
    primary_live = secondary_live = True
    while primary_live or secondary_live:
        if primary_live:
            primary_live = next(primary, _DONE) is not _DONE
        if secondary_live:
            secondary_live = next(secondary, _DONE) is not _DONE


def _ssd_kernel(layer, is_first, x_ref, gpre_ref, win_hbm, cw_ref, cb_ref, dtb_ref, alog_ref,
                dskip_ref, normw_ref, wout_hbm, gpost_ref,
                o_ref, wz_ref, wxbc_ref, wout_ref, wstage_ref, wsem,
                expand_ref, tail_ref, state_ref, xbc_ref, y_ref, xstage_ref, ostage_ref):
    t = pl.program_id(1)

    @pl.when(_is_first_step())
    def _():
        w_in_t = win_hbm.at[layer]
        _stream_weight_transposed(w_in_t.at[pl.ds(SSD_D_INNER, SSD_CONV_DIM + SSD_N_HEADS), :],
                                  wxbc_ref, wstage_ref, wsem)
        _stream_weight_transposed(w_in_t.at[pl.ds(0, SSD_D_INNER), :], wz_ref, wstage_ref, wsem)
        _stream_weight(wout_hbm.at[layer], wout_ref, wstage_ref, wsem)
        expand_ref[...] = _head_expander()

    @pl.when(t == 0)
    def _():
        tail_ref[...] = jnp.zeros_like(tail_ref)
        state_ref[...] = jnp.zeros_like(state_ref)

    tc = SCAN_CHUNK
    time_l = _time_of_position(lax.broadcasted_iota(jnp.int32, (tc, tc), 0))
    time_s = _time_of_position(lax.broadcasted_iota(jnp.int32, (tc, tc), 1))
    causal = time_l >= time_s
    tril_b = jnp.where(causal, 1.0, 0.0).astype(BF16)
    gcol = lax.broadcasted_iota(jnp.int32, (tc, GROUP_COLS), 1) // SSD_HEAD_DIM

    n_chunks = SSD_CONV_DIM // MXU_COLS
    n_sub = TIME_BLOCK // SUB_ROWS
    tails = [tail_ref[:, pl.ds(c * MXU_COLS, MXU_COLS)] for c in range(n_chunks)]
    ctx = [dict() for _ in range(n_sub)]

    def in_proj(sb):
        r0 = sb * SUB_ROWS
        x = _load_permuted(x_ref, xstage_ref, r0, SUB_ROWS, is_first)
        hb = _rms_norm(x, gpre_ref[...]).astype(BF16)
        ctx[sb].update(x=x, hb=hb)
        dt_raw = _dot(hb, wxbc_ref[:, pl.ds(SSD_CONV_DIM, LANES)])
        dt = _softplus(dt_raw + dtb_ref[...])
        a = dt * -jnp.exp(alog_ref[...])
        chunk_rows = [slice(k * tc, (k + 1) * tc) for k in range(SUB_ROWS // tc)]
        acs, terms, preludes = [], [], []
        for c in range(n_chunks):
            cols = pl.ds(c * MXU_COLS, MXU_COLS)
            raw = _dot(hb, wxbc_ref[:, cols])
            xbc_ref[pl.ds(r0, SUB_ROWS), cols] = _conv_silu(raw, tails[c], cw_ref[:, cols],
                                                            cb_ref[:, cols])
            tails[c] = raw[SUB_ROWS - TAIL_ROWS:]
            if c == 1:
                acs = [_cumsum_time(tril_b, a[rows]) for rows in chunk_rows]
            if c == 5:
                terms = [_ssd_decay_terms(dt[rows], acs_k) for rows, acs_k in zip(chunk_rows, acs)]
            if c == 6:
                preludes = [(acs_k, src_t, _dot(e_in, expand_ref[...]), _dot(w_out, expand_ref[...]))
                            for acs_k, (src_t, e_in, w_out) in zip(acs, terms)]
            yield
        ctx[sb].update(preludes=preludes)

    def scan(sb):
        for k, prelude in enumerate(ctx[sb]["preludes"]):
            yield from _ssd_scan_chunk(xbc_ref, y_ref, state_ref, dskip_ref, sb * SUB_ROWS + k * tc,
                                       prelude, causal, gcol)

    def gate_proj(sb):
        hb = ctx[sb]["hb"]
        gates = []
        for c in range(SSD_D_INNER // MXU_COLS):
            gates.append(_silu(_dot(hb, wz_ref[:, pl.ds(c * MXU_COLS, MXU_COLS)])))
            yield
        ctx[sb].update(gates=gates)

    def out_proj(sb):
        r0 = sb * SUB_ROWS
        sub = pl.ds(r0, SUB_ROWS)
        x = ctx[sb]["x"]
        sumsq = jnp.zeros((SUB_ROWS, 1), F32)
        for c, gate in enumerate(ctx[sb]["gates"]):
            cols = pl.ds(c * MXU_COLS, MXU_COLS)
            gated = y_ref[sub, cols] * gate
            sumsq = sumsq + jnp.sum(gated * gated, axis=-1, keepdims=True)
            y_ref[sub, cols] = gated
        yw = (y_ref[sub, :] * normw_ref[...]).astype(BF16)
        parts = []
        for c in range(D_MODEL // MXU_COLS):
            parts.append(_dot(yw, wout_ref[:, pl.ds(c * MXU_COLS, MXU_COLS)]))
            yield
        m = jnp.concatenate(parts, axis=1) * lax.rsqrt(sumsq * (1.0 / SSD_D_INNER) + EPS)
        _store_rows(o_ref, ostage_ref, r0, x + _rms_norm(m, gpost_ref[...]), False)

    _run(in_proj(0))
    for sb in range(n_sub):
        fill = []
        if sb > 0:
            fill += [gate_proj(sb - 1), out_proj(sb - 1)]
        if sb + 1 < n_sub:
            fill.append(in_proj(sb + 1))
        else:
            fill.append(gate_proj(sb))
        _interleave(scan(sb), _chain(*fill))
    _run(out_proj(n_sub - 1))
    for c in range(n_chunks):
        tail_ref[:, pl.ds(c * MXU_COLS, MXU_COLS)] = tails[c]


def _ssd_layer(layer, is_first, x, gpre, win, cw, cb, dtb, alog, dskip, normw, wout, gpost):
    bsz, length, _ = x.shape
    small = (cw, cb, dtb, alog, dskip, normw)
    return pl.pallas_call(
        functools.partial(_ssd_kernel, layer, is_first),
        name="ssd_mixer",
        grid=(bsz, length // TIME_BLOCK),
        in_specs=([_x_spec(), _const_spec(gpre.shape), _HBM_SPEC]
                  + [_const_spec(c.shape) for c in small] + [_HBM_SPEC, _const_spec(gpost.shape)]),
        out_specs=_x_spec(),
        out_shape=jax.ShapeDtypeStruct(x.shape, x.dtype),
        scratch_shapes=[
            _weight_scratch(D_MODEL, SSD_D_INNER),
            _weight_scratch(D_MODEL, SSD_CONV_DIM),
            _weight_scratch(SSD_D_INNER, D_MODEL),
            _weight_stage(256, D_MODEL),
            _DMA_SEMS,
            pltpu.VMEM((2 * LANES, SSD_D_INNER), BF16),
            pltpu.VMEM((TAIL_ROWS, SSD_CONV_DIM), F32),
            pltpu.VMEM((SSD_N_GROUPS, SSD_D_STATE, GROUP_COLS), F32),
            pltpu.VMEM((TIME_BLOCK, SSD_CONV_DIM), F32),
            pltpu.VMEM((TIME_BLOCK, SSD_D_INNER), F32),
            _stage_scratch(),
            _stage_scratch(),
        ],
        compiler_params=_compiler_params(),
    )(x, gpre, win, *small, wout, gpost)


def _pad_lanes(v):
    return jnp.pad(v, ((0, 0), (0, LANES - v.shape[-1])))


def kernel(x, mix_pre_g, mix_post_g, ffn_pre_g, ffn_post_g, ssd_w_in, ssd_conv_w, ssd_conv_b,
           ssd_dt_bias, ssd_A_log, ssd_D, ssd_norm_w, ssd_w_out, sc_w_in, sc_conv_w, sc_w_out,
           ffn_w_up, ffn_conv_w, ffn_conv_b, ffn_w_down):
    depth = mix_pre_g.shape[0]
    w_in_t = jnp.swapaxes(ssd_w_in, 1, 2)
    row = lambda v: v.reshape(1, -1)
    for i in range(depth):
        j = i // 2
        if i % 2 == 0:
            dskip = jnp.repeat(ssd_D[j], SSD_HEAD_DIM).reshape(1, -1)
            x = _ssd_layer(j, i == 0, x, row(mix_pre_g[i]), w_in_t, ssd_conv_w[j], row(ssd_conv_b[j]),
                           _pad_lanes(row(ssd_dt_bias[j])), _pad_lanes(row(ssd_A_log[j])), dskip,
                           row(ssd_norm_w[j]), ssd_w_out, row(mix_post_g[i]))
        else:
            x = _sc_layer(j, x, row(mix_pre_g[i]), sc_w_in, sc_conv_w[j], sc_w_out,
                          row(mix_post_g[i]))
        x = _ffn_layer(i, i == depth - 1, x, row(ffn_pre_g[i]), ffn_w_up, ffn_conv_w[i], row(ffn_conv_b[i]),
                       ffn_w_down, row(ffn_post_g[i]))
    return x
```

```python
import functools

import jax
import jax.numpy as jnp
from jax import lax
from jax.experimental import pallas as pl
from jax.experimental.pallas import tpu as pltpu

EPS = 1e-6
D_MODEL = 1024

SSD_D_INNER = 2048
SSD_HEAD_DIM = 64
SSD_N_HEADS = 32
SSD_N_GROUPS = 8
SSD_HEADS_PER_GROUP = 4
SSD_D_STATE = 128
SSD_CONV_W = 4
SSD_BC_DIM = SSD_N_GROUPS * SSD_D_STATE
SSD_CONV_DIM = SSD_D_INNER + 2 * SSD_BC_DIM
GROUP_COLS = SSD_HEADS_PER_GROUP * SSD_HEAD_DIM

SC_WIDTH = 1024
SC_CONV_W = 3
FFN_HIDDEN = 2816
FFN_CONV_W = 3

LANES = 128
SUBLANES = 8
MXU_COLS = 256
VMEM_LIMIT_BYTES = 60 * 1024 * 1024

ROW_GROUP = SUBLANES * SUBLANES
TAIL_VREG_ROWS = 3
TAIL_ROWS = TAIL_VREG_ROWS * SUBLANES

TIME_BLOCK = 512
SC_TIME_BLOCK = 1024
SUB_ROWS = 256
SCAN_CHUNK = 128
NEG_BIG = -1e30
STREAM_SLOTS = 3

BF16 = jnp.bfloat16
F32 = jnp.float32


def _rms_norm(x, g):
    ms = jnp.mean(x * x, axis=-1, keepdims=True)
    return x * lax.rsqrt(ms + EPS) * g


def _silu(x):
    hx = 0.5 * x
    return hx * jnp.tanh(hx) + hx


def _softplus(x):
    return jnp.maximum(x, 0.0) + jnp.log1p(jnp.exp(-jnp.abs(x)))


def _dot(a, b):
    return jnp.dot(a, b, preferred_element_type=F32)


def _load_permuted(x_ref, stage_ref, r0, rows, from_time_order):
    if not from_time_order:
        return x_ref[0, pl.ds(r0, rows), :]
    d = x_ref.shape[2]
    for j in range(d // LANES):
        stage_ref[j, pl.ds(r0, rows), :] = x_ref[0, pl.ds(r0, rows), j * LANES:(j + 1) * LANES]
    cols = []
    for j in range(d // LANES):
        tiles = [stage_ref[j, pl.ds(r0 + g * ROW_GROUP + r, SUBLANES, stride=SUBLANES), :]
                 for g in range(rows // ROW_GROUP) for r in range(SUBLANES)]
        cols.append(jnp.concatenate(tiles, axis=0))
    return jnp.concatenate(cols, axis=1)


def _store_rows(o_ref, stage_ref, r0, val, to_time_order):
    rows, d = val.shape
    if not to_time_order:
        o_ref[0, pl.ds(r0, rows), :] = val
        return
    for j in range(d // LANES):
        for g in range(rows // ROW_GROUP):
            for r in range(SUBLANES):
                p0 = g * ROW_GROUP + r * SUBLANES
                stage_ref[j, pl.ds(r0 + g * ROW_GROUP + r, SUBLANES, stride=SUBLANES), :] = (
                    val[p0:p0 + SUBLANES, j * LANES:(j + 1) * LANES])
    for j in range(d // LANES):
        o_ref[0, pl.ds(r0, rows), j * LANES:(j + 1) * LANES] = stage_ref[j, pl.ds(r0, rows), :]


def _time_of_position(p):
    return ((p >> 6) << 6) + ((p & 7) << 3) + ((p >> 3) & 7)


def _shifted(cur, prev_tail, k_max):
    rows = cur.shape[0]
    sub = lax.broadcasted_iota(jnp.int32, (SUBLANES, cur.shape[1]), 0)
    last = sub == SUBLANES - 1
    first_q = SUBLANES - TAIL_VREG_ROWS
    out = [[] for _ in range(k_max)]
    for g in range(rows // ROW_GROUP):
        base = g * ROW_GROUP
        prev = prev_tail if g == 0 else cur[base - TAIL_ROWS:base]
        rolled = {}
        for q in range(SUBLANES - k_max, SUBLANES):
            cur_q = cur[base + q * SUBLANES:base + (q + 1) * SUBLANES]
            prev_q = prev[(q - first_q) * SUBLANES:(q - first_q + 1) * SUBLANES]
            rolled[q] = pltpu.roll(jnp.where(last, prev_q, cur_q), 1, axis=0)
        for k in range(1, k_max + 1):
            pieces = [rolled[q] for q in range(SUBLANES - k, SUBLANES)]
            pieces.append(cur[base:base + (SUBLANES - k) * SUBLANES])
            out[k - 1].append(jnp.concatenate(pieces, axis=0))
    return [jnp.concatenate(o, axis=0) for o in out]


def _causal_conv(cur, prev_tail, w, b=None):
    k = w.shape[0]
    out = cur * w[k - 1:k]
    for s, shifted in enumerate(_shifted(cur, prev_tail, k - 1), start=1):
        out = out + shifted * w[k - 1 - s:k - s]
    if b is not None:
        out = out + b
    return out


def _conv_silu(cur, prev_tail, w, b):
    h = _causal_conv(cur, prev_tail, 0.5 * w, 0.5 * b)
    return h * jnp.tanh(h) + h


def _weight_scratch(rows, cols):
    return pltpu.VMEM((rows, cols + LANES), BF16)


def _weight_stage(row_chunk, cols):
    return pltpu.VMEM((STREAM_SLOTS, row_chunk, cols), F32)


def _stream_weight(src, dst_ref, stage_ref, sem):
    rows, cols = src.shape
    row_chunk = stage_ref.shape[1]
    n = rows // row_chunk
    if cols % LANES:
        last_tile = pl.ds(cols // LANES * LANES, LANES)
        dst_ref[:, last_tile] = jnp.zeros((rows, LANES), BF16)

    def copy(i):
        slot = i % STREAM_SLOTS
        return pltpu.make_async_copy(src.at[pl.ds(i * row_chunk, row_chunk), :],
                                     stage_ref.at[slot], sem.at[slot])

    for i in range(min(STREAM_SLOTS - 1, n)):
        copy(i).start()
    for i in range(n):
        ahead = i + STREAM_SLOTS - 1
        if ahead < n:
            copy(ahead).start()
        copy(i).wait()
        dst_ref[pl.ds(i * row_chunk, row_chunk), pl.ds(0, cols)] = (
            stage_ref[i % STREAM_SLOTS].astype(BF16))


def _stream_weight_transposed(src, dst_ref, stage_ref, sem):
    n_out, _ = src.shape
    chunk = stage_ref.shape[1]
    chunks = [(start, min(chunk, n_out - start)) for start in range(0, n_out, chunk)]

    def padded(size):
        return -(-size // LANES) * LANES

    def copy(i):
        start, size = chunks[i]
        slot = i % STREAM_SLOTS
        return pltpu.make_async_copy(src.at[pl.ds(start, size), :],
                                     stage_ref.at[slot, pl.ds(0, size), :], sem.at[slot])

    def start_copy(i):
        size = chunks[i][1]
        if size < padded(size):
            stage_ref[i % STREAM_SLOTS, pl.ds(size, padded(size) - size), :] = jnp.zeros(
                (padded(size) - size, stage_ref.shape[2]), F32)
        copy(i).start()

    for i in range(min(STREAM_SLOTS - 1, len(chunks))):
        start_copy(i)
    for i in range(len(chunks)):
        ahead = i + STREAM_SLOTS - 1
        if ahead < len(chunks):
            start_copy(ahead)
        copy(i).wait()
        start, size = chunks[i]
        cols = padded(size)
        dst_ref[:, pl.ds(start, cols)] = (
            stage_ref[i % STREAM_SLOTS, pl.ds(0, cols), :].T.astype(BF16))


def _is_first_step():
    return jnp.logical_and(pl.program_id(0) == 0, pl.program_id(1) == 0)


_HBM_SPEC = pl.BlockSpec(memory_space=pl.ANY)
_DMA_SEMS = pltpu.SemaphoreType.DMA((STREAM_SLOTS,))


def _const_spec(shape):
    return pl.BlockSpec(shape, lambda b, t: (0,) * len(shape), pipeline_mode=pl.Buffered(1))


def _x_spec(time_block=TIME_BLOCK):
    return pl.BlockSpec((1, time_block, D_MODEL), lambda b, t: (b, t, 0))


def _stage_scratch(time_block=TIME_BLOCK):
    return pltpu.VMEM((D_MODEL // LANES, time_block, LANES), F32)


def _compiler_params():
    return pltpu.CompilerParams(
        dimension_semantics=("arbitrary", "arbitrary"),
        vmem_limit_bytes=VMEM_LIMIT_BYTES,
    )


def _ffn_kernel(layer, is_last, x_ref, gpre_ref, wup_hbm, cw_ref, cb_ref, wdown_hbm, gpost_ref,
                o_ref, wup_ref, wdown_ref, upstage_ref, downstage_ref, wsem,
                tail_ref, hid_ref, xstage_ref, ostage_ref):
    t = pl.program_id(1)

    @pl.when(_is_first_step())
    def _():
        _stream_weight(wup_hbm.at[layer], wup_ref, upstage_ref, wsem)
        _stream_weight(wdown_hbm.at[layer], wdown_ref, downstage_ref, wsem)

    @pl.when(t == 0)
    def _():
        tail_ref[...] = jnp.zeros_like(tail_ref)

    n_chunks = FFN_HIDDEN // MXU_COLS
    tails = [tail_ref[:, pl.ds(c * MXU_COLS, MXU_COLS)] for c in range(n_chunks)]
    for sb in range(TIME_BLOCK // SUB_ROWS):
        r0 = sb * SUB_ROWS
        x = _load_permuted(x_ref, xstage_ref, r0, SUB_ROWS, False)
        hb = _rms_norm(x, gpre_ref[...]).astype(BF16)
        for c in range(n_chunks):
            cols = pl.ds(c * MXU_COLS, MXU_COLS)
            gate = _dot(hb, wup_ref[:, cols])
            val = _dot(hb, wup_ref[:, pl.ds(FFN_HIDDEN + c * MXU_COLS, MXU_COLS)])
            act = _conv_silu(gate, tails[c], cw_ref[:, cols], cb_ref[:, cols])
            tails[c] = gate[SUB_ROWS - TAIL_ROWS:]
            hid_ref[pl.ds(r0, SUB_ROWS), cols] = (act * val).astype(BF16)
        f = _dot(hid_ref[pl.ds(r0, SUB_ROWS), :], wdown_ref[:, pl.ds(0, D_MODEL)])
        _store_rows(o_ref, ostage_ref, r0, x + _rms_norm(f, gpost_ref[...]), is_last)
    for c in range(n_chunks):
        tail_ref[:, pl.ds(c * MXU_COLS, MXU_COLS)] = tails[c]


def _ffn_layer(layer, is_last, x, gpre, wup, cw, cb, wdown, gpost):
    bsz, length, _ = x.shape
    return pl.pallas_call(
        functools.partial(_ffn_kernel, layer, is_last),
        name="conv_ffn",
        grid=(bsz, length // TIME_BLOCK),
        in_specs=[
            _x_spec(),
            _const_spec(gpre.shape),
            _HBM_SPEC,
            _const_spec(cw.shape),
            _const_spec(cb.shape),
            _HBM_SPEC,
            _const_spec(gpost.shape),
        ],
        out_specs=_x_spec(),
        out_shape=jax.ShapeDtypeStruct(x.shape, x.dtype),
        scratch_shapes=[
            pltpu.VMEM((D_MODEL, 2 * FFN_HIDDEN), BF16),
            _weight_scratch(FFN_HIDDEN, D_MODEL),
            _weight_stage(64, 2 * FFN_HIDDEN),
            _weight_stage(256, D_MODEL),
            _DMA_SEMS,
            pltpu.VMEM((TAIL_ROWS, FFN_HIDDEN), F32),
            pltpu.VMEM((TIME_BLOCK, FFN_HIDDEN), BF16),
            _stage_scratch(),
            _stage_scratch(),
        ],
        compiler_params=_compiler_params(),
    )(x, gpre, wup, cw, cb, wdown, gpost)


def _sc_kernel(layer, x_ref, gpre_ref, win_hbm, cw_ref, wout_hbm, gpost_ref, o_ref,
               win_ref, wout_ref, instage_ref, outstage_ref, wsem,
               tail_ref, xstage_ref, ostage_ref):
    t = pl.program_id(1)

    @pl.when(_is_first_step())
    def _():
        _stream_weight(win_hbm.at[layer], win_ref, instage_ref, wsem)
        _stream_weight(wout_hbm.at[layer], wout_ref, outstage_ref, wsem)

    @pl.when(t == 0)
    def _():
        tail_ref[...] = jnp.zeros_like(tail_ref)

    n_chunks = SC_WIDTH // MXU_COLS
    tails = [tail_ref[:, pl.ds(c * MXU_COLS, MXU_COLS)] for c in range(n_chunks)]
    for sb in range(SC_TIME_BLOCK // SUB_ROWS):
        r0 = sb * SUB_ROWS
        x = _load_permuted(x_ref, xstage_ref, r0, SUB_ROWS, False)
        hb = _rms_norm(x, gpre_ref[...]).astype(BF16)
        parts = []
        for c in range(n_chunks):
            cols = pl.ds(c * MXU_COLS, MXU_COLS)
            gb = _dot(hb, win_ref[:, cols])
            gc = _dot(hb, win_ref[:, pl.ds(SC_WIDTH + c * MXU_COLS, MXU_COLS)])
            v = _dot(hb, win_ref[:, pl.ds(2 * SC_WIDTH + c * MXU_COLS, MXU_COLS)])
            gcv = gc * v
            u = _causal_conv(gcv, tails[c], cw_ref[:, cols])
            tails[c] = gcv[SUB_ROWS - TAIL_ROWS:]
            parts.append((gb * u).astype(BF16))
        m = _dot(jnp.concatenate(parts, axis=1), wout_ref[:, pl.ds(0, D_MODEL)])
        _store_rows(o_ref, ostage_ref, r0, x + _rms_norm(m, gpost_ref[...]), False)
    for c in range(n_chunks):
        tail_ref[:, pl.ds(c * MXU_COLS, MXU_COLS)] = tails[c]


def _sc_layer(layer, x, gpre, win, cw, wout, gpost):
    bsz, length, _ = x.shape
    return pl.pallas_call(
        functools.partial(_sc_kernel, layer),
        name="shortconv_mixer",
        grid=(bsz, length // SC_TIME_BLOCK),
        in_specs=[
            _x_spec(SC_TIME_BLOCK),
            _const_spec(gpre.shape),
            _HBM_SPEC,
            _const_spec(cw.shape),
            _HBM_SPEC,
            _const_spec(gpost.shape),
        ],
        out_specs=_x_spec(SC_TIME_BLOCK),
        out_shape=jax.ShapeDtypeStruct(x.shape, x.dtype),
        scratch_shapes=[
            _weight_scratch(D_MODEL, 3 * SC_WIDTH),
            _weight_scratch(SC_WIDTH, D_MODEL),
            _weight_stage(128, 3 * SC_WIDTH),
            _weight_stage(256, D_MODEL),
            _DMA_SEMS,
            pltpu.VMEM((TAIL_ROWS, SC_WIDTH), F32),
            _stage_scratch(SC_TIME_BLOCK),
            _stage_scratch(SC_TIME_BLOCK),
        ],
        compiler_params=_compiler_params(),
    )(x, gpre, win, cw, wout, gpost)


def _split_bf16(v):
    hi = v.astype(BF16)
    lo = (v - hi.astype(F32)).astype(BF16)
    return jnp.concatenate([hi, lo], axis=1)


def _head_expander():
    k = lax.broadcasted_iota(jnp.int32, (2 * LANES, SSD_D_INNER), 0) % LANES
    c = lax.broadcasted_iota(jnp.int32, (2 * LANES, SSD_D_INNER), 1) // SSD_HEAD_DIM
    return jnp.where(k == c, 1.0, 0.0).astype(BF16)


def _cumsum_time(tril_b, a_c):
    hi = a_c.astype(BF16)
    rest = a_c - hi.astype(F32)
    mid = rest.astype(BF16)
    lo = (rest - mid.astype(F32)).astype(BF16)
    return _dot(jnp.concatenate([tril_b, tril_b, tril_b], axis=1),
                jnp.concatenate([hi, mid, lo], axis=0))


def _ssd_decay_terms(dt_c, acs):
    tc = SCAN_CHUNK
    src_t = (acs - jnp.log(dt_c)).T
    total = acs[tc - 1:tc]
    e_in = jnp.exp(acs)
    w_out = jnp.exp(total - acs) * dt_c
    return src_t, _split_bf16(e_in), _split_bf16(w_out)


def _ssd_scan_chunk(xbc_ref, y_ref, state_ref, dskip_ref, r0, prelude, causal, gcol):
    tc = SCAN_CHUNK
    acs, src_t, e_in_x, w_out_x = prelude

    def start_pair(p):
        rows = pl.ds(r0, tc)
        b_pair = [xbc_ref[rows, pl.ds(SSD_D_INNER + (2 * p + i) * SSD_D_STATE, SSD_D_STATE)]
                  .astype(BF16) for i in range(2)]
        c_pair = xbc_ref[rows, pl.ds(SSD_D_INNER + SSD_BC_DIM + 2 * p * SSD_D_STATE,
                                     2 * SSD_D_STATE)].astype(BF16)
        zero = jnp.zeros_like(b_pair[0])
        b_diag = jnp.concatenate([jnp.concatenate([b_pair[0], zero], axis=1),
                                  jnp.concatenate([zero, b_pair[1]], axis=1)], axis=0)
        cb_pair = lax.dot_general(c_pair, b_diag, (((1,), (1,)), ((), ())),
                                  preferred_element_type=F32)
        return b_pair, c_pair, cb_pair

    def prepare(g, pair):
        b_pair, c_pair, cb_pair = pair
        i = g % 2
        cb_mat = cb_pair[:, i * tc:(i + 1) * tc]
        xg = xbc_ref[pl.ds(r0, tc), pl.ds(g * GROUP_COLS, GROUP_COLS)]
        m_parts = []
        x_parts = []
        for j in range(SSD_HEADS_PER_GROUP):
            h = g * SSD_HEADS_PER_GROUP + j
            seg = acs[:, h:h + 1] - src_t[h:h + 1, :]
            lmat = jnp.exp(jnp.where(causal, seg, NEG_BIG))
            m_parts.append((cb_mat * lmat).astype(BF16))
            x_parts.append(jnp.where(gcol == j, xg, 0.0).astype(BF16))
        m_cat = jnp.concatenate(m_parts, axis=1)
        x_bd = jnp.concatenate(x_parts, axis=0)
        return b_pair[i], c_pair[:, i * SSD_D_STATE:(i + 1) * SSD_D_STATE], xg, m_cat, x_bd

    pairs = {0: start_pair(0)}
    prepared = prepare(0, pairs[0])
    for g in range(SSD_N_GROUPS):
        if g % 2 == 0 and g + 2 < SSD_N_GROUPS:
            pairs[g // 2 + 1] = start_pair(g // 2 + 1)
        bg, cg, xg, m_cat, x_bd = prepared
        if g + 1 < SSD_N_GROUPS:
            prepared = prepare(g + 1, pairs[(g + 1) // 2])
        xcols = pl.ds(g * GROUP_COLS, GROUP_COLS)
        y_diag = _dot(m_cat, x_bd)

        e_in_g = e_in_x[:, g * GROUP_COLS:(g + 1) * GROUP_COLS]
        w_out_g = w_out_x[:, g * GROUP_COLS:(g + 1) * GROUP_COLS]
        e_tot_g = e_in_g[tc - 1:tc]
        state = state_ref[g]
        y_off = _dot(cg, state.astype(BF16)) * e_in_g
        y_ref[pl.ds(r0, tc), xcols] = y_diag + y_off + xg * dskip_ref[:, xcols]
        upd = lax.dot_general(bg, (xg * w_out_g).astype(BF16), (((0,), (0,)), ((), ())),
                              preferred_element_type=F32)
        state_ref[g] = state * e_tot_g + upd
        yield


_DONE = object()


def _run(task):
    for _ in task:
        pass


def _chain(*tasks):
    for task in tasks:
        yield from task


def _interleave(primary, secondary, ratio=1.0):
    primary_live = secondary_live = True
    owed = 0.0
    while primary_live or secondary_live:
        if primary_live:
            primary_live = next(primary, _DONE) is not _DONE
        owed = owed + ratio if primary_live else float("inf")
        while secondary_live and owed >= 1.0:
            secondary_live = next(secondary, _DONE) is not _DONE
            owed -= 1.0


def _ssd_kernel(layer, is_first, x_ref, gpre_ref, win_hbm, cw_ref, cb_ref, dtb_ref, alog_ref,
                dskip_ref, normw_ref, wout_hbm, gpost_ref,
                o_ref, wz_ref, wxbc_ref, wout_ref, wstage_ref, wsem,
                expand_ref, tail_ref, state_ref, xbc_ref, y_ref, xstage_ref, ostage_ref):
    t = pl.program_id(1)

    @pl.when(_is_first_step())
    def _():
        w_in_t = win_hbm.at[layer]
        _stream_weight_transposed(w_in_t.at[pl.ds(SSD_D_INNER, SSD_CONV_DIM + SSD_N_HEADS), :],
                                  wxbc_ref, wstage_ref, wsem)
        _stream_weight_transposed(w_in_t.at[pl.ds(0, SSD_D_INNER), :], wz_ref, wstage_ref, wsem)
        _stream_weight(wout_hbm.at[layer], wout_ref, wstage_ref, wsem)
        expand_ref[...] = _head_expander()

    @pl.when(t == 0)
    def _():
        tail_ref[...] = jnp.zeros_like(tail_ref)
        state_ref[...] = jnp.zeros_like(state_ref)

    tc = SCAN_CHUNK
    time_l = _time_of_position(lax.broadcasted_iota(jnp.int32, (tc, tc), 0))
    time_s = _time_of_position(lax.broadcasted_iota(jnp.int32, (tc, tc), 1))
    causal = time_l >= time_s
    tril_b = jnp.where(causal, 1.0, 0.0).astype(BF16)
    gcol = lax.broadcasted_iota(jnp.int32, (tc, GROUP_COLS), 1) // SSD_HEAD_DIM

    n_chunks = SSD_CONV_DIM // MXU_COLS
    n_sub = TIME_BLOCK // SUB_ROWS
    tails = [tail_ref[:, pl.ds(c * MXU_COLS, MXU_COLS)] for c in range(n_chunks)]
    ctx = [dict() for _ in range(n_sub)]

    def in_proj(sb):
        r0 = sb * SUB_ROWS
        x = _load_permuted(x_ref, xstage_ref, r0, SUB_ROWS, is_first)
        hb = _rms_norm(x, gpre_ref[...]).astype(BF16)
        ctx[sb].update(x=x, hb=hb)
        dt_raw = _dot(hb, wxbc_ref[:, pl.ds(SSD_CONV_DIM, LANES)])
        dt = _softplus(dt_raw + dtb_ref[...])
        a = dt * -jnp.exp(alog_ref[...])
        chunk_rows = [slice(k * tc, (k + 1) * tc) for k in range(SUB_ROWS // tc)]
        acs, terms, preludes = [], [], []
        for c in range(n_chunks):
            cols = pl.ds(c * MXU_COLS, MXU_COLS)
            raw = _dot(hb, wxbc_ref[:, cols])
            xbc_ref[pl.ds(r0, SUB_ROWS), cols] = _conv_silu(raw, tails[c], cw_ref[:, cols],
                                                            cb_ref[:, cols])
            tails[c] = raw[SUB_ROWS - TAIL_ROWS:]
            if c == 1:
                acs = [_cumsum_time(tril_b, a[rows]) for rows in chunk_rows]
            if c == 5:
                terms = [_ssd_decay_terms(dt[rows], acs_k) for rows, acs_k in zip(chunk_rows, acs)]
            if c == 6:
                preludes = [(acs_k, src_t, _dot(e_in, expand_ref[...]), _dot(w_out, expand_ref[...]))
                            for acs_k, (src_t, e_in, w_out) in zip(acs, terms)]
            yield
        ctx[sb].update(preludes=preludes)

    def scan(sb):
        for k, prelude in enumerate(ctx[sb]["preludes"]):
            yield from _ssd_scan_chunk(xbc_ref, y_ref, state_ref, dskip_ref, sb * SUB_ROWS + k * tc,
                                       prelude, causal, gcol)

    def gate_proj(sb):
        hb = ctx[sb]["hb"]
        gates = []
        for c in range(SSD_D_INNER // MXU_COLS):
            gates.append(_silu(_dot(hb, wz_ref[:, pl.ds(c * MXU_COLS, MXU_COLS)])))
            yield
        ctx[sb].update(gates=gates)

    def out_proj(sb):
        r0 = sb * SUB_ROWS
        sub = pl.ds(r0, SUB_ROWS)
        x = ctx[sb]["x"]
        sumsq = jnp.zeros((SUB_ROWS, 1), F32)
        for c, gate in enumerate(ctx[sb]["gates"]):
            cols = pl.ds(c * MXU_COLS, MXU_COLS)
            gated = y_ref[sub, cols] * gate
            sumsq = sumsq + jnp.sum(gated * gated, axis=-1, keepdims=True)
            y_ref[sub, cols] = gated
        yw = (y_ref[sub, :] * normw_ref[...]).astype(BF16)
        parts = []
        for c in range(D_MODEL // MXU_COLS):
            parts.append(_dot(yw, wout_ref[:, pl.ds(c * MXU_COLS, MXU_COLS)]))
            yield
        m = jnp.concatenate(parts, axis=1) * lax.rsqrt(sumsq * (1.0 / SSD_D_INNER) + EPS)
        _store_rows(o_ref, ostage_ref, r0, x + _rms_norm(m, gpost_ref[...]), False)

    _run(in_proj(0))
    for sb in range(n_sub):
        fill, steps = [], 0
        if sb > 0:
            fill.append(out_proj(sb - 1))
            steps += D_MODEL // MXU_COLS
        if sb + 1 < n_sub:
            fill.append(in_proj(sb + 1))
            steps += n_chunks
        fill.append(gate_proj(sb))
        steps += SSD_D_INNER // MXU_COLS
        scan_steps = SUB_ROWS // tc * SSD_N_GROUPS
        _interleave(scan(sb), _chain(*fill), steps / scan_steps)
    _run(out_proj(n_sub - 1))
    for c in range(n_chunks):
        tail_ref[:, pl.ds(c * MXU_COLS, MXU_COLS)] = tails[c]


def _ssd_layer(layer, is_first, x, gpre, win, cw, cb, dtb, alog, dskip, normw, wout, gpost):
    bsz, length, _ = x.shape
    small = (cw, cb, dtb, alog, dskip, normw)
    return pl.pallas_call(
        functools.partial(_ssd_kernel, layer, is_first),
        name="ssd_mixer",
        grid=(bsz, length // TIME_BLOCK),
        in_specs=([_x_spec(), _const_spec(gpre.shape), _HBM_SPEC]
                  + [_const_spec(c.shape) for c in small] + [_HBM_SPEC, _const_spec(gpost.shape)]),
        out_specs=_x_spec(),
        out_shape=jax.ShapeDtypeStruct(x.shape, x.dtype),
        scratch_shapes=[
            _weight_scratch(D_MODEL, SSD_D_INNER),
            _weight_scratch(D_MODEL, SSD_CONV_DIM),
            _weight_scratch(SSD_D_INNER, D_MODEL),
            _weight_stage(256, D_MODEL),
            _DMA_SEMS,
            pltpu.VMEM((2 * LANES, SSD_D_INNER), BF16),
            pltpu.VMEM((TAIL_ROWS, SSD_CONV_DIM), F32),
            pltpu.VMEM((SSD_N_GROUPS, SSD_D_STATE, GROUP_COLS), F32),
            pltpu.VMEM((TIME_BLOCK, SSD_CONV_DIM), F32),
            pltpu.VMEM((TIME_BLOCK, SSD_D_INNER), F32),
            _stage_scratch(),
            _stage_scratch(),
        ],
        compiler_params=_compiler_params(),
    )(x, gpre, win, *small, wout, gpost)


def _pad_lanes(v):
    return jnp.pad(v, ((0, 0), (0, LANES - v.shape[-1])))


def kernel(x, mix_pre_g, mix_post_g, ffn_pre_g, ffn_post_g, ssd_w_in, ssd_conv_w, ssd_conv_b,
           ssd_dt_bias, ssd_A_log, ssd_D, ssd_norm_w, ssd_w_out, sc_w_in, sc_conv_w, sc_w_out,
           ffn_w_up, ffn_conv_w, ffn_conv_b, ffn_w_down):
    depth = mix_pre_g.shape[0]
    w_in_t = jnp.swapaxes(ssd_w_in, 1, 2)
    row = lambda v: v.reshape(1, -1)
    for i in range(depth):
        j = i // 2
        if i % 2 == 0:
            dskip = jnp.repeat(ssd_D[j], SSD_HEAD_DIM).reshape(1, -1)
            x = _ssd_layer(j, i == 0, x, row(mix_pre_g[i]), w_in_t, ssd_conv_w[j], row(ssd_conv_b[j]),
                           _pad_lanes(row(ssd_dt_bias[j])), _pad_lanes(row(ssd_A_log[j])), dskip,
                           row(ssd_norm_w[j]), ssd_w_out, row(mix_post_g[i]))
        else:
            x = _sc_layer(j, x, row(mix_pre_g[i]), sc_w_in, sc_conv_w[j], sc_w_out,
                          row(mix_post_g[i]))
        x = _ffn_layer(i, i == depth - 1, x, row(ffn_pre_g[i]), ffn_w_up, ffn_conv_w[i], row(ffn_conv_b[i]),
                       ffn_w_down, row(ffn_post_g[i]))
    return x
```

```python
import functools

import jax
import jax.numpy as jnp
from jax import lax
from jax.experimental import pallas as pl
from jax.experimental.pallas import tpu as pltpu

EPS = 1e-6
D_MODEL = 1024

SSD_D_INNER = 2048
SSD_HEAD_DIM = 64
SSD_N_HEADS = 32
SSD_N_GROUPS = 8
SSD_HEADS_PER_GROUP = 4
SSD_D_STATE = 128
SSD_CONV_W = 4
SSD_BC_DIM = SSD_N_GROUPS * SSD_D_STATE
SSD_CONV_DIM = SSD_D_INNER + 2 * SSD_BC_DIM
GROUP_COLS = SSD_HEADS_PER_GROUP * SSD_HEAD_DIM

SC_WIDTH = 1024
SC_CONV_W = 3
FFN_HIDDEN = 2816
FFN_CONV_W = 3

LANES = 128
SUBLANES = 8
MXU_COLS = 256
VMEM_LIMIT_BYTES = 60 * 1024 * 1024

ROW_GROUP = SUBLANES * SUBLANES
TAIL_VREG_ROWS = 3
TAIL_ROWS = TAIL_VREG_ROWS * SUBLANES

TIME_BLOCK = 512
SC_TIME_BLOCK = 1024
FFN_TIME_BLOCK = 1024
SUB_ROWS = 256
SCAN_CHUNK = 128
NEG_BIG = -1e30
STREAM_SLOTS = 3

BF16 = jnp.bfloat16
F32 = jnp.float32


def _rms_norm(x, g):
    ms = jnp.mean(x * x, axis=-1, keepdims=True)
    return x * lax.rsqrt(ms + EPS) * g


def _silu(x):
    hx = 0.5 * x
    return hx * jnp.tanh(hx) + hx


def _softplus(x):
    return jnp.maximum(x, 0.0) + jnp.log1p(jnp.exp(-jnp.abs(x)))


def _dot(a, b):
    return jnp.dot(a, b, preferred_element_type=F32)


def _load_permuted(x_ref, stage_ref, r0, rows, from_time_order):
    if not from_time_order:
        return x_ref[0, pl.ds(r0, rows), :]
    d = x_ref.shape[2]
    for j in range(d // LANES):
        stage_ref[j, pl.ds(r0, rows), :] = x_ref[0, pl.ds(r0, rows), j * LANES:(j + 1) * LANES]
    cols = []
    for j in range(d // LANES):
        tiles = [stage_ref[j, pl.ds(r0 + g * ROW_GROUP + r, SUBLANES, stride=SUBLANES), :]
                 for g in range(rows // ROW_GROUP) for r in range(SUBLANES)]
        cols.append(jnp.concatenate(tiles, axis=0))
    return jnp.concatenate(cols, axis=1)


def _store_rows(o_ref, stage_ref, r0, val, to_time_order):
    rows, d = val.shape
    if not to_time_order:
        o_ref[0, pl.ds(r0, rows), :] = val
        return
    for j in range(d // LANES):
        for g in range(rows // ROW_GROUP):
            for r in range(SUBLANES):
                p0 = g * ROW_GROUP + r * SUBLANES
                stage_ref[j, pl.ds(r0 + g * ROW_GROUP + r, SUBLANES, stride=SUBLANES), :] = (
                    val[p0:p0 + SUBLANES, j * LANES:(j + 1) * LANES])
    for j in range(d // LANES):
        o_ref[0, pl.ds(r0, rows), j * LANES:(j + 1) * LANES] = stage_ref[j, pl.ds(r0, rows), :]


def _time_of_position(p):
    return ((p >> 6) << 6) + ((p & 7) << 3) + ((p >> 3) & 7)


def _shifted(cur, prev_tail, k_max):
    rows = cur.shape[0]
    sub = lax.broadcasted_iota(jnp.int32, (SUBLANES, cur.shape[1]), 0)
    last = sub == SUBLANES - 1
    first_q = SUBLANES - TAIL_VREG_ROWS
    out = [[] for _ in range(k_max)]
    for g in range(rows // ROW_GROUP):
        base = g * ROW_GROUP
        prev = prev_tail if g == 0 else cur[base - TAIL_ROWS:base]
        rolled = {}
        for q in range(SUBLANES - k_max, SUBLANES):
            cur_q = cur[base + q * SUBLANES:base + (q + 1) * SUBLANES]
            prev_q = prev[(q - first_q) * SUBLANES:(q - first_q + 1) * SUBLANES]
            rolled[q] = pltpu.roll(jnp.where(last, prev_q, cur_q), 1, axis=0)
        for k in range(1, k_max + 1):
            pieces = [rolled[q] for q in range(SUBLANES - k, SUBLANES)]
            pieces.append(cur[base:base + (SUBLANES - k) * SUBLANES])
            out[k - 1].append(jnp.concatenate(pieces, axis=0))
    return [jnp.concatenate(o, axis=0) for o in out]


def _causal_conv(cur, prev_tail, w, b=None):
    k = w.shape[0]
    out = cur * w[k - 1:k]
    for s, shifted in enumerate(_shifted(cur, prev_tail, k - 1), start=1):
        out = out + shifted * w[k - 1 - s:k - s]
    if b is not None:
        out = out + b
    return out


def _conv_silu(cur, prev_tail, w, b):
    h = _causal_conv(cur, prev_tail, 0.5 * w, 0.5 * b)
    return h * jnp.tanh(h) + h


def _weight_scratch(rows, cols):
    return pltpu.VMEM((rows, cols + LANES), BF16)


def _weight_stage(row_chunk, cols):
    return pltpu.VMEM((STREAM_SLOTS, row_chunk, cols), F32)


def _stream_weight(src, dst_ref, stage_ref, sem):
    rows, cols = src.shape
    row_chunk = stage_ref.shape[1]
    n = rows // row_chunk
    if cols % LANES:
        last_tile = pl.ds(cols // LANES * LANES, LANES)
        dst_ref[:, last_tile] = jnp.zeros((rows, LANES), BF16)

    def copy(i):
        slot = i % STREAM_SLOTS
        return pltpu.make_async_copy(src.at[pl.ds(i * row_chunk, row_chunk), :],
                                     stage_ref.at[slot], sem.at[slot])

    for i in range(min(STREAM_SLOTS - 1, n)):
        copy(i).start()
    for i in range(n):
        ahead = i + STREAM_SLOTS - 1
        if ahead < n:
            copy(ahead).start()
        copy(i).wait()
        dst_ref[pl.ds(i * row_chunk, row_chunk), pl.ds(0, cols)] = (
            stage_ref[i % STREAM_SLOTS].astype(BF16))


def _stream_weight_transposed(src, dst_ref, stage_ref, sem):
    n_out, _ = src.shape
    chunk = stage_ref.shape[1]
    chunks = [(start, min(chunk, n_out - start)) for start in range(0, n_out, chunk)]

    def padded(size):
        return -(-size // LANES) * LANES

    def copy(i):
        start, size = chunks[i]
        slot = i % STREAM_SLOTS
        return pltpu.make_async_copy(src.at[pl.ds(start, size), :],
                                     stage_ref.at[slot, pl.ds(0, size), :], sem.at[slot])

    def start_copy(i):
        size = chunks[i][1]
        if size < padded(size):
            stage_ref[i % STREAM_SLOTS, pl.ds(size, padded(size) - size), :] = jnp.zeros(
                (padded(size) - size, stage_ref.shape[2]), F32)
        copy(i).start()

    for i in range(min(STREAM_SLOTS - 1, len(chunks))):
        start_copy(i)
    for i in range(len(chunks)):
        ahead = i + STREAM_SLOTS - 1
        if ahead < len(chunks):
            start_copy(ahead)
        copy(i).wait()
        start, size = chunks[i]
        cols = padded(size)
        dst_ref[:, pl.ds(start, cols)] = (
            stage_ref[i % STREAM_SLOTS, pl.ds(0, cols), :].T.astype(BF16))


def _is_first_step():
    return jnp.logical_and(pl.program_id(0) == 0, pl.program_id(1) == 0)


_HBM_SPEC = pl.BlockSpec(memory_space=pl.ANY)
_DMA_SEMS = pltpu.SemaphoreType.DMA((STREAM_SLOTS,))


def _const_spec(shape):
    return pl.BlockSpec(shape, lambda b, t: (0,) * len(shape), pipeline_mode=pl.Buffered(1))


def _x_spec(time_block=TIME_BLOCK):
    return pl.BlockSpec((1, time_block, D_MODEL), lambda b, t: (b, t, 0))


def _stage_scratch(time_block=TIME_BLOCK):
    return pltpu.VMEM((D_MODEL // LANES, time_block, LANES), F32)


def _compiler_params():
    return pltpu.CompilerParams(
        dimension_semantics=("arbitrary", "arbitrary"),
        vmem_limit_bytes=VMEM_LIMIT_BYTES,
    )


def _ffn_kernel(layer, is_last, x_ref, gpre_ref, wup_hbm, cw_ref, cb_ref, wdown_hbm, gpost_ref,
                o_ref, wup_ref, wdown_ref, upstage_ref, downstage_ref, wsem,
                tail_ref, hid_ref, ostage_ref=None):
    t = pl.program_id(1)

    @pl.when(_is_first_step())
    def _():
        _stream_weight(wup_hbm.at[layer], wup_ref, upstage_ref, wsem)
        _stream_weight(wdown_hbm.at[layer], wdown_ref, downstage_ref, wsem)

    @pl.when(t == 0)
    def _():
        tail_ref[...] = jnp.zeros_like(tail_ref)

    n_chunks = FFN_HIDDEN // MXU_COLS
    tails = [tail_ref[:, pl.ds(c * MXU_COLS, MXU_COLS)] for c in range(n_chunks)]
    for sb in range(FFN_TIME_BLOCK // SUB_ROWS):
        r0 = sb * SUB_ROWS
        x = _load_permuted(x_ref, None, r0, SUB_ROWS, False)
        hb = _rms_norm(x, gpre_ref[...]).astype(BF16)
        for c in range(n_chunks):
            cols = pl.ds(c * MXU_COLS, MXU_COLS)
            gate = _dot(hb, wup_ref[:, cols])
            val = _dot(hb, wup_ref[:, pl.ds(FFN_HIDDEN + c * MXU_COLS, MXU_COLS)])
            act = _conv_silu(gate, tails[c], cw_ref[:, cols], cb_ref[:, cols])
            tails[c] = gate[SUB_ROWS - TAIL_ROWS:]
            hid_ref[pl.ds(r0, SUB_ROWS), cols] = (act * val).astype(BF16)
        f = _dot(hid_ref[pl.ds(r0, SUB_ROWS), :], wdown_ref[:, pl.ds(0, D_MODEL)])
        _store_rows(o_ref, ostage_ref, r0, x + _rms_norm(f, gpost_ref[...]), is_last)
    for c in range(n_chunks):
        tail_ref[:, pl.ds(c * MXU_COLS, MXU_COLS)] = tails[c]


def _ffn_layer(layer, is_last, x, gpre, wup, cw, cb, wdown, gpost):
    bsz, length, _ = x.shape
    return pl.pallas_call(
        functools.partial(_ffn_kernel, layer, is_last),
        name="conv_ffn",
        grid=(bsz, length // FFN_TIME_BLOCK),
        in_specs=[
            _x_spec(FFN_TIME_BLOCK),
            _const_spec(gpre.shape),
            _HBM_SPEC,
            _const_spec(cw.shape),
            _const_spec(cb.shape),
            _HBM_SPEC,
            _const_spec(gpost.shape),
        ],
        out_specs=_x_spec(FFN_TIME_BLOCK),
        out_shape=jax.ShapeDtypeStruct(x.shape, x.dtype),
        scratch_shapes=[
            pltpu.VMEM((D_MODEL, 2 * FFN_HIDDEN), BF16),
            _weight_scratch(FFN_HIDDEN, D_MODEL),
            _weight_stage(64, 2 * FFN_HIDDEN),
            _weight_stage(256, D_MODEL),
            _DMA_SEMS,
            pltpu.VMEM((TAIL_ROWS, FFN_HIDDEN), F32),
            pltpu.VMEM((FFN_TIME_BLOCK, FFN_HIDDEN), BF16),
        ] + ([_stage_scratch(FFN_TIME_BLOCK)] if is_last else []),
        compiler_params=_compiler_params(),
    )(x, gpre, wup, cw, cb, wdown, gpost)


def _sc_kernel(layer, x_ref, gpre_ref, win_hbm, cw_ref, wout_hbm, gpost_ref, o_ref,
               win_ref, wout_ref, instage_ref, outstage_ref, wsem,
               tail_ref, xstage_ref, ostage_ref):
    t = pl.program_id(1)

    @pl.when(_is_first_step())
    def _():
        _stream_weight(win_hbm.at[layer], win_ref, instage_ref, wsem)
        _stream_weight(wout_hbm.at[layer], wout_ref, outstage_ref, wsem)

    @pl.when(t == 0)
    def _():
        tail_ref[...] = jnp.zeros_like(tail_ref)

    n_chunks = SC_WIDTH // MXU_COLS
    tails = [tail_ref[:, pl.ds(c * MXU_COLS, MXU_COLS)] for c in range(n_chunks)]
    for sb in range(SC_TIME_BLOCK // SUB_ROWS):
        r0 = sb * SUB_ROWS
        x = _load_permuted(x_ref, xstage_ref, r0, SUB_ROWS, False)
        hb = _rms_norm(x, gpre_ref[...]).astype(BF16)
        parts = []
        for c in range(n_chunks):
            cols = pl.ds(c * MXU_COLS, MXU_COLS)
            gb = _dot(hb, win_ref[:, cols])
            gc = _dot(hb, win_ref[:, pl.ds(SC_WIDTH + c * MXU_COLS, MXU_COLS)])
            v = _dot(hb, win_ref[:, pl.ds(2 * SC_WIDTH + c * MXU_COLS, MXU_COLS)])
            gcv = gc * v
            u = _causal_conv(gcv, tails[c], cw_ref[:, cols])
            tails[c] = gcv[SUB_ROWS - TAIL_ROWS:]
            parts.append((gb * u).astype(BF16))
        m = _dot(jnp.concatenate(parts, axis=1), wout_ref[:, pl.ds(0, D_MODEL)])
        _store_rows(o_ref, ostage_ref, r0, x + _rms_norm(m, gpost_ref[...]), False)
    for c in range(n_chunks):
        tail_ref[:, pl.ds(c * MXU_COLS, MXU_COLS)] = tails[c]


def _sc_layer(layer, x, gpre, win, cw, wout, gpost):
    bsz, length, _ = x.shape
    return pl.pallas_call(
        functools.partial(_sc_kernel, layer),
        name="shortconv_mixer",
        grid=(bsz, length // SC_TIME_BLOCK),
        in_specs=[
            _x_spec(SC_TIME_BLOCK),
            _const_spec(gpre.shape),
            _HBM_SPEC,
            _const_spec(cw.shape),
            _HBM_SPEC,
            _const_spec(gpost.shape),
        ],
        out_specs=_x_spec(SC_TIME_BLOCK),
        out_shape=jax.ShapeDtypeStruct(x.shape, x.dtype),
        scratch_shapes=[
            _weight_scratch(D_MODEL, 3 * SC_WIDTH),
            _weight_scratch(SC_WIDTH, D_MODEL),
            _weight_stage(128, 3 * SC_WIDTH),
            _weight_stage(256, D_MODEL),
            _DMA_SEMS,
            pltpu.VMEM((TAIL_ROWS, SC_WIDTH), F32),
            _stage_scratch(SC_TIME_BLOCK),
            _stage_scratch(SC_TIME_BLOCK),
        ],
        compiler_params=_compiler_params(),
    )(x, gpre, win, cw, wout, gpost)


def _split_bf16(v):
    hi = v.astype(BF16)
    lo = (v - hi.astype(F32)).astype(BF16)
    return jnp.concatenate([hi, lo], axis=1)


def _head_expander():
    k = lax.broadcasted_iota(jnp.int32, (2 * LANES, SSD_D_INNER), 0) % LANES
    c = lax.broadcasted_iota(jnp.int32, (2 * LANES, SSD_D_INNER), 1) // SSD_HEAD_DIM
    return jnp.where(k == c, 1.0, 0.0).astype(BF16)


def _cumsum_time(tril_b, a_c):
    hi = a_c.astype(BF16)
    rest = a_c - hi.astype(F32)
    mid = rest.astype(BF16)
    lo = (rest - mid.astype(F32)).astype(BF16)
    return _dot(jnp.concatenate([tril_b, tril_b, tril_b], axis=1),
                jnp.concatenate([hi, mid, lo], axis=0))


def _ssd_decay_terms(dt_c, acs):
    tc = SCAN_CHUNK
    src_t = (acs - jnp.log(dt_c)).T
    total = acs[tc - 1:tc]
    e_in = jnp.exp(acs)
    w_out = jnp.exp(total - acs) * dt_c
    return src_t, _split_bf16(e_in), _split_bf16(w_out)


def _ssd_scan_chunk(xbc_ref, y_ref, state_ref, dskip_ref, r0, prelude, causal, gcol):
    tc = SCAN_CHUNK
    acs, src_t, e_in_x, w_out_x = prelude

    def start_pair(p):
        rows = pl.ds(r0, tc)
        b_pair = [xbc_ref[rows, pl.ds(SSD_D_INNER + (2 * p + i) * SSD_D_STATE, SSD_D_STATE)]
                  .astype(BF16) for i in range(2)]
        c_pair = xbc_ref[rows, pl.ds(SSD_D_INNER + SSD_BC_DIM + 2 * p * SSD_D_STATE,
                                     2 * SSD_D_STATE)].astype(BF16)
        zero = jnp.zeros_like(b_pair[0])
        b_diag = jnp.concatenate([jnp.concatenate([b_pair[0], zero], axis=1),
                                  jnp.concatenate([zero, b_pair[1]], axis=1)], axis=0)
        cb_pair = lax.dot_general(c_pair, b_diag, (((1,), (1,)), ((), ())),
                                  preferred_element_type=F32)
        return b_pair, c_pair, cb_pair

    def prepare(g, pair):
        b_pair, c_pair, cb_pair = pair
        i = g % 2
        cb_mat = cb_pair[:, i * tc:(i + 1) * tc]
        xg = xbc_ref[pl.ds(r0, tc), pl.ds(g * GROUP_COLS, GROUP_COLS)]
        m_parts = []
        x_parts = []
        for j in range(SSD_HEADS_PER_GROUP):
            h = g * SSD_HEADS_PER_GROUP + j
            seg = acs[:, h:h + 1] - src_t[h:h + 1, :]
            lmat = jnp.exp(jnp.where(causal, seg, NEG_BIG))
            m_parts.append((cb_mat * lmat).astype(BF16))
            x_parts.append(jnp.where(gcol == j, xg, 0.0).astype(BF16))
        m_cat = jnp.concatenate(m_parts, axis=1)
        x_bd = jnp.concatenate(x_parts, axis=0)
        return b_pair[i], c_pair[:, i * SSD_D_STATE:(i + 1) * SSD_D_STATE], xg, m_cat, x_bd

    pairs = {0: start_pair(0)}
    prepared = prepare(0, pairs[0])
    for g in range(SSD_N_GROUPS):
        if g % 2 == 0 and g + 2 < SSD_N_GROUPS:
            pairs[g // 2 + 1] = start_pair(g // 2 + 1)
        bg, cg, xg, m_cat, x_bd = prepared
        if g + 1 < SSD_N_GROUPS:
            prepared = prepare(g + 1, pairs[(g + 1) // 2])
        xcols = pl.ds(g * GROUP_COLS, GROUP_COLS)
        y_diag = _dot(m_cat, x_bd)

        e_in_g = e_in_x[:, g * GROUP_COLS:(g + 1) * GROUP_COLS]
        w_out_g = w_out_x[:, g * GROUP_COLS:(g + 1) * GROUP_COLS]
        e_tot_g = e_in_g[tc - 1:tc]
        state = state_ref[g]
        y_off = _dot(cg, state.astype(BF16)) * e_in_g
        y_ref[pl.ds(r0, tc), xcols] = y_diag + y_off + xg * dskip_ref[:, xcols]
        upd = lax.dot_general(bg, (xg * w_out_g).astype(BF16), (((0,), (0,)), ((), ())),
                              preferred_element_type=F32)
        state_ref[g] = state * e_tot_g + upd
        yield


_DONE = object()


def _run(task):
    for _ in task:
        pass


def _chain(*tasks):
    for task in tasks:
        yield from task


def _interleave(primary, secondary, ratio=1.0):
    primary_live = secondary_live = True
    owed = 0.0
    while primary_live or secondary_live:
        if primary_live:
            primary_live = next(primary, _DONE) is not _DONE
        owed = owed + ratio if primary_live else float("inf")
        while secondary_live and owed >= 1.0:
            secondary_live = next(secondary, _DONE) is not _DONE
            owed -= 1.0


def _ssd_kernel(layer, is_first, x_ref, gpre_ref, win_hbm, cw_ref, cb_ref, dtb_ref, alog_ref,
                dskip_ref, normw_ref, wout_hbm, gpost_ref,
                o_ref, wz_ref, wxbc_ref, wout_ref, wstage_ref, wsem,
                expand_ref, tail_ref, state_ref, xbc_ref, y_ref, xstage_ref, ostage_ref):
    t = pl.program_id(1)

    @pl.when(_is_first_step())
    def _():
        w_in_t = win_hbm.at[layer]
        _stream_weight_transposed(w_in_t.at[pl.ds(SSD_D_INNER, SSD_CONV_DIM + SSD_N_HEADS), :],
                                  wxbc_ref, wstage_ref, wsem)
        _stream_weight_transposed(w_in_t.at[pl.ds(0, SSD_D_INNER), :], wz_ref, wstage_ref, wsem)
        _stream_weight(wout_hbm.at[layer], wout_ref, wstage_ref, wsem)
        expand_ref[...] = _head_expander()

    @pl.when(t == 0)
    def _():
        tail_ref[...] = jnp.zeros_like(tail_ref)
        state_ref[...] = jnp.zeros_like(state_ref)

    tc = SCAN_CHUNK
    time_l = _time_of_position(lax.broadcasted_iota(jnp.int32, (tc, tc), 0))
    time_s = _time_of_position(lax.broadcasted_iota(jnp.int32, (tc, tc), 1))
    causal = time_l >= time_s
    tril_b = jnp.where(causal, 1.0, 0.0).astype(BF16)
    gcol = lax.broadcasted_iota(jnp.int32, (tc, GROUP_COLS), 1) // SSD_HEAD_DIM

    n_chunks = SSD_CONV_DIM // MXU_COLS
    n_sub = TIME_BLOCK // SUB_ROWS
    tails = [tail_ref[:, pl.ds(c * MXU_COLS, MXU_COLS)] for c in range(n_chunks)]
    ctx = [dict() for _ in range(n_sub)]

    def in_proj(sb):
        r0 = sb * SUB_ROWS
        x = _load_permuted(x_ref, xstage_ref, r0, SUB_ROWS, is_first)
        hb = _rms_norm(x, gpre_ref[...]).astype(BF16)
        ctx[sb].update(x=x, hb=hb)
        dt_raw = _dot(hb, wxbc_ref[:, pl.ds(SSD_CONV_DIM, LANES)])
        dt = _softplus(dt_raw + dtb_ref[...])
        a = dt * -jnp.exp(alog_ref[...])
        chunk_rows = [slice(k * tc, (k + 1) * tc) for k in range(SUB_ROWS // tc)]
        acs, terms, preludes = [], [], []
        for c in range(n_chunks):
            cols = pl.ds(c * MXU_COLS, MXU_COLS)
            raw = _dot(hb, wxbc_ref[:, cols])
            xbc_ref[pl.ds(r0, SUB_ROWS), cols] = _conv_silu(raw, tails[c], cw_ref[:, cols],
                                                            cb_ref[:, cols])
            tails[c] = raw[SUB_ROWS - TAIL_ROWS:]
            if c == 1:
                acs = [_cumsum_time(tril_b, a[rows]) for rows in chunk_rows]
            if c == 5:
                terms = [_ssd_decay_terms(dt[rows], acs_k) for rows, acs_k in zip(chunk_rows, acs)]
            if c == 6:
                preludes = [(acs_k, src_t, _dot(e_in, expand_ref[...]), _dot(w_out, expand_ref[...]))
                            for acs_k, (src_t, e_in, w_out) in zip(acs, terms)]
            yield
        ctx[sb].update(preludes=preludes)

    def scan(sb):
        for k, prelude in enumerate(ctx[sb]["preludes"]):
            yield from _ssd_scan_chunk(xbc_ref, y_ref, state_ref, dskip_ref, sb * SUB_ROWS + k * tc,
                                       prelude, causal, gcol)

    def gate_proj(sb):
        hb = ctx[sb]["hb"]
        gates = []
        for c in range(SSD_D_INNER // MXU_COLS):
            gates.append(_silu(_dot(hb, wz_ref[:, pl.ds(c * MXU_COLS, MXU_COLS)])))
            yield
        ctx[sb].update(gates=gates)

    def out_proj(sb):
        r0 = sb * SUB_ROWS
        sub = pl.ds(r0, SUB_ROWS)
        x = ctx[sb]["x"]
        sumsq = jnp.zeros((SUB_ROWS, 1), F32)
        for c, gate in enumerate(ctx[sb]["gates"]):
            cols = pl.ds(c * MXU_COLS, MXU_COLS)
            gated = y_ref[sub, cols] * gate
            sumsq = sumsq + jnp.sum(gated * gated, axis=-1, keepdims=True)
            y_ref[sub, cols] = gated
        yw = (y_ref[sub, :] * normw_ref[...]).astype(BF16)
        parts = []
        for c in range(D_MODEL // MXU_COLS):
            parts.append(_dot(yw, wout_ref[:, pl.ds(c * MXU_COLS, MXU_COLS)]))
            yield
        m = jnp.concatenate(parts, axis=1) * lax.rsqrt(sumsq * (1.0 / SSD_D_INNER) + EPS)
        _store_rows(o_ref, ostage_ref, r0, x + _rms_norm(m, gpost_ref[...]), False)

    _run(in_proj(0))
    for sb in range(n_sub):
        fill, steps = [], 0
        if sb > 0:
            fill.append(out_proj(sb - 1))
            steps += D_MODEL // MXU_COLS
        if sb + 1 < n_sub:
            fill.append(in_proj(sb + 1))
            steps += n_chunks
        fill.append(gate_proj(sb))
        steps += SSD_D_INNER // MXU_COLS
        scan_steps = SUB_ROWS // tc * SSD_N_GROUPS
        _interleave(scan(sb), _chain(*fill), steps / scan_steps)
    _run(out_proj(n_sub - 1))
    for c in range(n_chunks):
        tail_ref[:, pl.ds(c * MXU_COLS, MXU_COLS)] = tails[c]


def _ssd_layer(layer, is_first, x, gpre, win, cw, cb, dtb, alog, dskip, normw, wout, gpost):
    bsz, length, _ = x.shape
    small = (cw, cb, dtb, alog, dskip, normw)
    return pl.pallas_call(
        functools.partial(_ssd_kernel, layer, is_first),
        name="ssd_mixer",
        grid=(bsz, length // TIME_BLOCK),
        in_specs=([_x_spec(), _const_spec(gpre.shape), _HBM_SPEC]
                  + [_const_spec(c.shape) for c in small] + [_HBM_SPEC, _const_spec(gpost.shape)]),
        out_specs=_x_spec(),
        out_shape=jax.ShapeDtypeStruct(x.shape, x.dtype),
        scratch_shapes=[
            _weight_scratch(D_MODEL, SSD_D_INNER),
            _weight_scratch(D_MODEL, SSD_CONV_DIM),
            _weight_scratch(SSD_D_INNER, D_MODEL),
            _weight_stage(256, D_MODEL),
            _DMA_SEMS,
            pltpu.VMEM((2 * LANES, SSD_D_INNER), BF16),
            pltpu.VMEM((TAIL_ROWS, SSD_CONV_DIM), F32),
            pltpu.VMEM((SSD_N_GROUPS, SSD_D_STATE, GROUP_COLS), F32),
            pltpu.VMEM((TIME_BLOCK, SSD_CONV_DIM), F32),
            pltpu.VMEM((TIME_BLOCK, SSD_D_INNER), F32),
            _stage_scratch(),
            _stage_scratch(),
        ],
        compiler_params=_compiler_params(),
    )(x, gpre, win, *small, wout, gpost)


def _pad_lanes(v):
    return jnp.pad(v, ((0, 0), (0, LANES - v.shape[-1])))


def kernel(x, mix_pre_g, mix_post_g, ffn_pre_g, ffn_post_g, ssd_w_in, ssd_conv_w, ssd_conv_b,
           ssd_dt_bias, ssd_A_log, ssd_D, ssd_norm_w, ssd_w_out, sc_w_in, sc_conv_w, sc_w_out,
           ffn_w_up, ffn_conv_w, ffn_conv_b, ffn_w_down):
    depth = mix_pre_g.shape[0]
    w_in_t = jnp.swapaxes(ssd_w_in, 1, 2)
    row = lambda v: v.reshape(1, -1)
    for i in range(depth):
        j = i // 2
        if i % 2 == 0:
            dskip = jnp.repeat(ssd_D[j], SSD_HEAD_DIM).reshape(1, -1)
            x = _ssd_layer(j, i == 0, x, row(mix_pre_g[i]), w_in_t, ssd_conv_w[j], row(ssd_conv_b[j]),
                           _pad_lanes(row(ssd_dt_bias[j])), _pad_lanes(row(ssd_A_log[j])), dskip,
                           row(ssd_norm_w[j]), ssd_w_out, row(mix_post_g[i]))
        else:
            x = _sc_layer(j, x, row(mix_pre_g[i]), sc_w_in, sc_conv_w[j], sc_w_out,
                          row(mix_post_g[i]))
        x = _ffn_layer(i, i == depth - 1, x, row(ffn_pre_g[i]), ffn_w_up, ffn_conv_w[i], row(ffn_conv_b[i]),
                       ffn_w_down, row(ffn_post_g[i]))
    return x
```

```python
import functools

import jax
import jax.numpy as jnp
from jax import lax
from jax.experimental import pallas as pl
from jax.experimental.pallas import tpu as pltpu

EPS = 1e-6
D_MODEL = 1024

SSD_D_INNER = 2048
SSD_HEAD_DIM = 64
SSD_N_HEADS = 32
SSD_N_GROUPS = 8
SSD_HEADS_PER_GROUP = 4
SSD_D_STATE = 128
SSD_CONV_W = 4
SSD_BC_DIM = SSD_N_GROUPS * SSD_D_STATE
SSD_CONV_DIM = SSD_D_INNER + 2 * SSD_BC_DIM
GROUP_COLS = SSD_HEADS_PER_GROUP * SSD_HEAD_DIM

SC_WIDTH = 1024
SC_CONV_W = 3
FFN_HIDDEN = 2816
FFN_CONV_W = 3

LANES = 128
SUBLANES = 8
MXU_COLS = 256
VMEM_LIMIT_BYTES = 60 * 1024 * 1024

ROW_GROUP = SUBLANES * SUBLANES
TAIL_VREG_ROWS = 3
TAIL_ROWS = TAIL_VREG_ROWS * SUBLANES

TIME_BLOCK = 512
SC_TIME_BLOCK = 1024
SUB_ROWS = 256
SCAN_CHUNK = 128
NEG_BIG = -1e30
STREAM_SLOTS = 3

BF16 = jnp.bfloat16
F32 = jnp.float32


def _rms_norm(x, g):
    ms = jnp.mean(x * x, axis=-1, keepdims=True)
    return x * lax.rsqrt(ms + EPS) * g


def _silu(x):
    hx = 0.5 * x
    return hx * jnp.tanh(hx) + hx


def _softplus(x):
    return jnp.maximum(x, 0.0) + jnp.log1p(jnp.exp(-jnp.abs(x)))


def _dot(a, b):
    return jnp.dot(a, b, preferred_element_type=F32)


def _swap_row_order(v):
    rows = v.shape[0]
    sub = lax.broadcasted_iota(jnp.int32, (SUBLANES, v.shape[1]), 0)
    out = []
    for g in range(rows // ROW_GROUP):
        tiles = [v[g * ROW_GROUP + r * SUBLANES:g * ROW_GROUP + (r + 1) * SUBLANES]
                 for r in range(SUBLANES)]
        for k in (4, 2, 1):
            keep = (sub & k) == 0
            swapped = list(tiles)
            for i in range(SUBLANES):
                if i & k:
                    continue
                lo, hi = tiles[i], tiles[i + k]
                swapped[i] = jnp.where(keep, lo, pltpu.roll(hi, k, axis=0))
                swapped[i + k] = jnp.where(keep, pltpu.roll(lo, SUBLANES - k, axis=0), hi)
            tiles = swapped
        out += tiles
    return jnp.concatenate(out, axis=0)


def _load_rows(x_ref, r0, rows, from_time_order):
    x = x_ref[0, pl.ds(r0, rows), :]
    return _swap_row_order(x) if from_time_order else x


def _store_rows(o_ref, r0, val, to_time_order):
    o_ref[0, pl.ds(r0, val.shape[0]), :] = _swap_row_order(val) if to_time_order else val


def _time_of_position(p):
    return ((p >> 6) << 6) + ((p & 7) << 3) + ((p >> 3) & 7)


def _shifted(cur, prev_tail, k_max):
    rows = cur.shape[0]
    sub = lax.broadcasted_iota(jnp.int32, (SUBLANES, cur.shape[1]), 0)
    last = sub == SUBLANES - 1
    first_q = SUBLANES - TAIL_VREG_ROWS
    out = [[] for _ in range(k_max)]
    for g in range(rows // ROW_GROUP):
        base = g * ROW_GROUP
        prev = prev_tail if g == 0 else cur[base - TAIL_ROWS:base]
        rolled = {}
        for q in range(SUBLANES - k_max, SUBLANES):
            cur_q = cur[base + q * SUBLANES:base + (q + 1) * SUBLANES]
            prev_q = prev[(q - first_q) * SUBLANES:(q - first_q + 1) * SUBLANES]
            rolled[q] = pltpu.roll(jnp.where(last, prev_q, cur_q), 1, axis=0)
        for k in range(1, k_max + 1):
            pieces = [rolled[q] for q in range(SUBLANES - k, SUBLANES)]
            pieces.append(cur[base:base + (SUBLANES - k) * SUBLANES])
            out[k - 1].append(jnp.concatenate(pieces, axis=0))
    return [jnp.concatenate(o, axis=0) for o in out]


def _causal_conv(cur, prev_tail, w, b=None):
    k = w.shape[0]
    out = cur * w[k - 1:k]
    for s, shifted in enumerate(_shifted(cur, prev_tail, k - 1), start=1):
        out = out + shifted * w[k - 1 - s:k - s]
    if b is not None:
        out = out + b
    return out


def _conv_silu(cur, prev_tail, w, b):
    h = _causal_conv(cur, prev_tail, 0.5 * w, 0.5 * b)
    return h * jnp.tanh(h) + h


def _weight_scratch(rows, cols):
    return pltpu.VMEM((rows, cols + LANES), BF16)


def _weight_stage(row_chunk, cols):
    return pltpu.VMEM((STREAM_SLOTS, row_chunk, cols), F32)


def _stream_weight(src, dst_ref, stage_ref, sem):
    rows, cols = src.shape
    row_chunk = stage_ref.shape[1]
    n = rows // row_chunk
    if cols % LANES:
        last_tile = pl.ds(cols // LANES * LANES, LANES)
        dst_ref[:, last_tile] = jnp.zeros((rows, LANES), BF16)

    def copy(i):
        slot = i % STREAM_SLOTS
        return pltpu.make_async_copy(src.at[pl.ds(i * row_chunk, row_chunk), :],
                                     stage_ref.at[slot], sem.at[slot])

    for i in range(min(STREAM_SLOTS - 1, n)):
        copy(i).start()
    for i in range(n):
        ahead = i + STREAM_SLOTS - 1
        if ahead < n:
            copy(ahead).start()
        copy(i).wait()
        dst_ref[pl.ds(i * row_chunk, row_chunk), pl.ds(0, cols)] = (
            stage_ref[i % STREAM_SLOTS].astype(BF16))


def _stream_weight_transposed(src, dst_ref, stage_ref, sem):
    n_out, _ = src.shape
    chunk = stage_ref.shape[1]
    chunks = [(start, min(chunk, n_out - start)) for start in range(0, n_out, chunk)]

    def padded(size):
        return -(-size // LANES) * LANES

    def copy(i):
        start, size = chunks[i]
        slot = i % STREAM_SLOTS
        return pltpu.make_async_copy(src.at[pl.ds(start, size), :],
                                     stage_ref.at[slot, pl.ds(0, size), :], sem.at[slot])

    def start_copy(i):
        size = chunks[i][1]
        if size < padded(size):
            stage_ref[i % STREAM_SLOTS, pl.ds(size, padded(size) - size), :] = jnp.zeros(
                (padded(size) - size, stage_ref.shape[2]), F32)
        copy(i).start()

    for i in range(min(STREAM_SLOTS - 1, len(chunks))):
        start_copy(i)
    for i in range(len(chunks)):
        ahead = i + STREAM_SLOTS - 1
        if ahead < len(chunks):
            start_copy(ahead)
        copy(i).wait()
        start, size = chunks[i]
        cols = padded(size)
        dst_ref[:, pl.ds(start, cols)] = (
            stage_ref[i % STREAM_SLOTS, pl.ds(0, cols), :].T.astype(BF16))


def _is_first_step():
    return jnp.logical_and(pl.program_id(0) == 0, pl.program_id(1) == 0)


_HBM_SPEC = pl.BlockSpec(memory_space=pl.ANY)
_DMA_SEMS = pltpu.SemaphoreType.DMA((STREAM_SLOTS,))


def _const_spec(shape):
    return pl.BlockSpec(shape, lambda b, t: (0,) * len(shape), pipeline_mode=pl.Buffered(1))


def _x_spec(time_block=TIME_BLOCK):
    return pl.BlockSpec((1, time_block, D_MODEL), lambda b, t: (b, t, 0))


def _compiler_params():
    return pltpu.CompilerParams(
        dimension_semantics=("arbitrary", "arbitrary"),
        vmem_limit_bytes=VMEM_LIMIT_BYTES,
    )


def _ffn_kernel(layer, is_last, x_ref, gpre_ref, wup_hbm, cw_ref, cb_ref, wdown_hbm, gpost_ref,
                o_ref, wup_ref, wdown_ref, upstage_ref, downstage_ref, wsem,
                tail_ref, hid_ref):
    t = pl.program_id(1)

    @pl.when(_is_first_step())
    def _():
        _stream_weight(wup_hbm.at[layer], wup_ref, upstage_ref, wsem)
        _stream_weight(wdown_hbm.at[layer], wdown_ref, downstage_ref, wsem)

    @pl.when(t == 0)
    def _():
        tail_ref[...] = jnp.zeros_like(tail_ref)

    n_chunks = FFN_HIDDEN // MXU_COLS
    tails = [tail_ref[:, pl.ds(c * MXU_COLS, MXU_COLS)] for c in range(n_chunks)]
    for sb in range(TIME_BLOCK // SUB_ROWS):
        r0 = sb * SUB_ROWS
        x = _load_rows(x_ref, r0, SUB_ROWS, False)
        hb = _rms_norm(x, gpre_ref[...]).astype(BF16)
        for c in range(n_chunks):
            cols = pl.ds(c * MXU_COLS, MXU_COLS)
            gate = _dot(hb, wup_ref[:, cols])
            val = _dot(hb, wup_ref[:, pl.ds(FFN_HIDDEN + c * MXU_COLS, MXU_COLS)])
            act = _conv_silu(gate, tails[c], cw_ref[:, cols], cb_ref[:, cols])
            tails[c] = gate[SUB_ROWS - TAIL_ROWS:]
            hid_ref[pl.ds(r0, SUB_ROWS), cols] = (act * val).astype(BF16)
        f = _dot(hid_ref[pl.ds(r0, SUB_ROWS), :], wdown_ref[:, pl.ds(0, D_MODEL)])
        _store_rows(o_ref, r0, x + _rms_norm(f, gpost_ref[...]), is_last)
    for c in range(n_chunks):
        tail_ref[:, pl.ds(c * MXU_COLS, MXU_COLS)] = tails[c]


def _ffn_layer(layer, is_last, x, gpre, wup, cw, cb, wdown, gpost):
    bsz, length, _ = x.shape
    return pl.pallas_call(
        functools.partial(_ffn_kernel, layer, is_last),
        name="conv_ffn",
        grid=(bsz, length // TIME_BLOCK),
        in_specs=[
            _x_spec(),
            _const_spec(gpre.shape),
            _HBM_SPEC,
            _const_spec(cw.shape),
            _const_spec(cb.shape),
            _HBM_SPEC,
            _const_spec(gpost.shape),
        ],
        out_specs=_x_spec(),
        out_shape=jax.ShapeDtypeStruct(x.shape, x.dtype),
        scratch_shapes=[
            pltpu.VMEM((D_MODEL, 2 * FFN_HIDDEN), BF16),
            _weight_scratch(FFN_HIDDEN, D_MODEL),
            _weight_stage(64, 2 * FFN_HIDDEN),
            _weight_stage(256, D_MODEL),
            _DMA_SEMS,
            pltpu.VMEM((TAIL_ROWS, FFN_HIDDEN), F32),
            pltpu.VMEM((TIME_BLOCK, FFN_HIDDEN), BF16),
        ],
        compiler_params=_compiler_params(),
    )(x, gpre, wup, cw, cb, wdown, gpost)


def _sc_kernel(layer, x_ref, gpre_ref, win_hbm, cw_ref, wout_hbm, gpost_ref, o_ref,
               win_ref, wout_ref, instage_ref, outstage_ref, wsem,
               tail_ref):
    t = pl.program_id(1)

    @pl.when(_is_first_step())
    def _():
        _stream_weight(win_hbm.at[layer], win_ref, instage_ref, wsem)
        _stream_weight(wout_hbm.at[layer], wout_ref, outstage_ref, wsem)

    @pl.when(t == 0)
    def _():
        tail_ref[...] = jnp.zeros_like(tail_ref)

    n_chunks = SC_WIDTH // MXU_COLS
    tails = [tail_ref[:, pl.ds(c * MXU_COLS, MXU_COLS)] for c in range(n_chunks)]
    for sb in range(SC_TIME_BLOCK // SUB_ROWS):
        r0 = sb * SUB_ROWS
        x = _load_rows(x_ref, r0, SUB_ROWS, False)
        hb = _rms_norm(x, gpre_ref[...]).astype(BF16)
        parts = []
        for c in range(n_chunks):
            cols = pl.ds(c * MXU_COLS, MXU_COLS)
            gb = _dot(hb, win_ref[:, cols])
            gc = _dot(hb, win_ref[:, pl.ds(SC_WIDTH + c * MXU_COLS, MXU_COLS)])
            v = _dot(hb, win_ref[:, pl.ds(2 * SC_WIDTH + c * MXU_COLS, MXU_COLS)])
            gcv = gc * v
            u = _causal_conv(gcv, tails[c], cw_ref[:, cols])
            tails[c] = gcv[SUB_ROWS - TAIL_ROWS:]
            parts.append((gb * u).astype(BF16))
        m = _dot(jnp.concatenate(parts, axis=1), wout_ref[:, pl.ds(0, D_MODEL)])
        _store_rows(o_ref, r0, x + _rms_norm(m, gpost_ref[...]), False)
    for c in range(n_chunks):
        tail_ref[:, pl.ds(c * MXU_COLS, MXU_COLS)] = tails[c]


def _sc_layer(layer, x, gpre, win, cw, wout, gpost):
    bsz, length, _ = x.shape
    return pl.pallas_call(
        functools.partial(_sc_kernel, layer),
        name="shortconv_mixer",
        grid=(bsz, length // SC_TIME_BLOCK),
        in_specs=[
            _x_spec(SC_TIME_BLOCK),
            _const_spec(gpre.shape),
            _HBM_SPEC,
            _const_spec(cw.shape),
            _HBM_SPEC,
            _const_spec(gpost.shape),
        ],
        out_specs=_x_spec(SC_TIME_BLOCK),
        out_shape=jax.ShapeDtypeStruct(x.shape, x.dtype),
        scratch_shapes=[
            _weight_scratch(D_MODEL, 3 * SC_WIDTH),
            _weight_scratch(SC_WIDTH, D_MODEL),
            _weight_stage(128, 3 * SC_WIDTH),
            _weight_stage(256, D_MODEL),
            _DMA_SEMS,
            pltpu.VMEM((TAIL_ROWS, SC_WIDTH), F32),
        ],
        compiler_params=_compiler_params(),
    )(x, gpre, win, cw, wout, gpost)


def _split_bf16(v):
    hi = v.astype(BF16)
    lo = (v - hi.astype(F32)).astype(BF16)
    return jnp.concatenate([hi, lo], axis=1)


def _head_expander():
    k = lax.broadcasted_iota(jnp.int32, (2 * LANES, SSD_D_INNER), 0) % LANES
    c = lax.broadcasted_iota(jnp.int32, (2 * LANES, SSD_D_INNER), 1) // SSD_HEAD_DIM
    return jnp.where(k == c, 1.0, 0.0).astype(BF16)


def _cumsum_time(tril_b, a_c):
    hi = a_c.astype(BF16)
    rest = a_c - hi.astype(F32)
    mid = rest.astype(BF16)
    lo = (rest - mid.astype(F32)).astype(BF16)
    return _dot(jnp.concatenate([tril_b, tril_b, tril_b], axis=1),
                jnp.concatenate([hi, mid, lo], axis=0))


def _ssd_decay_terms(dt_c, acs):
    tc = SCAN_CHUNK
    src_t = (acs - jnp.log(dt_c)).T
    total = acs[tc - 1:tc]
    e_in = jnp.exp(acs)
    w_out = jnp.exp(total - acs) * dt_c
    return src_t, _split_bf16(e_in), _split_bf16(w_out)


def _ssd_scan_chunk(xbc_ref, y_ref, state_ref, dskip_ref, r0, prelude, causal, gcol):
    tc = SCAN_CHUNK
    acs, src_t, e_in_x, w_out_x = prelude

    def start_pair(p):
        rows = pl.ds(r0, tc)
        b_pair = [xbc_ref[rows, pl.ds(SSD_D_INNER + (2 * p + i) * SSD_D_STATE, SSD_D_STATE)]
                  .astype(BF16) for i in range(2)]
        c_pair = xbc_ref[rows, pl.ds(SSD_D_INNER + SSD_BC_DIM + 2 * p * SSD_D_STATE,
                                     2 * SSD_D_STATE)].astype(BF16)
        zero = jnp.zeros_like(b_pair[0])
        b_diag = jnp.concatenate([jnp.concatenate([b_pair[0], zero], axis=1),
                                  jnp.concatenate([zero, b_pair[1]], axis=1)], axis=0)
        cb_pair = lax.dot_general(c_pair, b_diag, (((1,), (1,)), ((), ())),
                                  preferred_element_type=F32)
        return b_pair, c_pair, cb_pair

    def prepare(g, pair):
        b_pair, c_pair, cb_pair = pair
        i = g % 2
        cb_mat = cb_pair[:, i * tc:(i + 1) * tc]
        xg = xbc_ref[pl.ds(r0, tc), pl.ds(g * GROUP_COLS, GROUP_COLS)]
        m_parts = []
        x_parts = []
        for j in range(SSD_HEADS_PER_GROUP):
            h = g * SSD_HEADS_PER_GROUP + j
            seg = acs[:, h:h + 1] - src_t[h:h + 1, :]
            lmat = jnp.exp(jnp.where(causal, seg, NEG_BIG))
            m_parts.append((cb_mat * lmat).astype(BF16))
            x_parts.append(jnp.where(gcol == j, xg, 0.0).astype(BF16))
        m_cat = jnp.concatenate(m_parts, axis=1)
        x_bd = jnp.concatenate(x_parts, axis=0)
        return b_pair[i], c_pair[:, i * SSD_D_STATE:(i + 1) * SSD_D_STATE], xg, m_cat, x_bd

    pairs = {0: start_pair(0)}
    prepared = prepare(0, pairs[0])
    for g in range(SSD_N_GROUPS):
        if g % 2 == 0 and g + 2 < SSD_N_GROUPS:
            pairs[g // 2 + 1] = start_pair(g // 2 + 1)
        bg, cg, xg, m_cat, x_bd = prepared
        if g + 1 < SSD_N_GROUPS:
            prepared = prepare(g + 1, pairs[(g + 1) // 2])
        xcols = pl.ds(g * GROUP_COLS, GROUP_COLS)
        y_diag = _dot(m_cat, x_bd)

        e_in_g = e_in_x[:, g * GROUP_COLS:(g + 1) * GROUP_COLS]
        w_out_g = w_out_x[:, g * GROUP_COLS:(g + 1) * GROUP_COLS]
        e_tot_g = e_in_g[tc - 1:tc]
        state = state_ref[g]
        y_off = _dot(cg, state.astype(BF16)) * e_in_g
        y_ref[pl.ds(r0, tc), xcols] = y_diag + y_off + xg * dskip_ref[:, xcols]
        upd = lax.dot_general(bg, (xg * w_out_g).astype(BF16), (((0,), (0,)), ((), ())),
                              preferred_element_type=F32)
        state_ref[g] = state * e_tot_g + upd
        yield


_DONE = object()


def _run(task):
    for _ in task:
        pass


def _chain(*tasks):
    for task in tasks:
        yield from task


def _interleave(primary, secondary, ratio=1.0):
    primary_live = secondary_live = True
    owed = 0.0
    while primary_live or secondary_live:
        if primary_live:
            primary_live = next(primary, _DONE) is not _DONE
        owed = owed + ratio if primary_live else float("inf")
        while secondary_live and owed >= 1.0:
            secondary_live = next(secondary, _DONE) is not _DONE
            owed -= 1.0


def _ssd_kernel(layer, is_first, x_ref, gpre_ref, win_hbm, cw_ref, cb_ref, dtb_ref, alog_ref,
                dskip_ref, normw_ref, wout_hbm, gpost_ref,
                o_ref, wz_ref, wxbc_ref, wout_ref, wstage_ref, wsem,
                expand_ref, tail_ref, state_ref, xbc_ref, y_ref):
    t = pl.program_id(1)

    @pl.when(_is_first_step())
    def _():
        w_in_t = win_hbm.at[layer]
        _stream_weight_transposed(w_in_t.at[pl.ds(SSD_D_INNER, SSD_CONV_DIM + SSD_N_HEADS), :],
                                  wxbc_ref, wstage_ref, wsem)
        _stream_weight_transposed(w_in_t.at[pl.ds(0, SSD_D_INNER), :], wz_ref, wstage_ref, wsem)
        _stream_weight(wout_hbm.at[layer], wout_ref, wstage_ref, wsem)
        expand_ref[...] = _head_expander()

    @pl.when(t == 0)
    def _():
        tail_ref[...] = jnp.zeros_like(tail_ref)
        state_ref[...] = jnp.zeros_like(state_ref)

    tc = SCAN_CHUNK
    time_l = _time_of_position(lax.broadcasted_iota(jnp.int32, (tc, tc), 0))
    time_s = _time_of_position(lax.broadcasted_iota(jnp.int32, (tc, tc), 1))
    causal = time_l >= time_s
    tril_b = jnp.where(causal, 1.0, 0.0).astype(BF16)
    gcol = lax.broadcasted_iota(jnp.int32, (tc, GROUP_COLS), 1) // SSD_HEAD_DIM

    n_chunks = SSD_CONV_DIM // MXU_COLS
    n_sub = TIME_BLOCK // SUB_ROWS
    tails = [tail_ref[:, pl.ds(c * MXU_COLS, MXU_COLS)] for c in range(n_chunks)]
    ctx = [dict() for _ in range(n_sub)]

    def in_proj(sb):
        r0 = sb * SUB_ROWS
        x = _load_rows(x_ref, r0, SUB_ROWS, is_first)
        hb = _rms_norm(x, gpre_ref[...]).astype(BF16)
        ctx[sb].update(x=x, hb=hb)
        dt_raw = _dot(hb, wxbc_ref[:, pl.ds(SSD_CONV_DIM, LANES)])
        dt = _softplus(dt_raw + dtb_ref[...])
        a = dt * -jnp.exp(alog_ref[...])
        chunk_rows = [slice(k * tc, (k + 1) * tc) for k in range(SUB_ROWS // tc)]
        acs, terms, preludes = [], [], []
        for c in range(n_chunks):
            cols = pl.ds(c * MXU_COLS, MXU_COLS)
            raw = _dot(hb, wxbc_ref[:, cols])
            xbc_ref[pl.ds(r0, SUB_ROWS), cols] = _conv_silu(raw, tails[c], cw_ref[:, cols],
                                                            cb_ref[:, cols])
            tails[c] = raw[SUB_ROWS - TAIL_ROWS:]
            if c == 1:
                acs = [_cumsum_time(tril_b, a[rows]) for rows in chunk_rows]
            if c == 5:
                terms = [_ssd_decay_terms(dt[rows], acs_k) for rows, acs_k in zip(chunk_rows, acs)]
            if c == 6:
                preludes = [(acs_k, src_t, _dot(e_in, expand_ref[...]), _dot(w_out, expand_ref[...]))
                            for acs_k, (src_t, e_in, w_out) in zip(acs, terms)]
            yield
        ctx[sb].update(preludes=preludes)

    def scan(sb):
        for k, prelude in enumerate(ctx[sb]["preludes"]):
            yield from _ssd_scan_chunk(xbc_ref, y_ref, state_ref, dskip_ref, sb * SUB_ROWS + k * tc,
                                       prelude, causal, gcol)

    def gate_proj(sb):
        hb = ctx[sb]["hb"]
        gates = []
        for c in range(SSD_D_INNER // MXU_COLS):
            gates.append(_silu(_dot(hb, wz_ref[:, pl.ds(c * MXU_COLS, MXU_COLS)])))
            yield
        ctx[sb].update(gates=gates)

    def out_proj(sb):
        r0 = sb * SUB_ROWS
        sub = pl.ds(r0, SUB_ROWS)
        x = ctx[sb]["x"]
        sumsq = jnp.zeros((SUB_ROWS, 1), F32)
        for c, gate in enumerate(ctx[sb]["gates"]):
            cols = pl.ds(c * MXU_COLS, MXU_COLS)
            gated = y_ref[sub, cols] * gate
            sumsq = sumsq + jnp.sum(gated * gated, axis=-1, keepdims=True)
            y_ref[sub, cols] = gated
        yw = (y_ref[sub, :] * normw_ref[...]).astype(BF16)
        parts = []
        for c in range(D_MODEL // MXU_COLS):
            parts.append(_dot(yw, wout_ref[:, pl.ds(c * MXU_COLS, MXU_COLS)]))
            yield
        m = jnp.concatenate(parts, axis=1) * lax.rsqrt(sumsq * (1.0 / SSD_D_INNER) + EPS)
        _store_rows(o_ref, r0, x + _rms_norm(m, gpost_ref[...]), False)

    _run(in_proj(0))
    for sb in range(n_sub):
        fill, steps = [], 0
        if sb > 0:
            fill.append(out_proj(sb - 1))
            steps += D_MODEL // MXU_COLS
        if sb + 1 < n_sub:
            fill.append(in_proj(sb + 1))
            steps += n_chunks
        fill.append(gate_proj(sb))
        steps += SSD_D_INNER // MXU_COLS
        scan_steps = SUB_ROWS // tc * SSD_N_GROUPS
        _interleave(scan(sb), _chain(*fill), steps / scan_steps)
    _run(out_proj(n_sub - 1))
    for c in range(n_chunks):
        tail_ref[:, pl.ds(c * MXU_COLS, MXU_COLS)] = tails[c]


def _ssd_layer(layer, is_first, x, gpre, win, cw, cb, dtb, alog, dskip, normw, wout, gpost):
    bsz, length, _ = x.shape
    small = (cw, cb, dtb, alog, dskip, normw)
    return pl.pallas_call(
        functools.partial(_ssd_kernel, layer, is_first),
        name="ssd_mixer",
        grid=(bsz, length // TIME_BLOCK),
        in_specs=([_x_spec(), _const_spec(gpre.shape), _HBM_SPEC]
                  + [_const_spec(c.shape) for c in small] + [_HBM_SPEC, _const_spec(gpost.shape)]),
        out_specs=_x_spec(),
        out_shape=jax.ShapeDtypeStruct(x.shape, x.dtype),
        scratch_shapes=[
            _weight_scratch(D_MODEL, SSD_D_INNER),
            _weight_scratch(D_MODEL, SSD_CONV_DIM),
            _weight_scratch(SSD_D_INNER, D_MODEL),
            _weight_stage(256, D_MODEL),
            _DMA_SEMS,
            pltpu.VMEM((2 * LANES, SSD_D_INNER), BF16),
            pltpu.VMEM((TAIL_ROWS, SSD_CONV_DIM), F32),
            pltpu.VMEM((SSD_N_GROUPS, SSD_D_STATE, GROUP_COLS), F32),
            pltpu.VMEM((TIME_BLOCK, SSD_CONV_DIM), F32),
            pltpu.VMEM((TIME_BLOCK, SSD_D_INNER), F32),
        ],
        compiler_params=_compiler_params(),
    )(x, gpre, win, *small, wout, gpost)


def _pad_lanes(v):
    return jnp.pad(v, ((0, 0), (0, LANES - v.shape[-1])))


def kernel(x, mix_pre_g, mix_post_g, ffn_pre_g, ffn_post_g, ssd_w_in, ssd_conv_w, ssd_conv_b,
           ssd_dt_bias, ssd_A_log, ssd_D, ssd_norm_w, ssd_w_out, sc_w_in, sc_conv_w, sc_w_out,
           ffn_w_up, ffn_conv_w, ffn_conv_b, ffn_w_down):
    depth = mix_pre_g.shape[0]
    w_in_t = jnp.swapaxes(ssd_w_in, 1, 2)
    row = lambda v: v.reshape(1, -1)
    for i in range(depth):
        j = i // 2
        if i % 2 == 0:
            dskip = jnp.repeat(ssd_D[j], SSD_HEAD_DIM).reshape(1, -1)
            x = _ssd_layer(j, i == 0, x, row(mix_pre_g[i]), w_in_t, ssd_conv_w[j], row(ssd_conv_b[j]),
                           _pad_lanes(row(ssd_dt_bias[j])), _pad_lanes(row(ssd_A_log[j])), dskip,
                           row(ssd_norm_w[j]), ssd_w_out, row(mix_post_g[i]))
        else:
            x = _sc_layer(j, x, row(mix_pre_g[i]), sc_w_in, sc_conv_w[j], sc_w_out,
                          row(mix_post_g[i]))
        x = _ffn_layer(i, i == depth - 1, x, row(ffn_pre_g[i]), ffn_w_up, ffn_conv_w[i], row(ffn_conv_b[i]),
                       ffn_w_down, row(ffn_post_g[i]))
    return x
```

```python
import functools

import jax
import jax.numpy as jnp
from jax import lax
from jax.experimental import pallas as pl
from jax.experimental.pallas import tpu as pltpu

EPS = 1e-6
D_MODEL = 1024

SSD_D_INNER = 2048
SSD_HEAD_DIM = 64
SSD_N_HEADS = 32
SSD_N_GROUPS = 8
SSD_HEADS_PER_GROUP = 4
SSD_D_STATE = 128
SSD_CONV_W = 4
SSD_BC_DIM = SSD_N_GROUPS * SSD_D_STATE
SSD_CONV_DIM = SSD_D_INNER + 2 * SSD_BC_DIM
GROUP_COLS = SSD_HEADS_PER_GROUP * SSD_HEAD_DIM

SC_WIDTH = 1024
SC_CONV_W = 3
FFN_HIDDEN = 2816
FFN_CONV_W = 3

LANES = 128
SUBLANES = 8
MXU_COLS = 256
VMEM_LIMIT_BYTES = 60 * 1024 * 1024

ROW_GROUP = SUBLANES * SUBLANES
TAIL_VREG_ROWS = 3
TAIL_ROWS = TAIL_VREG_ROWS * SUBLANES

TIME_BLOCK = 512
SC_TIME_BLOCK = 1024
SUB_ROWS = 256
SCAN_CHUNK = 128
NEG_BIG = -1e30
STREAM_SLOTS = 3

BF16 = jnp.bfloat16
F32 = jnp.float32


def _rms_norm(x, g):
    ms = jnp.mean(x * x, axis=-1, keepdims=True)
    return x * lax.rsqrt(ms + EPS) * g


def _silu(x):
    hx = 0.5 * x
    return hx * jnp.tanh(hx) + hx


def _softplus(x):
    return jnp.maximum(x, 0.0) + jnp.log1p(jnp.exp(-jnp.abs(x)))


def _dot(a, b):
    return jnp.dot(a, b, preferred_element_type=F32)


def _swap_row_order(v):
    rows = v.shape[0]
    sub = lax.broadcasted_iota(jnp.int32, (SUBLANES, v.shape[1]), 0)
    out = []
    for g in range(rows // ROW_GROUP):
        tiles = [v[g * ROW_GROUP + r * SUBLANES:g * ROW_GROUP + (r + 1) * SUBLANES]
                 for r in range(SUBLANES)]
        for k in (4, 2, 1):
            keep = (sub & k) == 0
            swapped = list(tiles)
            for i in range(SUBLANES):
                if i & k:
                    continue
                lo, hi = tiles[i], tiles[i + k]
                swapped[i] = jnp.where(keep, lo, pltpu.roll(hi, k, axis=0))
                swapped[i + k] = jnp.where(keep, pltpu.roll(lo, SUBLANES - k, axis=0), hi)
            tiles = swapped
        out += tiles
    return jnp.concatenate(out, axis=0)


def _load_rows(x_ref, r0, rows, from_time_order):
    x = x_ref[0, pl.ds(r0, rows), :]
    return _swap_row_order(x) if from_time_order else x


def _store_rows(o_ref, r0, val, to_time_order):
    o_ref[0, pl.ds(r0, val.shape[0]), :] = _swap_row_order(val) if to_time_order else val


def _time_of_position(p):
    return ((p >> 6) << 6) + ((p & 7) << 3) + ((p >> 3) & 7)


def _shifted(cur, prev_tail, k_max):
    rows = cur.shape[0]
    sub = lax.broadcasted_iota(jnp.int32, (SUBLANES, cur.shape[1]), 0)
    last = sub == SUBLANES - 1
    first_q = SUBLANES - TAIL_VREG_ROWS
    out = [[] for _ in range(k_max)]
    for g in range(rows // ROW_GROUP):
        base = g * ROW_GROUP
        prev = prev_tail if g == 0 else cur[base - TAIL_ROWS:base]
        rolled = {}
        for q in range(SUBLANES - k_max, SUBLANES):
            cur_q = cur[base + q * SUBLANES:base + (q + 1) * SUBLANES]
            prev_q = prev[(q - first_q) * SUBLANES:(q - first_q + 1) * SUBLANES]
            rolled[q] = pltpu.roll(jnp.where(last, prev_q, cur_q), 1, axis=0)
        for k in range(1, k_max + 1):
            pieces = [rolled[q] for q in range(SUBLANES - k, SUBLANES)]
            pieces.append(cur[base:base + (SUBLANES - k) * SUBLANES])
            out[k - 1].append(jnp.concatenate(pieces, axis=0))
    return [jnp.concatenate(o, axis=0) for o in out]


def _causal_conv(cur, prev_tail, w, b=None):
    k = w.shape[0]
    out = cur * w[k - 1:k]
    for s, shifted in enumerate(_shifted(cur, prev_tail, k - 1), start=1):
        out = out + shifted * w[k - 1 - s:k - s]
    if b is not None:
        out = out + b
    return out


def _conv_silu(cur, prev_tail, w, b):
    h = _causal_conv(cur, prev_tail, 0.5 * w, 0.5 * b)
    return h * jnp.tanh(h) + h


def _weight_scratch(rows, cols):
    return pltpu.VMEM((rows, cols + LANES), BF16)


def _weight_stage(row_chunk, cols):
    return pltpu.VMEM((STREAM_SLOTS, row_chunk, cols), F32)


def _stream_weight(src, dst_ref, stage_ref, sem):
    rows, cols = src.shape
    row_chunk = stage_ref.shape[1]
    n = rows // row_chunk
    if cols % LANES:
        last_tile = pl.ds(cols // LANES * LANES, LANES)
        dst_ref[:, last_tile] = jnp.zeros((rows, LANES), BF16)

    def copy(i):
        slot = i % STREAM_SLOTS
        return pltpu.make_async_copy(src.at[pl.ds(i * row_chunk, row_chunk), :],
                                     stage_ref.at[slot], sem.at[slot])

    for i in range(min(STREAM_SLOTS - 1, n)):
        copy(i).start()
    for i in range(n):
        ahead = i + STREAM_SLOTS - 1
        if ahead < n:
            copy(ahead).start()
        copy(i).wait()
        dst_ref[pl.ds(i * row_chunk, row_chunk), pl.ds(0, cols)] = (
            stage_ref[i % STREAM_SLOTS].astype(BF16))


def _stream_weight_transposed(src, dst_ref, stage_ref, sem):
    n_out, _ = src.shape
    chunk = stage_ref.shape[1]
    chunks = [(start, min(chunk, n_out - start)) for start in range(0, n_out, chunk)]

    def padded(size):
        return -(-size // LANES) * LANES

    def copy(i):
        start, size = chunks[i]
        slot = i % STREAM_SLOTS
        return pltpu.make_async_copy(src.at[pl.ds(start, size), :],
                                     stage_ref.at[slot, pl.ds(0, size), :], sem.at[slot])

    def start_copy(i):
        size = chunks[i][1]
        if size < padded(size):
            stage_ref[i % STREAM_SLOTS, pl.ds(size, padded(size) - size), :] = jnp.zeros(
                (padded(size) - size, stage_ref.shape[2]), F32)
        copy(i).start()

    for i in range(min(STREAM_SLOTS - 1, len(chunks))):
        start_copy(i)
    for i in range(len(chunks)):
        ahead = i + STREAM_SLOTS - 1
        if ahead < len(chunks):
            start_copy(ahead)
        copy(i).wait()
        start, size = chunks[i]
        cols = padded(size)
        dst_ref[:, pl.ds(start, cols)] = (
            stage_ref[i % STREAM_SLOTS, pl.ds(0, cols), :].T.astype(BF16))


def _is_first_step():
    return jnp.logical_and(pl.program_id(0) == 0, pl.program_id(1) == 0)


_HBM_SPEC = pl.BlockSpec(memory_space=pl.ANY)
_DMA_SEMS = pltpu.SemaphoreType.DMA((STREAM_SLOTS,))


def _const_spec(shape):
    return pl.BlockSpec(shape, lambda b, t: (0,) * len(shape), pipeline_mode=pl.Buffered(1))


def _x_spec(time_block=TIME_BLOCK):
    return pl.BlockSpec((1, time_block, D_MODEL), lambda b, t: (b, t, 0))


def _compiler_params():
    return pltpu.CompilerParams(
        dimension_semantics=("arbitrary", "arbitrary"),
        vmem_limit_bytes=VMEM_LIMIT_BYTES,
    )


def _ffn_kernel(layer, is_last, x_ref, gpre_ref, wup_hbm, cw_ref, cb_ref, wdown_hbm, gpost_ref,
                o_ref, wup_ref, wdown_ref, upstage_ref, downstage_ref, wsem,
                tail_ref, hid_ref):
    t = pl.program_id(1)

    @pl.when(_is_first_step())
    def _():
        _stream_weight(wup_hbm.at[layer], wup_ref, upstage_ref, wsem)
        _stream_weight(wdown_hbm.at[layer], wdown_ref, downstage_ref, wsem)

    @pl.when(t == 0)
    def _():
        tail_ref[...] = jnp.zeros_like(tail_ref)

    n_chunks = FFN_HIDDEN // MXU_COLS
    tails = [tail_ref[:, pl.ds(c * MXU_COLS, MXU_COLS)] for c in range(n_chunks)]
    for sb in range(TIME_BLOCK // SUB_ROWS):
        r0 = sb * SUB_ROWS
        x = _load_rows(x_ref, r0, SUB_ROWS, False)
        hb = _rms_norm(x, gpre_ref[...]).astype(BF16)
        for c in range(n_chunks):
            cols = pl.ds(c * MXU_COLS, MXU_COLS)
            gate = _dot(hb, wup_ref[:, cols])
            val = _dot(hb, wup_ref[:, pl.ds(FFN_HIDDEN + c * MXU_COLS, MXU_COLS)])
            act = _conv_silu(gate, tails[c], cw_ref[:, cols], cb_ref[:, cols])
            tails[c] = gate[SUB_ROWS - TAIL_ROWS:]
            hid_ref[pl.ds(r0, SUB_ROWS), cols] = (act * val).astype(BF16)
        f = _dot(hid_ref[pl.ds(r0, SUB_ROWS), :], wdown_ref[:, pl.ds(0, D_MODEL)])
        _store_rows(o_ref, r0, x + _rms_norm(f, gpost_ref[...]), is_last)
    for c in range(n_chunks):
        tail_ref[:, pl.ds(c * MXU_COLS, MXU_COLS)] = tails[c]


def _ffn_layer(layer, is_last, x, gpre, wup, cw, cb, wdown, gpost):
    bsz, length, _ = x.shape
    return pl.pallas_call(
        functools.partial(_ffn_kernel, layer, is_last),
        name="conv_ffn",
        grid=(bsz, length // TIME_BLOCK),
        in_specs=[
            _x_spec(),
            _const_spec(gpre.shape),
            _HBM_SPEC,
            _const_spec(cw.shape),
            _const_spec(cb.shape),
            _HBM_SPEC,
            _const_spec(gpost.shape),
        ],
        out_specs=_x_spec(),
        out_shape=jax.ShapeDtypeStruct(x.shape, x.dtype),
        scratch_shapes=[
            pltpu.VMEM((D_MODEL, 2 * FFN_HIDDEN), BF16),
            _weight_scratch(FFN_HIDDEN, D_MODEL),
            _weight_stage(64, 2 * FFN_HIDDEN),
            _weight_stage(256, D_MODEL),
            _DMA_SEMS,
            pltpu.VMEM((TAIL_ROWS, FFN_HIDDEN), F32),
            pltpu.VMEM((TIME_BLOCK, FFN_HIDDEN), BF16),
        ],
        compiler_params=_compiler_params(),
    )(x, gpre, wup, cw, cb, wdown, gpost)


def _sc_kernel(layer, x_ref, gpre_ref, win_hbm, cw_ref, wout_hbm, gpost_ref, o_ref,
               win_ref, wout_ref, instage_ref, outstage_ref, wsem,
               tail_ref):
    t = pl.program_id(1)

    @pl.when(_is_first_step())
    def _():
        _stream_weight(win_hbm.at[layer], win_ref, instage_ref, wsem)
        _stream_weight(wout_hbm.at[layer], wout_ref, outstage_ref, wsem)

    @pl.when(t == 0)
    def _():
        tail_ref[...] = jnp.zeros_like(tail_ref)

    n_chunks = SC_WIDTH // MXU_COLS
    tails = [tail_ref[:, pl.ds(c * MXU_COLS, MXU_COLS)] for c in range(n_chunks)]
    for sb in range(SC_TIME_BLOCK // SUB_ROWS):
        r0 = sb * SUB_ROWS
        x = _load_rows(x_ref, r0, SUB_ROWS, False)
        hb = _rms_norm(x, gpre_ref[...]).astype(BF16)
        parts = []
        for c in range(n_chunks):
            cols = pl.ds(c * MXU_COLS, MXU_COLS)
            gb = _dot(hb, win_ref[:, cols])
            gc = _dot(hb, win_ref[:, pl.ds(SC_WIDTH + c * MXU_COLS, MXU_COLS)])
            v = _dot(hb, win_ref[:, pl.ds(2 * SC_WIDTH + c * MXU_COLS, MXU_COLS)])
            gcv = gc * v
            u = _causal_conv(gcv, tails[c], cw_ref[:, cols])
            tails[c] = gcv[SUB_ROWS - TAIL_ROWS:]
            parts.append((gb * u).astype(BF16))
        m = _dot(jnp.concatenate(parts, axis=1), wout_ref[:, pl.ds(0, D_MODEL)])
        _store_rows(o_ref, r0, x + _rms_norm(m, gpost_ref[...]), False)
    for c in range(n_chunks):
        tail_ref[:, pl.ds(c * MXU_COLS, MXU_COLS)] = tails[c]


def _sc_layer(layer, x, gpre, win, cw, wout, gpost):
    bsz, length, _ = x.shape
    return pl.pallas_call(
        functools.partial(_sc_kernel, layer),
        name="shortconv_mixer",
        grid=(bsz, length // SC_TIME_BLOCK),
        in_specs=[
            _x_spec(SC_TIME_BLOCK),
            _const_spec(gpre.shape),
            _HBM_SPEC,
            _const_spec(cw.shape),
            _HBM_SPEC,
            _const_spec(gpost.shape),
        ],
        out_specs=_x_spec(SC_TIME_BLOCK),
        out_shape=jax.ShapeDtypeStruct(x.shape, x.dtype),
        scratch_shapes=[
            _weight_scratch(D_MODEL, 3 * SC_WIDTH),
            _weight_scratch(SC_WIDTH, D_MODEL),
            _weight_stage(128, 3 * SC_WIDTH),
            _weight_stage(256, D_MODEL),
            _DMA_SEMS,
            pltpu.VMEM((TAIL_ROWS, SC_WIDTH), F32),
        ],
        compiler_params=_compiler_params(),
    )(x, gpre, win, cw, wout, gpost)


def _split_bf16(v):
    hi = v.astype(BF16)
    lo = (v - hi.astype(F32)).astype(BF16)
    return jnp.concatenate([hi, lo], axis=1)


def _head_expander():
    k = lax.broadcasted_iota(jnp.int32, (2 * LANES, SSD_D_INNER), 0) % LANES
    c = lax.broadcasted_iota(jnp.int32, (2 * LANES, SSD_D_INNER), 1) // SSD_HEAD_DIM
    return jnp.where(k == c, 1.0, 0.0).astype(BF16)


def _cumsum_time(tril_b, a_c):
    hi = a_c.astype(BF16)
    rest = a_c - hi.astype(F32)
    mid = rest.astype(BF16)
    lo = (rest - mid.astype(F32)).astype(BF16)
    return _dot(jnp.concatenate([tril_b, tril_b, tril_b], axis=1),
                jnp.concatenate([hi, mid, lo], axis=0))


def _ssd_decay_terms(dt_c, acs):
    tc = SCAN_CHUNK
    src_t = (acs - jnp.log(dt_c)).T
    total = acs[tc - 1:tc]
    e_in = jnp.exp(acs)
    w_out = jnp.exp(total - acs) * dt_c
    return src_t, _split_bf16(e_in), _split_bf16(w_out)


def _ssd_scan_chunk(xbc_ref, y_ref, state_ref, dskip_ref, r0, prelude, causal, gcol):
    tc = SCAN_CHUNK
    acs, src_t, e_in_x, w_out_x = prelude

    def start_pair(p):
        rows = pl.ds(r0, tc)
        b_pair = [xbc_ref[rows, pl.ds(SSD_D_INNER + (2 * p + i) * SSD_D_STATE, SSD_D_STATE)]
                  .astype(BF16) for i in range(2)]
        c_pair = xbc_ref[rows, pl.ds(SSD_D_INNER + SSD_BC_DIM + 2 * p * SSD_D_STATE,
                                     2 * SSD_D_STATE)].astype(BF16)
        zero = jnp.zeros_like(b_pair[0])
        b_diag = jnp.concatenate([jnp.concatenate([b_pair[0], zero], axis=1),
                                  jnp.concatenate([zero, b_pair[1]], axis=1)], axis=0)
        cb_pair = lax.dot_general(c_pair, b_diag, (((1,), (1,)), ((), ())),
                                  preferred_element_type=F32)
        return b_pair, c_pair, cb_pair

    def prepare(g, pair):
        b_pair, c_pair, cb_pair = pair
        i = g % 2
        cb_mat = cb_pair[:, i * tc:(i + 1) * tc]
        xg = xbc_ref[pl.ds(r0, tc), pl.ds(g * GROUP_COLS, GROUP_COLS)]
        xg_b = xg.astype(BF16)
        m_parts = []
        x_parts = []
        for j in range(SSD_HEADS_PER_GROUP):
            h = g * SSD_HEADS_PER_GROUP + j
            seg = acs[:, h:h + 1] - src_t[h:h + 1, :]
            lmat = jnp.exp(jnp.where(causal, seg, NEG_BIG))
            m_parts.append((cb_mat * lmat).astype(BF16))
            x_parts.append(jnp.where(gcol == j, xg_b, jnp.zeros_like(xg_b)))
        m_cat = jnp.concatenate(m_parts, axis=1)
        x_bd = jnp.concatenate(x_parts, axis=0)
        return b_pair[i], c_pair[:, i * SSD_D_STATE:(i + 1) * SSD_D_STATE], xg, m_cat, x_bd

    pairs = {0: start_pair(0)}
    prepared = prepare(0, pairs[0])
    for g in range(SSD_N_GROUPS):
        if g % 2 == 0 and g + 2 < SSD_N_GROUPS:
            pairs[g // 2 + 1] = start_pair(g // 2 + 1)
        bg, cg, xg, m_cat, x_bd = prepared
        if g + 1 < SSD_N_GROUPS:
            prepared = prepare(g + 1, pairs[(g + 1) // 2])
        xcols = pl.ds(g * GROUP_COLS, GROUP_COLS)
        y_diag = _dot(m_cat, x_bd)

        e_in_g = e_in_x[:, g * GROUP_COLS:(g + 1) * GROUP_COLS]
        w_out_g = w_out_x[:, g * GROUP_COLS:(g + 1) * GROUP_COLS]
        e_tot_g = e_in_g[tc - 1:tc]
        state = state_ref[g]
        y_off = _dot(cg, state.astype(BF16)) * e_in_g
        y_ref[pl.ds(r0, tc), xcols] = y_diag + y_off + xg * dskip_ref[:, xcols]
        upd = lax.dot_general(bg, (xg * w_out_g).astype(BF16), (((0,), (0,)), ((), ())),
                              preferred_element_type=F32)
        state_ref[g] = state * e_tot_g + upd
        yield


_DONE = object()


def _run(task):
    for _ in task:
        pass


def _chain(*tasks):
    for task in tasks:
        yield from task


def _interleave(primary, secondary, ratio=1.0):
    primary_live = secondary_live = True
    owed = 0.0
    while primary_live or secondary_live:
        if primary_live:
            primary_live = next(primary, _DONE) is not _DONE
        owed = owed + ratio if primary_live else float("inf")
        while secondary_live and owed >= 1.0:
            secondary_live = next(secondary, _DONE) is not _DONE
            owed -= 1.0


def _ssd_kernel(layer, is_first, x_ref, gpre_ref, win_hbm, cw_ref, cb_ref, dtb_ref, alog_ref,
                dskip_ref, normw_ref, wout_hbm, gpost_ref,
                o_ref, wz_ref, wxbc_ref, wout_ref, wstage_ref, wsem,
                expand_ref, tail_ref, state_ref, xbc_ref, y_ref):
    t = pl.program_id(1)

    @pl.when(_is_first_step())
    def _():
        w_in_t = win_hbm.at[layer]
        _stream_weight_transposed(w_in_t.at[pl.ds(SSD_D_INNER, SSD_CONV_DIM + SSD_N_HEADS), :],
                                  wxbc_ref, wstage_ref, wsem)
        _stream_weight_transposed(w_in_t.at[pl.ds(0, SSD_D_INNER), :], wz_ref, wstage_ref, wsem)
        _stream_weight(wout_hbm.at[layer], wout_ref, wstage_ref, wsem)
        expand_ref[...] = _head_expander()

    @pl.when(t == 0)
    def _():
        tail_ref[...] = jnp.zeros_like(tail_ref)
        state_ref[...] = jnp.zeros_like(state_ref)

    tc = SCAN_CHUNK
    time_l = _time_of_position(lax.broadcasted_iota(jnp.int32, (tc, tc), 0))
    time_s = _time_of_position(lax.broadcasted_iota(jnp.int32, (tc, tc), 1))
    causal = time_l >= time_s
    tril_b = jnp.where(causal, 1.0, 0.0).astype(BF16)
    gcol = lax.broadcasted_iota(jnp.int32, (tc, GROUP_COLS), 1) // SSD_HEAD_DIM

    n_chunks = SSD_CONV_DIM // MXU_COLS
    n_sub = TIME_BLOCK // SUB_ROWS
    tails = [tail_ref[:, pl.ds(c * MXU_COLS, MXU_COLS)] for c in range(n_chunks)]
    ctx = [dict() for _ in range(n_sub)]

    def in_proj(sb):
        r0 = sb * SUB_ROWS
        x = _load_rows(x_ref, r0, SUB_ROWS, is_first)
        hb = _rms_norm(x, gpre_ref[...]).astype(BF16)
        ctx[sb].update(x=x, hb=hb)
        dt_raw = _dot(hb, wxbc_ref[:, pl.ds(SSD_CONV_DIM, LANES)])
        dt = _softplus(dt_raw + dtb_ref[...])
        a = dt * -jnp.exp(alog_ref[...])
        chunk_rows = [slice(k * tc, (k + 1) * tc) for k in range(SUB_ROWS // tc)]
        acs, terms, preludes = [], [], []
        for c in range(n_chunks):
            cols = pl.ds(c * MXU_COLS, MXU_COLS)
            raw = _dot(hb, wxbc_ref[:, cols])
            xbc_ref[pl.ds(r0, SUB_ROWS), cols] = _conv_silu(raw, tails[c], cw_ref[:, cols],
                                                            cb_ref[:, cols])
            tails[c] = raw[SUB_ROWS - TAIL_ROWS:]
            if c == 1:
                acs = [_cumsum_time(tril_b, a[rows]) for rows in chunk_rows]
            if c == 5:
                terms = [_ssd_decay_terms(dt[rows], acs_k) for rows, acs_k in zip(chunk_rows, acs)]
            if c == 6:
                preludes = [(acs_k, src_t, _dot(e_in, expand_ref[...]), _dot(w_out, expand_ref[...]))
                            for acs_k, (src_t, e_in, w_out) in zip(acs, terms)]
            yield
        ctx[sb].update(preludes=preludes)

    def scan(sb):
        for k, prelude in enumerate(ctx[sb]["preludes"]):
            yield from _ssd_scan_chunk(xbc_ref, y_ref, state_ref, dskip_ref, sb * SUB_ROWS + k * tc,
                                       prelude, causal, gcol)

    def gate_proj(sb):
        hb = ctx[sb]["hb"]
        gates = []
        for c in range(SSD_D_INNER // MXU_COLS):
            gates.append(_silu(_dot(hb, wz_ref[:, pl.ds(c * MXU_COLS, MXU_COLS)])))
            yield
        ctx[sb].update(gates=gates)

    def out_proj(sb):
        r0 = sb * SUB_ROWS
        sub = pl.ds(r0, SUB_ROWS)
        x = ctx[sb]["x"]
        sumsq = jnp.zeros((SUB_ROWS, 1), F32)
        for c, gate in enumerate(ctx[sb]["gates"]):
            cols = pl.ds(c * MXU_COLS, MXU_COLS)
            gated = y_ref[sub, cols] * gate
            sumsq = sumsq + jnp.sum(gated * gated, axis=-1, keepdims=True)
            y_ref[sub, cols] = gated
        yw = (y_ref[sub, :] * normw_ref[...]).astype(BF16)
        parts = []
        for c in range(D_MODEL // MXU_COLS):
            parts.append(_dot(yw, wout_ref[:, pl.ds(c * MXU_COLS, MXU_COLS)]))
            yield
        m = jnp.concatenate(parts, axis=1) * lax.rsqrt(sumsq * (1.0 / SSD_D_INNER) + EPS)
        _store_rows(o_ref, r0, x + _rms_norm(m, gpost_ref[...]), False)

    _run(in_proj(0))
    for sb in range(n_sub):
        fill, steps = [], 0
        if sb > 0:
            fill.append(out_proj(sb - 1))
            steps += D_MODEL // MXU_COLS
        if sb + 1 < n_sub:
            fill.append(in_proj(sb + 1))
            steps += n_chunks
        fill.append(gate_proj(sb))
        steps += SSD_D_INNER // MXU_COLS
        scan_steps = SUB_ROWS // tc * SSD_N_GROUPS
        _interleave(scan(sb), _chain(*fill), steps / scan_steps)
    _run(out_proj(n_sub - 1))
    for c in range(n_chunks):
        tail_ref[:, pl.ds(c * MXU_COLS, MXU_COLS)] = tails[c]


def _ssd_layer(layer, is_first, x, gpre, win, cw, cb, dtb, alog, dskip, normw, wout, gpost):
    bsz, length, _ = x.shape
    small = (cw, cb, dtb, alog, dskip, normw)
    return pl.pallas_call(
        functools.partial(_ssd_kernel, layer, is_first),
        name="ssd_mixer",
        grid=(bsz, length // TIME_BLOCK),
        in_specs=([_x_spec(), _const_spec(gpre.shape), _HBM_SPEC]
                  + [_const_spec(c.shape) for c in small] + [_HBM_SPEC, _const_spec(gpost.shape)]),
        out_specs=_x_spec(),
        out_shape=jax.ShapeDtypeStruct(x.shape, x.dtype),
        scratch_shapes=[
            _weight_scratch(D_MODEL, SSD_D_INNER),
            _weight_scratch(D_MODEL, SSD_CONV_DIM),
            _weight_scratch(SSD_D_INNER, D_MODEL),
            _weight_stage(256, D_MODEL),
            _DMA_SEMS,
            pltpu.VMEM((2 * LANES, SSD_D_INNER), BF16),
            pltpu.VMEM((TAIL_ROWS, SSD_CONV_DIM), F32),
            pltpu.VMEM((SSD_N_GROUPS, SSD_D_STATE, GROUP_COLS), F32),
            pltpu.VMEM((TIME_BLOCK, SSD_CONV_DIM), F32),
            pltpu.VMEM((TIME_BLOCK, SSD_D_INNER), F32),
        ],
        compiler_params=_compiler_params(),
    )(x, gpre, win, *small, wout, gpost)


def _pad_lanes(v):
    return jnp.pad(v, ((0, 0), (0, LANES - v.shape[-1])))


def kernel(x, mix_pre_g, mix_post_g, ffn_pre_g, ffn_post_g, ssd_w_in, ssd_conv_w, ssd_conv_b,
           ssd_dt_bias, ssd_A_log, ssd_D, ssd_norm_w, ssd_w_out, sc_w_in, sc_conv_w, sc_w_out,
           ffn_w_up, ffn_conv_w, ffn_conv_b, ffn_w_down):
    depth = mix_pre_g.shape[0]
    w_in_t = jnp.swapaxes(ssd_w_in, 1, 2)
    row = lambda v: v.reshape(1, -1)
    for i in range(depth):
        j = i // 2
        if i % 2 == 0:
            dskip = jnp.repeat(ssd_D[j], SSD_HEAD_DIM).reshape(1, -1)
            x = _ssd_layer(j, i == 0, x, row(mix_pre_g[i]), w_in_t, ssd_conv_w[j], row(ssd_conv_b[j]),
                           _pad_lanes(row(ssd_dt_bias[j])), _pad_lanes(row(ssd_A_log[j])), dskip,
                           row(ssd_norm_w[j]), ssd_w_out, row(mix_post_g[i]))
        else:
            x = _sc_layer(j, x, row(mix_pre_g[i]), sc_w_in, sc_conv_w[j], sc_w_out,
                          row(mix_post_g[i]))
        x = _ffn_layer(i, i == depth - 1, x, row(ffn_pre_g[i]), ffn_w_up, ffn_conv_w[i], row(ffn_conv_b[i]),
                       ffn_w_down, row(ffn_post_g[i]))
    return x
```

```python
import functools

import jax
import jax.numpy as jnp
from jax import lax
from jax.experimental import pallas as pl
from jax.experimental.pallas import tpu as pltpu

EPS = 1e-6
D_MODEL = 1024

SSD_D_INNER = 2048
SSD_HEAD_DIM = 64
SSD_N_HEADS = 32
SSD_N_GROUPS = 8
SSD_HEADS_PER_GROUP = 4
SSD_D_STATE = 128
SSD_CONV_W = 4
SSD_BC_DIM = SSD_N_GROUPS * SSD_D_STATE
SSD_CONV_DIM = SSD_D_INNER + 2 * SSD_BC_DIM
GROUP_COLS = SSD_HEADS_PER_GROUP * SSD_HEAD_DIM

SC_WIDTH = 1024
SC_CONV_W = 3
FFN_HIDDEN = 2816
FFN_CONV_W = 3

LANES = 128
SUBLANES = 8
MXU_COLS = 256
VMEM_LIMIT_BYTES = 60 * 1024 * 1024

ROW_GROUP = SUBLANES * SUBLANES
TAIL_VREG_ROWS = 3
TAIL_ROWS = TAIL_VREG_ROWS * SUBLANES

TIME_BLOCK = 512
SC_TIME_BLOCK = 1024
SUB_ROWS = 256
SCAN_CHUNK = 128
NEG_BIG = -1e30
STREAM_SLOTS = 3

BF16 = jnp.bfloat16
F32 = jnp.float32


def _rms_norm(x, g):
    ms = jnp.mean(x * x, axis=-1, keepdims=True)
    return x * lax.rsqrt(ms + EPS) * g


def _silu(x):
    hx = 0.5 * x
    return hx * jnp.tanh(hx) + hx


def _softplus(x):
    return jnp.maximum(x, 0.0) + jnp.log1p(jnp.exp(-jnp.abs(x)))


def _dot(a, b):
    return jnp.dot(a, b, preferred_element_type=F32)


def _swap_row_order(v):
    rows = v.shape[0]
    sub = lax.broadcasted_iota(jnp.int32, (SUBLANES, v.shape[1]), 0)
    out = []
    for g in range(rows // ROW_GROUP):
        tiles = [v[g * ROW_GROUP + r * SUBLANES:g * ROW_GROUP + (r + 1) * SUBLANES]
                 for r in range(SUBLANES)]
        for k in (4, 2, 1):
            keep = (sub & k) == 0
            swapped = list(tiles)
            for i in range(SUBLANES):
                if i & k:
                    continue
                lo, hi = tiles[i], tiles[i + k]
                swapped[i] = jnp.where(keep, lo, pltpu.roll(hi, k, axis=0))
                swapped[i + k] = jnp.where(keep, pltpu.roll(lo, SUBLANES - k, axis=0), hi)
            tiles = swapped
        out += tiles
    return jnp.concatenate(out, axis=0)


def _load_rows(x_ref, r0, rows, from_time_order):
    x = x_ref[0, pl.ds(r0, rows), :]
    return _swap_row_order(x) if from_time_order else x


def _store_rows(o_ref, r0, val, to_time_order):
    o_ref[0, pl.ds(r0, val.shape[0]), :] = _swap_row_order(val) if to_time_order else val


def _time_of_position(p):
    return ((p >> 6) << 6) + ((p & 7) << 3) + ((p >> 3) & 7)


def _shifted(cur, prev_tail, k_max):
    rows = cur.shape[0]
    sub = lax.broadcasted_iota(jnp.int32, (SUBLANES, cur.shape[1]), 0)
    last = sub == SUBLANES - 1
    first_q = SUBLANES - TAIL_VREG_ROWS
    out = [[] for _ in range(k_max)]
    for g in range(rows // ROW_GROUP):
        base = g * ROW_GROUP
        prev = prev_tail if g == 0 else cur[base - TAIL_ROWS:base]
        rolled = {}
        for q in range(SUBLANES - k_max, SUBLANES):
            cur_q = cur[base + q * SUBLANES:base + (q + 1) * SUBLANES]
            prev_q = prev[(q - first_q) * SUBLANES:(q - first_q + 1) * SUBLANES]
            rolled[q] = pltpu.roll(jnp.where(last, prev_q, cur_q), 1, axis=0)
        for k in range(1, k_max + 1):
            pieces = [rolled[q] for q in range(SUBLANES - k, SUBLANES)]
            pieces.append(cur[base:base + (SUBLANES - k) * SUBLANES])
            out[k - 1].append(jnp.concatenate(pieces, axis=0))
    return [jnp.concatenate(o, axis=0) for o in out]


def _causal_conv(cur, prev_tail, w, b=None):
    k = w.shape[0]
    out = cur * w[k - 1:k]
    for s, shifted in enumerate(_shifted(cur, prev_tail, k - 1), start=1):
        out = out + shifted * w[k - 1 - s:k - s]
    if b is not None:
        out = out + b
    return out


def _conv_silu(cur, prev_tail, w, b):
    h = _causal_conv(cur, prev_tail, 0.5 * w, 0.5 * b)
    return h * jnp.tanh(h) + h


def _weight_scratch(rows, cols):
    return pltpu.VMEM((rows, cols + LANES), BF16)


def _weight_stage(row_chunk, cols):
    return pltpu.VMEM((STREAM_SLOTS, row_chunk, cols), F32)


def _stream_weight(src, dst_ref, stage_ref, sem):
    rows, cols = src.shape
    row_chunk = stage_ref.shape[1]
    n = rows // row_chunk
    if cols % LANES:
        last_tile = pl.ds(cols // LANES * LANES, LANES)
        dst_ref[:, last_tile] = jnp.zeros((rows, LANES), BF16)

    def copy(i):
        slot = i % STREAM_SLOTS
        return pltpu.make_async_copy(src.at[pl.ds(i * row_chunk, row_chunk), :],
                                     stage_ref.at[slot], sem.at[slot])

    for i in range(min(STREAM_SLOTS - 1, n)):
        copy(i).start()
    for i in range(n):
        ahead = i + STREAM_SLOTS - 1
        if ahead < n:
            copy(ahead).start()
        copy(i).wait()
        dst_ref[pl.ds(i * row_chunk, row_chunk), pl.ds(0, cols)] = (
            stage_ref[i % STREAM_SLOTS].astype(BF16))


def _stream_weight_transposed(src, dst_ref, stage_ref, sem):
    n_out, _ = src.shape
    chunk = stage_ref.shape[1]
    chunks = [(start, min(chunk, n_out - start)) for start in range(0, n_out, chunk)]

    def padded(size):
        return -(-size // LANES) * LANES

    def copy(i):
        start, size = chunks[i]
        slot = i % STREAM_SLOTS
        return pltpu.make_async_copy(src.at[pl.ds(start, size), :],
                                     stage_ref.at[slot, pl.ds(0, size), :], sem.at[slot])

    def start_copy(i):
        size = chunks[i][1]
        if size < padded(size):
            stage_ref[i % STREAM_SLOTS, pl.ds(size, padded(size) - size), :] = jnp.zeros(
                (padded(size) - size, stage_ref.shape[2]), F32)
        copy(i).start()

    for i in range(min(STREAM_SLOTS - 1, len(chunks))):
        start_copy(i)
    for i in range(len(chunks)):
        ahead = i + STREAM_SLOTS - 1
        if ahead < len(chunks):
            start_copy(ahead)
        copy(i).wait()
        start, size = chunks[i]
        cols = padded(size)
        dst_ref[:, pl.ds(start, cols)] = (
            stage_ref[i % STREAM_SLOTS, pl.ds(0, cols), :].T.astype(BF16))


def _is_first_step():
    return jnp.logical_and(pl.program_id(0) == 0, pl.program_id(1) == 0)


_HBM_SPEC = pl.BlockSpec(memory_space=pl.ANY)
_DMA_SEMS = pltpu.SemaphoreType.DMA((STREAM_SLOTS,))


def _const_spec(shape):
    return pl.BlockSpec(shape, lambda b, t: (0,) * len(shape), pipeline_mode=pl.Buffered(1))


def _x_spec(time_block=TIME_BLOCK):
    return pl.BlockSpec((1, time_block, D_MODEL), lambda b, t: (b, t, 0))


def _compiler_params():
    return pltpu.CompilerParams(
        dimension_semantics=("arbitrary", "arbitrary"),
        vmem_limit_bytes=VMEM_LIMIT_BYTES,
    )


def _ffn_kernel(layer, is_last, x_ref, gpre_ref, wup_hbm, cw_ref, cb_ref, wdown_hbm, gpost_ref,
                o_ref, wup_ref, wdown_ref, upstage_ref, downstage_ref, wsem,
                tail_ref, hid_ref):
    t = pl.program_id(1)

    @pl.when(_is_first_step())
    def _():
        _stream_weight(wup_hbm.at[layer], wup_ref, upstage_ref, wsem)
        _stream_weight(wdown_hbm.at[layer], wdown_ref, downstage_ref, wsem)

    @pl.when(t == 0)
    def _():
        tail_ref[...] = jnp.zeros_like(tail_ref)

    n_chunks = FFN_HIDDEN // MXU_COLS
    tails = [tail_ref[:, pl.ds(c * MXU_COLS, MXU_COLS)] for c in range(n_chunks)]
    for sb in range(TIME_BLOCK // SUB_ROWS):
        r0 = sb * SUB_ROWS
        x = _load_rows(x_ref, r0, SUB_ROWS, False)
        hb = _rms_norm(x, gpre_ref[...]).astype(BF16)
        for c in range(n_chunks):
            cols = pl.ds(c * MXU_COLS, MXU_COLS)
            gate = _dot(hb, wup_ref[:, cols])
            val = _dot(hb, wup_ref[:, pl.ds(FFN_HIDDEN + c * MXU_COLS, MXU_COLS)])
            act = _conv_silu(gate, tails[c], cw_ref[:, cols], cb_ref[:, cols])
            tails[c] = gate[SUB_ROWS - TAIL_ROWS:]
            hid_ref[pl.ds(r0, SUB_ROWS), cols] = (act * val).astype(BF16)
        f = _dot(hid_ref[pl.ds(r0, SUB_ROWS), :], wdown_ref[:, pl.ds(0, D_MODEL)])
        _store_rows(o_ref, r0, x + _rms_norm(f, gpost_ref[...]), is_last)
    for c in range(n_chunks):
        tail_ref[:, pl.ds(c * MXU_COLS, MXU_COLS)] = tails[c]


def _ffn_layer(layer, is_last, x, gpre, wup, cw, cb, wdown, gpost):
    bsz, length, _ = x.shape
    return pl.pallas_call(
        functools.partial(_ffn_kernel, layer, is_last),
        name="conv_ffn",
        grid=(bsz, length // TIME_BLOCK),
        in_specs=[
            _x_spec(),
            _const_spec(gpre.shape),
            _HBM_SPEC,
            _const_spec(cw.shape),
            _const_spec(cb.shape),
            _HBM_SPEC,
            _const_spec(gpost.shape),
        ],
        out_specs=_x_spec(),
        out_shape=jax.ShapeDtypeStruct(x.shape, x.dtype),
        scratch_shapes=[
            pltpu.VMEM((D_MODEL, 2 * FFN_HIDDEN), BF16),
            _weight_scratch(FFN_HIDDEN, D_MODEL),
            _weight_stage(64, 2 * FFN_HIDDEN),
            _weight_stage(256, D_MODEL),
            _DMA_SEMS,
            pltpu.VMEM((TAIL_ROWS, FFN_HIDDEN), F32),
            pltpu.VMEM((TIME_BLOCK, FFN_HIDDEN), BF16),
        ],
        compiler_params=_compiler_params(),
    )(x, gpre, wup, cw, cb, wdown, gpost)


def _sc_kernel(layer, x_ref, gpre_ref, win_hbm, cw_ref, wout_hbm, gpost_ref, o_ref,
               win_ref, wout_ref, instage_ref, outstage_ref, wsem,
               tail_ref):
    t = pl.program_id(1)

    @pl.when(_is_first_step())
    def _():
        _stream_weight(win_hbm.at[layer], win_ref, instage_ref, wsem)
        _stream_weight(wout_hbm.at[layer], wout_ref, outstage_ref, wsem)

    @pl.when(t == 0)
    def _():
        tail_ref[...] = jnp.zeros_like(tail_ref)

    n_chunks = SC_WIDTH // MXU_COLS
    tails = [tail_ref[:, pl.ds(c * MXU_COLS, MXU_COLS)] for c in range(n_chunks)]
    for sb in range(SC_TIME_BLOCK // SUB_ROWS):
        r0 = sb * SUB_ROWS
        x = _load_rows(x_ref, r0, SUB_ROWS, False)
        hb = _rms_norm(x, gpre_ref[...]).astype(BF16)
        parts = []
        for c in range(n_chunks):
            cols = pl.ds(c * MXU_COLS, MXU_COLS)
            gb = _dot(hb, win_ref[:, cols])
            gc = _dot(hb, win_ref[:, pl.ds(SC_WIDTH + c * MXU_COLS, MXU_COLS)])
            v = _dot(hb, win_ref[:, pl.ds(2 * SC_WIDTH + c * MXU_COLS, MXU_COLS)])
            gcv = gc * v
            u = _causal_conv(gcv, tails[c], cw_ref[:, cols])
            tails[c] = gcv[SUB_ROWS - TAIL_ROWS:]
            parts.append((gb * u).astype(BF16))
        m = _dot(jnp.concatenate(parts, axis=1), wout_ref[:, pl.ds(0, D_MODEL)])
        _store_rows(o_ref, r0, x + _rms_norm(m, gpost_ref[...]), False)
    for c in range(n_chunks):
        tail_ref[:, pl.ds(c * MXU_COLS, MXU_COLS)] = tails[c]


def _sc_layer(layer, x, gpre, win, cw, wout, gpost):
    bsz, length, _ = x.shape
    return pl.pallas_call(
        functools.partial(_sc_kernel, layer),
        name="shortconv_mixer",
        grid=(bsz, length // SC_TIME_BLOCK),
        in_specs=[
            _x_spec(SC_TIME_BLOCK),
            _const_spec(gpre.shape),
            _HBM_SPEC,
            _const_spec(cw.shape),
            _HBM_SPEC,
            _const_spec(gpost.shape),
        ],
        out_specs=_x_spec(SC_TIME_BLOCK),
        out_shape=jax.ShapeDtypeStruct(x.shape, x.dtype),
        scratch_shapes=[
            _weight_scratch(D_MODEL, 3 * SC_WIDTH),
            _weight_scratch(SC_WIDTH, D_MODEL),
            _weight_stage(128, 3 * SC_WIDTH),
            _weight_stage(256, D_MODEL),
            _DMA_SEMS,
            pltpu.VMEM((TAIL_ROWS, SC_WIDTH), F32),
        ],
        compiler_params=_compiler_params(),
    )(x, gpre, win, cw, wout, gpost)


def _split_bf16(v):
    hi = v.astype(BF16)
    lo = (v - hi.astype(F32)).astype(BF16)
    return jnp.concatenate([hi, lo], axis=1)


def _head_expander():
    k = lax.broadcasted_iota(jnp.int32, (2 * LANES, SSD_D_INNER), 0) % LANES
    c = lax.broadcasted_iota(jnp.int32, (2 * LANES, SSD_D_INNER), 1) // SSD_HEAD_DIM
    return jnp.where(k == c, 1.0, 0.0).astype(BF16)


def _cumsum_time(tril_b, a_c):
    hi = a_c.astype(BF16)
    rest = a_c - hi.astype(F32)
    mid = rest.astype(BF16)
    lo = (rest - mid.astype(F32)).astype(BF16)
    return _dot(jnp.concatenate([tril_b, tril_b, tril_b], axis=1),
                jnp.concatenate([hi, mid, lo], axis=0))


def _ssd_decay_terms(dt_c, acs):
    tc = SCAN_CHUNK
    src_t = (acs - jnp.log(dt_c)).T
    total = acs[tc - 1:tc]
    e_in = jnp.exp(acs)
    w_out = jnp.exp(total - acs) * dt_c
    return src_t, _split_bf16(e_in), _split_bf16(w_out)


def _ssd_scan_chunk(xbc_ref, y_ref, state_ref, dskip_ref, r0, prelude, causal, gcol):
    tc = SCAN_CHUNK
    acs, src_t, e_in_x, w_out_x = prelude

    def start_pair(p):
        rows = pl.ds(r0, tc)
        b_pair = [xbc_ref[rows, pl.ds(SSD_D_INNER + (2 * p + i) * SSD_D_STATE, SSD_D_STATE)]
                  .astype(BF16) for i in range(2)]
        c_pair = xbc_ref[rows, pl.ds(SSD_D_INNER + SSD_BC_DIM + 2 * p * SSD_D_STATE,
                                     2 * SSD_D_STATE)].astype(BF16)
        zero = jnp.zeros_like(b_pair[0])
        b_diag = jnp.concatenate([jnp.concatenate([b_pair[0], zero], axis=1),
                                  jnp.concatenate([zero, b_pair[1]], axis=1)], axis=0)
        cb_pair = lax.dot_general(c_pair, b_diag, (((1,), (1,)), ((), ())),
                                  preferred_element_type=F32)
        return b_pair, c_pair, cb_pair

    def prepare(g, pair):
        b_pair, c_pair, cb_pair = pair
        i = g % 2
        cb_mat = cb_pair[:, i * tc:(i + 1) * tc]
        xg = xbc_ref[pl.ds(r0, tc), pl.ds(g * GROUP_COLS, GROUP_COLS)]
        xg_b = xg.astype(BF16)
        m_parts = []
        x_parts = []
        for j in range(SSD_HEADS_PER_GROUP):
            h = g * SSD_HEADS_PER_GROUP + j
            seg = acs[:, h:h + 1] - src_t[h:h + 1, :]
            lmat = jnp.exp(jnp.where(causal, seg, NEG_BIG))
            m_parts.append((cb_mat * lmat).astype(BF16))
            x_parts.append(jnp.where(gcol == j, xg_b, jnp.zeros_like(xg_b)))
        m_cat = jnp.concatenate(m_parts, axis=1)
        x_bd = jnp.concatenate(x_parts, axis=0)
        return b_pair[i], c_pair[:, i * SSD_D_STATE:(i + 1) * SSD_D_STATE], xg, m_cat, x_bd

    pairs = {0: start_pair(0), 1: start_pair(1)}
    ready = [prepare(0, pairs[0]), prepare(1, pairs[0])]
    for g in range(SSD_N_GROUPS):
        if g % 2 == 0 and g + 4 < SSD_N_GROUPS:
            pairs[g // 2 + 2] = start_pair(g // 2 + 2)
        bg, cg, xg, m_cat, x_bd = ready.pop(0)
        if g + 2 < SSD_N_GROUPS:
            ready.append(prepare(g + 2, pairs[(g + 2) // 2]))
        xcols = pl.ds(g * GROUP_COLS, GROUP_COLS)
        y_diag = _dot(m_cat, x_bd)

        e_in_g = e_in_x[:, g * GROUP_COLS:(g + 1) * GROUP_COLS]
        w_out_g = w_out_x[:, g * GROUP_COLS:(g + 1) * GROUP_COLS]
        e_tot_g = e_in_g[tc - 1:tc]
        state = state_ref[g]
        y_off = _dot(cg, state.astype(BF16)) * e_in_g
        y_ref[pl.ds(r0, tc), xcols] = y_diag + y_off + xg * dskip_ref[:, xcols]
        upd = lax.dot_general(bg, (xg * w_out_g).astype(BF16), (((0,), (0,)), ((), ())),
                              preferred_element_type=F32)
        state_ref[g] = state * e_tot_g + upd
        yield


_DONE = object()


def _run(task):
    for _ in task:
        pass


def _chain(*tasks):
    for task in tasks:
        yield from task


def _interleave(primary, secondary, ratio=1.0):
    primary_live = secondary_live = True
    owed = 0.0
    while primary_live or secondary_live:
        if primary_live:
            primary_live = next(primary, _DONE) is not _DONE
        owed = owed + ratio if primary_live else float("inf")
        while secondary_live and owed >= 1.0:
            secondary_live = next(secondary, _DONE) is not _DONE
            owed -= 1.0


def _ssd_kernel(layer, is_first, x_ref, gpre_ref, win_hbm, cw_ref, cb_ref, dtb_ref, alog_ref,
                dskip_ref, normw_ref, wout_hbm, gpost_ref,
                o_ref, wz_ref, wxbc_ref, wout_ref, wstage_ref, wsem,
                expand_ref, tail_ref, state_ref, xbc_ref, y_ref):
    t = pl.program_id(1)

    @pl.when(_is_first_step())
    def _():
        w_in_t = win_hbm.at[layer]
        _stream_weight_transposed(w_in_t.at[pl.ds(SSD_D_INNER, SSD_CONV_DIM + SSD_N_HEADS), :],
                                  wxbc_ref, wstage_ref, wsem)
        _stream_weight_transposed(w_in_t.at[pl.ds(0, SSD_D_INNER), :], wz_ref, wstage_ref, wsem)
        _stream_weight(wout_hbm.at[layer], wout_ref, wstage_ref, wsem)
        expand_ref[...] = _head_expander()

    @pl.when(t == 0)
    def _():
        tail_ref[...] = jnp.zeros_like(tail_ref)
        state_ref[...] = jnp.zeros_like(state_ref)

    tc = SCAN_CHUNK
    time_l = _time_of_position(lax.broadcasted_iota(jnp.int32, (tc, tc), 0))
    time_s = _time_of_position(lax.broadcasted_iota(jnp.int32, (tc, tc), 1))
    causal = time_l >= time_s
    tril_b = jnp.where(causal, 1.0, 0.0).astype(BF16)
    gcol = lax.broadcasted_iota(jnp.int32, (tc, GROUP_COLS), 1) // SSD_HEAD_DIM

    n_chunks = SSD_CONV_DIM // MXU_COLS
    n_sub = TIME_BLOCK // SUB_ROWS
    tails = [tail_ref[:, pl.ds(c * MXU_COLS, MXU_COLS)] for c in range(n_chunks)]
    ctx = [dict() for _ in range(n_sub)]

    def in_proj(sb):
        r0 = sb * SUB_ROWS
        x = _load_rows(x_ref, r0, SUB_ROWS, is_first)
        hb = _rms_norm(x, gpre_ref[...]).astype(BF16)
        ctx[sb].update(x=x, hb=hb)
        dt_raw = _dot(hb, wxbc_ref[:, pl.ds(SSD_CONV_DIM, LANES)])
        dt = _softplus(dt_raw + dtb_ref[...])
        a = dt * -jnp.exp(alog_ref[...])
        chunk_rows = [slice(k * tc, (k + 1) * tc) for k in range(SUB_ROWS // tc)]
        acs, terms, preludes = [], [], []
        for c in range(n_chunks):
            cols = pl.ds(c * MXU_COLS, MXU_COLS)
            raw = _dot(hb, wxbc_ref[:, cols])
            xbc_ref[pl.ds(r0, SUB_ROWS), cols] = _conv_silu(raw, tails[c], cw_ref[:, cols],
                                                            cb_ref[:, cols])
            tails[c] = raw[SUB_ROWS - TAIL_ROWS:]
            if c == 1:
                acs = [_cumsum_time(tril_b, a[rows]) for rows in chunk_rows]
            if c == 5:
                terms = [_ssd_decay_terms(dt[rows], acs_k) for rows, acs_k in zip(chunk_rows, acs)]
            if c == 6:
                preludes = [(acs_k, src_t, _dot(e_in, expand_ref[...]), _dot(w_out, expand_ref[...]))
                            for acs_k, (src_t, e_in, w_out) in zip(acs, terms)]
            yield
        ctx[sb].update(preludes=preludes)

    def scan(sb):
        for k, prelude in enumerate(ctx[sb]["preludes"]):
            yield from _ssd_scan_chunk(xbc_ref, y_ref, state_ref, dskip_ref, sb * SUB_ROWS + k * tc,
                                       prelude, causal, gcol)

    def gate_proj(sb):
        hb = ctx[sb]["hb"]
        gates = []
        for c in range(SSD_D_INNER // MXU_COLS):
            gates.append(_silu(_dot(hb, wz_ref[:, pl.ds(c * MXU_COLS, MXU_COLS)])))
            yield
        ctx[sb].update(gates=gates)

    def out_proj(sb):
        r0 = sb * SUB_ROWS
        sub = pl.ds(r0, SUB_ROWS)
        x = ctx[sb]["x"]
        sumsq = jnp.zeros((SUB_ROWS, 1), F32)
        for c, gate in enumerate(ctx[sb]["gates"]):
            cols = pl.ds(c * MXU_COLS, MXU_COLS)
            gated = y_ref[sub, cols] * gate
            sumsq = sumsq + jnp.sum(gated * gated, axis=-1, keepdims=True)
            y_ref[sub, cols] = gated
        yw = (y_ref[sub, :] * normw_ref[...]).astype(BF16)
        parts = []
        for c in range(D_MODEL // MXU_COLS):
            parts.append(_dot(yw, wout_ref[:, pl.ds(c * MXU_COLS, MXU_COLS)]))
            yield
        m = jnp.concatenate(parts, axis=1) * lax.rsqrt(sumsq * (1.0 / SSD_D_INNER) + EPS)
        _store_rows(o_ref, r0, x + _rms_norm(m, gpost_ref[...]), False)

    _run(in_proj(0))
    for sb in range(n_sub):
        fill, steps = [], 0
        if sb > 0:
            fill.append(out_proj(sb - 1))
            steps += D_MODEL // MXU_COLS
        if sb + 1 < n_sub:
            fill.append(in_proj(sb + 1))
            steps += n_chunks
        fill.append(gate_proj(sb))
        steps += SSD_D_INNER // MXU_COLS
        scan_steps = SUB_ROWS // tc * SSD_N_GROUPS
        _interleave(scan(sb), _chain(*fill), steps / scan_steps)
    _run(out_proj(n_sub - 1))
    for c in range(n_chunks):
        tail_ref[:, pl.ds(c * MXU_COLS, MXU_COLS)] = tails[c]


def _ssd_layer(layer, is_first, x, gpre, win, cw, cb, dtb, alog, dskip, normw, wout, gpost):
    bsz, length, _ = x.shape
    small = (cw, cb, dtb, alog, dskip, normw)
    return pl.pallas_call(
        functools.partial(_ssd_kernel, layer, is_first),
        name="ssd_mixer",
        grid=(bsz, length // TIME_BLOCK),
        in_specs=([_x_spec(), _const_spec(gpre.shape), _HBM_SPEC]
                  + [_const_spec(c.shape) for c in small] + [_HBM_SPEC, _const_spec(gpost.shape)]),
        out_specs=_x_spec(),
        out_shape=jax.ShapeDtypeStruct(x.shape, x.dtype),
        scratch_shapes=[
            _weight_scratch(D_MODEL, SSD_D_INNER),
            _weight_scratch(D_MODEL, SSD_CONV_DIM),
            _weight_scratch(SSD_D_INNER, D_MODEL),
            _weight_stage(256, D_MODEL),
            _DMA_SEMS,
            pltpu.VMEM((2 * LANES, SSD_D_INNER), BF16),
            pltpu.VMEM((TAIL_ROWS, SSD_CONV_DIM), F32),
            pltpu.VMEM((SSD_N_GROUPS, SSD_D_STATE, GROUP_COLS), F32),
            pltpu.VMEM((TIME_BLOCK, SSD_CONV_DIM), F32),
            pltpu.VMEM((TIME_BLOCK, SSD_D_INNER), F32),
        ],
        compiler_params=_compiler_params(),
    )(x, gpre, win, *small, wout, gpost)


def _pad_lanes(v):
    return jnp.pad(v, ((0, 0), (0, LANES - v.shape[-1])))


def kernel(x, mix_pre_g, mix_post_g, ffn_pre_g, ffn_post_g, ssd_w_in, ssd_conv_w, ssd_conv_b,
           ssd_dt_bias, ssd_A_log, ssd_D, ssd_norm_w, ssd_w_out, sc_w_in, sc_conv_w, sc_w_out,
           ffn_w_up, ffn_conv_w, ffn_conv_b, ffn_w_down):
    depth = mix_pre_g.shape[0]
    w_in_t = jnp.swapaxes(ssd_w_in, 1, 2)
    row = lambda v: v.reshape(1, -1)
    for i in range(depth):
        j = i // 2
        if i % 2 == 0:
            dskip = jnp.repeat(ssd_D[j], SSD_HEAD_DIM).reshape(1, -1)
            x = _ssd_layer(j, i == 0, x, row(mix_pre_g[i]), w_in_t, ssd_conv_w[j], row(ssd_conv_b[j]),
                           _pad_lanes(row(ssd_dt_bias[j])), _pad_lanes(row(ssd_A_log[j])), dskip,
                           row(ssd_norm_w[j]), ssd_w_out, row(mix_post_g[i]))
        else:
            x = _sc_layer(j, x, row(mix_pre_g[i]), sc_w_in, sc_conv_w[j], sc_w_out,
                          row(mix_post_g[i]))
        x = _ffn_layer(i, i == depth - 1, x, row(ffn_pre_g[i]), ffn_w_up, ffn_conv_w[i], row(ffn_conv_b[i]),
                       ffn_w_down, row(ffn_post_g[i]))
    return x
```

```python
import functools

import jax
import jax.numpy as jnp
from jax import lax
from jax.experimental import pallas as pl
from jax.experimental.pallas import tpu as pltpu

EPS = 1e-6
D_MODEL = 1024

SSD_D_INNER = 2048
SSD_HEAD_DIM = 64
SSD_N_HEADS = 32
SSD_N_GROUPS = 8
SSD_HEADS_PER_GROUP = 4
SSD_D_STATE = 128
SSD_CONV_W = 4
SSD_BC_DIM = SSD_N_GROUPS * SSD_D_STATE
SSD_CONV_DIM = SSD_D_INNER + 2 * SSD_BC_DIM
GROUP_COLS = SSD_HEADS_PER_GROUP * SSD_HEAD_DIM

SC_WIDTH = 1024
SC_CONV_W = 3
FFN_HIDDEN = 2816
FFN_CONV_W = 3

LANES = 128
SUBLANES = 8
MXU_COLS = 256
VMEM_LIMIT_BYTES = 60 * 1024 * 1024

ROW_GROUP = SUBLANES * SUBLANES
TAIL_VREG_ROWS = 3
TAIL_ROWS = TAIL_VREG_ROWS * SUBLANES

TIME_BLOCK = 512
SC_TIME_BLOCK = 1024
SUB_ROWS = 256
SCAN_CHUNK = 128
NEG_BIG = -1e30
STREAM_SLOTS = 3

BF16 = jnp.bfloat16
F32 = jnp.float32


def _rms_norm(x, g):
    ms = jnp.mean(x * x, axis=-1, keepdims=True)
    return x * lax.rsqrt(ms + EPS) * g


def _silu(x):
    hx = 0.5 * x
    return hx * jnp.tanh(hx) + hx


def _softplus(x):
    return jnp.maximum(x, 0.0) + jnp.log1p(jnp.exp(-jnp.abs(x)))


def _dot(a, b):
    return jnp.dot(a, b, preferred_element_type=F32)


def _swap_row_order(v):
    rows = v.shape[0]
    sub = lax.broadcasted_iota(jnp.int32, (SUBLANES, v.shape[1]), 0)
    out = []
    for g in range(rows // ROW_GROUP):
        tiles = [v[g * ROW_GROUP + r * SUBLANES:g * ROW_GROUP + (r + 1) * SUBLANES]
                 for r in range(SUBLANES)]
        for k in (4, 2, 1):
            keep = (sub & k) == 0
            swapped = list(tiles)
            for i in range(SUBLANES):
                if i & k:
                    continue
                lo, hi = tiles[i], tiles[i + k]
                swapped[i] = jnp.where(keep, lo, pltpu.roll(hi, k, axis=0))
                swapped[i + k] = jnp.where(keep, pltpu.roll(lo, SUBLANES - k, axis=0), hi)
            tiles = swapped
        out += tiles
    return jnp.concatenate(out, axis=0)


def _load_rows(x_ref, r0, rows, from_time_order):
    x = x_ref[0, pl.ds(r0, rows), :]
    return _swap_row_order(x) if from_time_order else x


def _store_rows(o_ref, r0, val, to_time_order):
    o_ref[0, pl.ds(r0, val.shape[0]), :] = _swap_row_order(val) if to_time_order else val


def _time_of_position(p):
    return ((p >> 6) << 6) + ((p & 7) << 3) + ((p >> 3) & 7)


def _shifted(cur, prev_tail, k_max):
    rows = cur.shape[0]
    sub = lax.broadcasted_iota(jnp.int32, (SUBLANES, cur.shape[1]), 0)
    last = sub == SUBLANES - 1
    first_q = SUBLANES - TAIL_VREG_ROWS
    out = [[] for _ in range(k_max)]
    for g in range(rows // ROW_GROUP):
        base = g * ROW_GROUP
        prev = prev_tail if g == 0 else cur[base - TAIL_ROWS:base]
        rolled = {}
        for q in range(SUBLANES - k_max, SUBLANES):
            cur_q = cur[base + q * SUBLANES:base + (q + 1) * SUBLANES]
            prev_q = prev[(q - first_q) * SUBLANES:(q - first_q + 1) * SUBLANES]
            rolled[q] = pltpu.roll(jnp.where(last, prev_q, cur_q), 1, axis=0)
        for k in range(1, k_max + 1):
            pieces = [rolled[q] for q in range(SUBLANES - k, SUBLANES)]
            pieces.append(cur[base:base + (SUBLANES - k) * SUBLANES])
            out[k - 1].append(jnp.concatenate(pieces, axis=0))
    return [jnp.concatenate(o, axis=0) for o in out]


def _causal_conv(cur, prev_tail, w, b=None):
    k = w.shape[0]
    out = cur * w[k - 1:k]
    for s, shifted in enumerate(_shifted(cur, prev_tail, k - 1), start=1):
        out = out + shifted * w[k - 1 - s:k - s]
    if b is not None:
        out = out + b
    return out


def _conv_silu(cur, prev_tail, w, b):
    h = _causal_conv(cur, prev_tail, 0.5 * w, 0.5 * b)
    return h * jnp.tanh(h) + h


def _weight_scratch(rows, cols):
    return pltpu.VMEM((rows, cols + LANES), BF16)


def _weight_stage(row_chunk, cols):
    return pltpu.VMEM((STREAM_SLOTS, row_chunk, cols), F32)


def _stream_weight(src, dst_ref, stage_ref, sem):
    rows, cols = src.shape
    row_chunk = stage_ref.shape[1]
    n = rows // row_chunk
    if cols % LANES:
        last_tile = pl.ds(cols // LANES * LANES, LANES)
        dst_ref[:, last_tile] = jnp.zeros((rows, LANES), BF16)

    def copy(i):
        slot = i % STREAM_SLOTS
        return pltpu.make_async_copy(src.at[pl.ds(i * row_chunk, row_chunk), :],
                                     stage_ref.at[slot], sem.at[slot])

    for i in range(min(STREAM_SLOTS - 1, n)):
        copy(i).start()
    for i in range(n):
        ahead = i + STREAM_SLOTS - 1
        if ahead < n:
            copy(ahead).start()
        copy(i).wait()
        dst_ref[pl.ds(i * row_chunk, row_chunk), pl.ds(0, cols)] = (
            stage_ref[i % STREAM_SLOTS].astype(BF16))


def _stream_weight_transposed(src, dst_ref, stage_ref, sem):
    n_out, _ = src.shape
    chunk = stage_ref.shape[1]
    chunks = [(start, min(chunk, n_out - start)) for start in range(0, n_out, chunk)]

    def padded(size):
        return -(-size // LANES) * LANES

    def copy(i):
        start, size = chunks[i]
        slot = i % STREAM_SLOTS
        return pltpu.make_async_copy(src.at[pl.ds(start, size), :],
                                     stage_ref.at[slot, pl.ds(0, size), :], sem.at[slot])

    def start_copy(i):
        size = chunks[i][1]
        if size < padded(size):
            stage_ref[i % STREAM_SLOTS, pl.ds(size, padded(size) - size), :] = jnp.zeros(
                (padded(size) - size, stage_ref.shape[2]), F32)
        copy(i).start()

    for i in range(min(STREAM_SLOTS - 1, len(chunks))):
        start_copy(i)
    for i in range(len(chunks)):
        ahead = i + STREAM_SLOTS - 1
        if ahead < len(chunks):
            start_copy(ahead)
        copy(i).wait()
        start, size = chunks[i]
        cols = padded(size)
        dst_ref[:, pl.ds(start, cols)] = (
            stage_ref[i % STREAM_SLOTS, pl.ds(0, cols), :].T.astype(BF16))


def _is_first_step():
    return jnp.logical_and(pl.program_id(0) == 0, pl.program_id(1) == 0)


_HBM_SPEC = pl.BlockSpec(memory_space=pl.ANY)
_DMA_SEMS = pltpu.SemaphoreType.DMA((STREAM_SLOTS,))


def _const_spec(shape):
    return pl.BlockSpec(shape, lambda b, t: (0,) * len(shape), pipeline_mode=pl.Buffered(1))


def _x_spec(time_block=TIME_BLOCK):
    return pl.BlockSpec((1, time_block, D_MODEL), lambda b, t: (b, t, 0))


def _compiler_params():
    return pltpu.CompilerParams(
        dimension_semantics=("arbitrary", "arbitrary"),
        vmem_limit_bytes=VMEM_LIMIT_BYTES,
    )


def _ffn_kernel(layer, is_last, x_ref, gpre_ref, wup_hbm, cw_ref, cb_ref, wdown_hbm, gpost_ref,
                o_ref, wup_ref, wdown_ref, upstage_ref, downstage_ref, wsem,
                tail_ref, hid_ref):
    t = pl.program_id(1)

    @pl.when(_is_first_step())
    def _():
        _stream_weight(wup_hbm.at[layer], wup_ref, upstage_ref, wsem)
        _stream_weight(wdown_hbm.at[layer], wdown_ref, downstage_ref, wsem)

    @pl.when(t == 0)
    def _():
        tail_ref[...] = jnp.zeros_like(tail_ref)

    n_chunks = FFN_HIDDEN // MXU_COLS
    tails = [tail_ref[:, pl.ds(c * MXU_COLS, MXU_COLS)] for c in range(n_chunks)]
    for sb in range(TIME_BLOCK // SUB_ROWS):
        r0 = sb * SUB_ROWS
        x = _load_rows(x_ref, r0, SUB_ROWS, False)
        hb = _rms_norm(x, gpre_ref[...]).astype(BF16)
        for c in range(n_chunks):
            cols = pl.ds(c * MXU_COLS, MXU_COLS)
            gate = _dot(hb, wup_ref[:, cols])
            val = _dot(hb, wup_ref[:, pl.ds(FFN_HIDDEN + c * MXU_COLS, MXU_COLS)])
            act = _conv_silu(gate, tails[c], cw_ref[:, cols], cb_ref[:, cols])
            tails[c] = gate[SUB_ROWS - TAIL_ROWS:]
            hid_ref[pl.ds(r0, SUB_ROWS), cols] = (act * val).astype(BF16)
        f = _dot(hid_ref[pl.ds(r0, SUB_ROWS), :], wdown_ref[:, pl.ds(0, D_MODEL)])
        _store_rows(o_ref, r0, x + _rms_norm(f, gpost_ref[...]), is_last)
    for c in range(n_chunks):
        tail_ref[:, pl.ds(c * MXU_COLS, MXU_COLS)] = tails[c]


def _ffn_layer(layer, is_last, x, gpre, wup, cw, cb, wdown, gpost):
    bsz, length, _ = x.shape
    return pl.pallas_call(
        functools.partial(_ffn_kernel, layer, is_last),
        name="conv_ffn",
        grid=(bsz, length // TIME_BLOCK),
        in_specs=[
            _x_spec(),
            _const_spec(gpre.shape),
            _HBM_SPEC,
            _const_spec(cw.shape),
            _const_spec(cb.shape),
            _HBM_SPEC,
            _const_spec(gpost.shape),
        ],
        out_specs=_x_spec(),
        out_shape=jax.ShapeDtypeStruct(x.shape, x.dtype),
        scratch_shapes=[
            pltpu.VMEM((D_MODEL, 2 * FFN_HIDDEN), BF16),
            _weight_scratch(FFN_HIDDEN, D_MODEL),
            _weight_stage(64, 2 * FFN_HIDDEN),
            _weight_stage(256, D_MODEL),
            _DMA_SEMS,
            pltpu.VMEM((TAIL_ROWS, FFN_HIDDEN), F32),
            pltpu.VMEM((TIME_BLOCK, FFN_HIDDEN), BF16),
        ],
        compiler_params=_compiler_params(),
    )(x, gpre, wup, cw, cb, wdown, gpost)


def _sc_kernel(layer, x_ref, gpre_ref, win_hbm, cw_ref, wout_hbm, gpost_ref, o_ref,
               win_ref, wout_ref, instage_ref, outstage_ref, wsem,
               tail_ref):
    t = pl.program_id(1)

    @pl.when(_is_first_step())
    def _():
        _stream_weight(win_hbm.at[layer], win_ref, instage_ref, wsem)
        _stream_weight(wout_hbm.at[layer], wout_ref, outstage_ref, wsem)

    @pl.when(t == 0)
    def _():
        tail_ref[...] = jnp.zeros_like(tail_ref)

    n_chunks = SC_WIDTH // MXU_COLS
    tails = [tail_ref[:, pl.ds(c * MXU_COLS, MXU_COLS)] for c in range(n_chunks)]
    for sb in range(SC_TIME_BLOCK // SUB_ROWS):
        r0 = sb * SUB_ROWS
        x = _load_rows(x_ref, r0, SUB_ROWS, False)
        hb = _rms_norm(x, gpre_ref[...]).astype(BF16)
        parts = []
        for c in range(n_chunks):
            cols = pl.ds(c * MXU_COLS, MXU_COLS)
            gb = _dot(hb, win_ref[:, cols])
            gc = _dot(hb, win_ref[:, pl.ds(SC_WIDTH + c * MXU_COLS, MXU_COLS)])
            v = _dot(hb, win_ref[:, pl.ds(2 * SC_WIDTH + c * MXU_COLS, MXU_COLS)])
            gcv = gc * v
            u = _causal_conv(gcv, tails[c], cw_ref[:, cols])
            tails[c] = gcv[SUB_ROWS - TAIL_ROWS:]
            parts.append((gb * u).astype(BF16))
        m = _dot(jnp.concatenate(parts, axis=1), wout_ref[:, pl.ds(0, D_MODEL)])
        _store_rows(o_ref, r0, x + _rms_norm(m, gpost_ref[...]), False)
    for c in range(n_chunks):
        tail_ref[:, pl.ds(c * MXU_COLS, MXU_COLS)] = tails[c]


def _sc_ffn_kernel(sc_layer, ffn_layer, is_last, x_ref, sc_gpre_ref, win_hbm, sc_cw_ref, wout_hbm,
                   sc_gpost_ref, ffn_gpre_ref, wup_hbm, ffn_cw_ref, ffn_cb_ref, wdown_hbm,
                   ffn_gpost_ref, o_ref, win_ref, wout_ref, wup_ref, wdown_ref, instage_ref,
                   upstage_ref, sqstage_ref, wsem, sc_tail_ref, ffn_tail_ref, hid_ref):
    t = pl.program_id(1)

    @pl.when(_is_first_step())
    def _():
        _stream_weight(win_hbm.at[sc_layer], win_ref, instage_ref, wsem)
        _stream_weight(wout_hbm.at[sc_layer], wout_ref, sqstage_ref, wsem)
        _stream_weight(wup_hbm.at[ffn_layer], wup_ref, upstage_ref, wsem)
        _stream_weight(wdown_hbm.at[ffn_layer], wdown_ref, sqstage_ref, wsem)

    @pl.when(t == 0)
    def _():
        sc_tail_ref[...] = jnp.zeros_like(sc_tail_ref)
        ffn_tail_ref[...] = jnp.zeros_like(ffn_tail_ref)

    n_sub = TIME_BLOCK // SUB_ROWS
    sc_chunks = SC_WIDTH // MXU_COLS
    ffn_chunks = FFN_HIDDEN // MXU_COLS
    sc_tails = [sc_tail_ref[:, pl.ds(c * MXU_COLS, MXU_COLS)] for c in range(sc_chunks)]
    ffn_tails = [ffn_tail_ref[:, pl.ds(c * MXU_COLS, MXU_COLS)] for c in range(ffn_chunks)]
    mixed = []
    for sb in range(n_sub):
        x = _load_rows(x_ref, sb * SUB_ROWS, SUB_ROWS, False)
        hb = _rms_norm(x, sc_gpre_ref[...]).astype(BF16)
        parts = []
        for c in range(sc_chunks):
            cols = pl.ds(c * MXU_COLS, MXU_COLS)
            gb = _dot(hb, win_ref[:, cols])
            gc = _dot(hb, win_ref[:, pl.ds(SC_WIDTH + c * MXU_COLS, MXU_COLS)])
            v = _dot(hb, win_ref[:, pl.ds(2 * SC_WIDTH + c * MXU_COLS, MXU_COLS)])
            gcv = gc * v
            u = _causal_conv(gcv, sc_tails[c], sc_cw_ref[:, cols])
            sc_tails[c] = gcv[SUB_ROWS - TAIL_ROWS:]
            parts.append((gb * u).astype(BF16))
        m = _dot(jnp.concatenate(parts, axis=1), wout_ref[:, pl.ds(0, D_MODEL)])
        mixed.append(x + _rms_norm(m, sc_gpost_ref[...]))
    for sb in range(n_sub):
        r0 = sb * SUB_ROWS
        x = mixed[sb]
        hb = _rms_norm(x, ffn_gpre_ref[...]).astype(BF16)
        for c in range(ffn_chunks):
            cols = pl.ds(c * MXU_COLS, MXU_COLS)
            gate = _dot(hb, wup_ref[:, cols])
            val = _dot(hb, wup_ref[:, pl.ds(FFN_HIDDEN + c * MXU_COLS, MXU_COLS)])
            act = _conv_silu(gate, ffn_tails[c], ffn_cw_ref[:, cols], ffn_cb_ref[:, cols])
            ffn_tails[c] = gate[SUB_ROWS - TAIL_ROWS:]
            hid_ref[pl.ds(r0, SUB_ROWS), cols] = (act * val).astype(BF16)
        f = _dot(hid_ref[pl.ds(r0, SUB_ROWS), :], wdown_ref[:, pl.ds(0, D_MODEL)])
        _store_rows(o_ref, r0, x + _rms_norm(f, ffn_gpost_ref[...]), is_last)
    for c in range(sc_chunks):
        sc_tail_ref[:, pl.ds(c * MXU_COLS, MXU_COLS)] = sc_tails[c]
    for c in range(ffn_chunks):
        ffn_tail_ref[:, pl.ds(c * MXU_COLS, MXU_COLS)] = ffn_tails[c]


def _sc_ffn_layer(sc_layer, ffn_layer, is_last, x, sc_gpre, win, sc_cw, wout, sc_gpost,
                  ffn_gpre, wup, ffn_cw, ffn_cb, wdown, ffn_gpost):
    bsz, length, _ = x.shape
    small = lambda a: _const_spec(a.shape)
    return pl.pallas_call(
        functools.partial(_sc_ffn_kernel, sc_layer, ffn_layer, is_last),
        name="shortconv_ffn",
        grid=(bsz, length // TIME_BLOCK),
        in_specs=[_x_spec(), small(sc_gpre), _HBM_SPEC, small(sc_cw), _HBM_SPEC, small(sc_gpost),
                  small(ffn_gpre), _HBM_SPEC, small(ffn_cw), small(ffn_cb), _HBM_SPEC,
                  small(ffn_gpost)],
        out_specs=_x_spec(),
        out_shape=jax.ShapeDtypeStruct(x.shape, x.dtype),
        scratch_shapes=[
            _weight_scratch(D_MODEL, 3 * SC_WIDTH),
            _weight_scratch(SC_WIDTH, D_MODEL),
            pltpu.VMEM((D_MODEL, 2 * FFN_HIDDEN), BF16),
            _weight_scratch(FFN_HIDDEN, D_MODEL),
            _weight_stage(64, 3 * SC_WIDTH),
            _weight_stage(64, 2 * FFN_HIDDEN),
            _weight_stage(256, D_MODEL),
            _DMA_SEMS,
            pltpu.VMEM((TAIL_ROWS, SC_WIDTH), F32),
            pltpu.VMEM((TAIL_ROWS, FFN_HIDDEN), F32),
            pltpu.VMEM((TIME_BLOCK, FFN_HIDDEN), BF16),
        ],
        compiler_params=_compiler_params(),
    )(x, sc_gpre, win, sc_cw, wout, sc_gpost, ffn_gpre, wup, ffn_cw, ffn_cb, wdown, ffn_gpost)


def _sc_layer(layer, x, gpre, win, cw, wout, gpost):
    bsz, length, _ = x.shape
    return pl.pallas_call(
        functools.partial(_sc_kernel, layer),
        name="shortconv_mixer",
        grid=(bsz, length // SC_TIME_BLOCK),
        in_specs=[
            _x_spec(SC_TIME_BLOCK),
            _const_spec(gpre.shape),
            _HBM_SPEC,
            _const_spec(cw.shape),
            _HBM_SPEC,
            _const_spec(gpost.shape),
        ],
        out_specs=_x_spec(SC_TIME_BLOCK),
        out_shape=jax.ShapeDtypeStruct(x.shape, x.dtype),
        scratch_shapes=[
            _weight_scratch(D_MODEL, 3 * SC_WIDTH),
            _weight_scratch(SC_WIDTH, D_MODEL),
            _weight_stage(128, 3 * SC_WIDTH),
            _weight_stage(256, D_MODEL),
            _DMA_SEMS,
            pltpu.VMEM((TAIL_ROWS, SC_WIDTH), F32),
        ],
        compiler_params=_compiler_params(),
    )(x, gpre, win, cw, wout, gpost)


def _split_bf16(v):
    hi = v.astype(BF16)
    lo = (v - hi.astype(F32)).astype(BF16)
    return jnp.concatenate([hi, lo], axis=1)


def _head_expander():
    k = lax.broadcasted_iota(jnp.int32, (2 * LANES, SSD_D_INNER), 0) % LANES
    c = lax.broadcasted_iota(jnp.int32, (2 * LANES, SSD_D_INNER), 1) // SSD_HEAD_DIM
    return jnp.where(k == c, 1.0, 0.0).astype(BF16)


def _cumsum_time(tril_b, a_c):
    hi = a_c.astype(BF16)
    rest = a_c - hi.astype(F32)
    mid = rest.astype(BF16)
    lo = (rest - mid.astype(F32)).astype(BF16)
    return _dot(jnp.concatenate([tril_b, tril_b, tril_b], axis=1),
                jnp.concatenate([hi, mid, lo], axis=0))


def _ssd_decay_terms(dt_c, acs):
    tc = SCAN_CHUNK
    src_t = (acs - jnp.log(dt_c)).T
    total = acs[tc - 1:tc]
    e_in = jnp.exp(acs)
    w_out = jnp.exp(total - acs) * dt_c
    return src_t, _split_bf16(e_in), _split_bf16(w_out)


def _ssd_scan_chunk(xbc_ref, y_ref, state_ref, dskip_ref, r0, prelude, causal, gcol):
    tc = SCAN_CHUNK
    acs, src_t, e_in_x, w_out_x = prelude

    def start_pair(p):
        rows = pl.ds(r0, tc)
        b_pair = [xbc_ref[rows, pl.ds(SSD_D_INNER + (2 * p + i) * SSD_D_STATE, SSD_D_STATE)]
                  .astype(BF16) for i in range(2)]
        c_pair = xbc_ref[rows, pl.ds(SSD_D_INNER + SSD_BC_DIM + 2 * p * SSD_D_STATE,
                                     2 * SSD_D_STATE)].astype(BF16)
        zero = jnp.zeros_like(b_pair[0])
        b_diag = jnp.concatenate([jnp.concatenate([b_pair[0], zero], axis=1),
                                  jnp.concatenate([zero, b_pair[1]], axis=1)], axis=0)
        cb_pair = lax.dot_general(c_pair, b_diag, (((1,), (1,)), ((), ())),
                                  preferred_element_type=F32)
        return b_pair, c_pair, cb_pair

    def prepare(g, pair):
        b_pair, c_pair, cb_pair = pair
        i = g % 2
        cb_mat = cb_pair[:, i * tc:(i + 1) * tc]
        xg = xbc_ref[pl.ds(r0, tc), pl.ds(g * GROUP_COLS, GROUP_COLS)]
        xg_b = xg.astype(BF16)
        m_parts = []
        x_parts = []
        for j in range(SSD_HEADS_PER_GROUP):
            h = g * SSD_HEADS_PER_GROUP + j
            seg = acs[:, h:h + 1] - src_t[h:h + 1, :]
            lmat = jnp.exp(jnp.where(causal, seg, NEG_BIG))
            m_parts.append((cb_mat * lmat).astype(BF16))
            x_parts.append(jnp.where(gcol == j, xg_b, jnp.zeros_like(xg_b)))
        m_cat = jnp.concatenate(m_parts, axis=1)
        x_bd = jnp.concatenate(x_parts, axis=0)
        return b_pair[i], c_pair[:, i * SSD_D_STATE:(i + 1) * SSD_D_STATE], xg, m_cat, x_bd

    pairs = {0: start_pair(0)}
    prepared = prepare(0, pairs[0])
    for g in range(SSD_N_GROUPS):
        if g % 2 == 0 and g + 2 < SSD_N_GROUPS:
            pairs[g // 2 + 1] = start_pair(g // 2 + 1)
        bg, cg, xg, m_cat, x_bd = prepared
        if g + 1 < SSD_N_GROUPS:
            prepared = prepare(g + 1, pairs[(g + 1) // 2])
        xcols = pl.ds(g * GROUP_COLS, GROUP_COLS)
        y_diag = _dot(m_cat, x_bd)

        e_in_g = e_in_x[:, g * GROUP_COLS:(g + 1) * GROUP_COLS]
        w_out_g = w_out_x[:, g * GROUP_COLS:(g + 1) * GROUP_COLS]
        e_tot_g = e_in_g[tc - 1:tc]
        state = state_ref[g]
        y_off = _dot(cg, state.astype(BF16)) * e_in_g
        y_ref[pl.ds(r0, tc), xcols] = y_diag + y_off + xg * dskip_ref[:, xcols]
        upd = lax.dot_general(bg, (xg * w_out_g).astype(BF16), (((0,), (0,)), ((), ())),
                              preferred_element_type=F32)
        state_ref[g] = state * e_tot_g + upd
        yield


_DONE = object()


def _run(task):
    for _ in task:
        pass


def _chain(*tasks):
    for task in tasks:
        yield from task


def _interleave(primary, secondary, ratio=1.0):
    primary_live = secondary_live = True
    owed = 0.0
    while primary_live or secondary_live:
        if primary_live:
            primary_live = next(primary, _DONE) is not _DONE
        owed = owed + ratio if primary_live else float("inf")
        while secondary_live and owed >= 1.0:
            secondary_live = next(secondary, _DONE) is not _DONE
            owed -= 1.0


def _ssd_kernel(layer, is_first, x_ref, gpre_ref, win_hbm, cw_ref, cb_ref, dtb_ref, alog_ref,
                dskip_ref, normw_ref, wout_hbm, gpost_ref,
                o_ref, wz_ref, wxbc_ref, wout_ref, wstage_ref, wsem,
                expand_ref, tail_ref, state_ref, xbc_ref, y_ref):
    t = pl.program_id(1)

    @pl.when(_is_first_step())
    def _():
        w_in_t = win_hbm.at[layer]
        _stream_weight_transposed(w_in_t.at[pl.ds(SSD_D_INNER, SSD_CONV_DIM + SSD_N_HEADS), :],
                                  wxbc_ref, wstage_ref, wsem)
        _stream_weight_transposed(w_in_t.at[pl.ds(0, SSD_D_INNER), :], wz_ref, wstage_ref, wsem)
        _stream_weight(wout_hbm.at[layer], wout_ref, wstage_ref, wsem)
        expand_ref[...] = _head_expander()

    @pl.when(t == 0)
    def _():
        tail_ref[...] = jnp.zeros_like(tail_ref)
        state_ref[...] = jnp.zeros_like(state_ref)

    tc = SCAN_CHUNK
    time_l = _time_of_position(lax.broadcasted_iota(jnp.int32, (tc, tc), 0))
    time_s = _time_of_position(lax.broadcasted_iota(jnp.int32, (tc, tc), 1))
    causal = time_l >= time_s
    tril_b = jnp.where(causal, 1.0, 0.0).astype(BF16)
    gcol = lax.broadcasted_iota(jnp.int32, (tc, GROUP_COLS), 1) // SSD_HEAD_DIM

    n_chunks = SSD_CONV_DIM // MXU_COLS
    n_sub = TIME_BLOCK // SUB_ROWS
    tails = [tail_ref[:, pl.ds(c * MXU_COLS, MXU_COLS)] for c in range(n_chunks)]
    ctx = [dict() for _ in range(n_sub)]

    def in_proj(sb):
        r0 = sb * SUB_ROWS
        x = _load_rows(x_ref, r0, SUB_ROWS, is_first)
        hb = _rms_norm(x, gpre_ref[...]).astype(BF16)
        ctx[sb].update(x=x, hb=hb)
        dt_raw = _dot(hb, wxbc_ref[:, pl.ds(SSD_CONV_DIM, LANES)])
        dt = _softplus(dt_raw + dtb_ref[...])
        a = dt * -jnp.exp(alog_ref[...])
        chunk_rows = [slice(k * tc, (k + 1) * tc) for k in range(SUB_ROWS // tc)]
        acs, terms, preludes = [], [], []
        for c in range(n_chunks):
            cols = pl.ds(c * MXU_COLS, MXU_COLS)
            raw = _dot(hb, wxbc_ref[:, cols])
            xbc_ref[pl.ds(r0, SUB_ROWS), cols] = _conv_silu(raw, tails[c], cw_ref[:, cols],
                                                            cb_ref[:, cols])
            tails[c] = raw[SUB_ROWS - TAIL_ROWS:]
            if c == 1:
                acs = [_cumsum_time(tril_b, a[rows]) for rows in chunk_rows]
            if c == 5:
                terms = [_ssd_decay_terms(dt[rows], acs_k) for rows, acs_k in zip(chunk_rows, acs)]
            if c == 6:
                preludes = [(acs_k, src_t, _dot(e_in, expand_ref[...]), _dot(w_out, expand_ref[...]))
                            for acs_k, (src_t, e_in, w_out) in zip(acs, terms)]
            yield
        ctx[sb].update(preludes=preludes)

    def scan(sb):
        for k, prelude in enumerate(ctx[sb]["preludes"]):
            yield from _ssd_scan_chunk(xbc_ref, y_ref, state_ref, dskip_ref, sb * SUB_ROWS + k * tc,
                                       prelude, causal, gcol)

    def gate_proj(sb):
        hb = ctx[sb]["hb"]
        gates = []
        for c in range(SSD_D_INNER // MXU_COLS):
            gates.append(_silu(_dot(hb, wz_ref[:, pl.ds(c * MXU_COLS, MXU_COLS)])))
            yield
        ctx[sb].update(gates=gates)

    def out_proj(sb):
        r0 = sb * SUB_ROWS
        sub = pl.ds(r0, SUB_ROWS)
        x = ctx[sb]["x"]
        sumsq = jnp.zeros((SUB_ROWS, 1), F32)
        for c, gate in enumerate(ctx[sb]["gates"]):
            cols = pl.ds(c * MXU_COLS, MXU_COLS)
            gated = y_ref[sub, cols] * gate
            sumsq = sumsq + jnp.sum(gated * gated, axis=-1, keepdims=True)
            y_ref[sub, cols] = gated
        yw = (y_ref[sub, :] * normw_ref[...]).astype(BF16)
        parts = []
        for c in range(D_MODEL // MXU_COLS):
            parts.append(_dot(yw, wout_ref[:, pl.ds(c * MXU_COLS, MXU_COLS)]))
            yield
        m = jnp.concatenate(parts, axis=1) * lax.rsqrt(sumsq * (1.0 / SSD_D_INNER) + EPS)
        _store_rows(o_ref, r0, x + _rms_norm(m, gpost_ref[...]), False)

    _run(in_proj(0))
    for sb in range(n_sub):
        fill, steps = [], 0
        if sb > 0:
            fill.append(out_proj(sb - 1))
            steps += D_MODEL // MXU_COLS
        if sb + 1 < n_sub:
            fill.append(in_proj(sb + 1))
            steps += n_chunks
        fill.append(gate_proj(sb))
        steps += SSD_D_INNER // MXU_COLS
        scan_steps = SUB_ROWS // tc * SSD_N_GROUPS
        _interleave(scan(sb), _chain(*fill), steps / scan_steps)
    _run(out_proj(n_sub - 1))
    for c in range(n_chunks):
        tail_ref[:, pl.ds(c * MXU_COLS, MXU_COLS)] = tails[c]


def _ssd_layer(layer, is_first, x, gpre, win, cw, cb, dtb, alog, dskip, normw, wout, gpost):
    bsz, length, _ = x.shape
    small = (cw, cb, dtb, alog, dskip, normw)
    return pl.pallas_call(
        functools.partial(_ssd_kernel, layer, is_first),
        name="ssd_mixer",
        grid=(bsz, length // TIME_BLOCK),
        in_specs=([_x_spec(), _const_spec(gpre.shape), _HBM_SPEC]
                  + [_const_spec(c.shape) for c in small] + [_HBM_SPEC, _const_spec(gpost.shape)]),
        out_specs=_x_spec(),
        out_shape=jax.ShapeDtypeStruct(x.shape, x.dtype),
        scratch_shapes=[
            _weight_scratch(D_MODEL, SSD_D_INNER),
            _weight_scratch(D_MODEL, SSD_CONV_DIM),
            _weight_scratch(SSD_D_INNER, D_MODEL),
            _weight_stage(256, D_MODEL),
            _DMA_SEMS,
            pltpu.VMEM((2 * LANES, SSD_D_INNER), BF16),
            pltpu.VMEM((TAIL_ROWS, SSD_CONV_DIM), F32),
            pltpu.VMEM((SSD_N_GROUPS, SSD_D_STATE, GROUP_COLS), F32),
            pltpu.VMEM((TIME_BLOCK, SSD_CONV_DIM), F32),
            pltpu.VMEM((TIME_BLOCK, SSD_D_INNER), F32),
        ],
        compiler_params=_compiler_params(),
    )(x, gpre, win, *small, wout, gpost)


def _pad_lanes(v):
    return jnp.pad(v, ((0, 0), (0, LANES - v.shape[-1])))


def kernel(x, mix_pre_g, mix_post_g, ffn_pre_g, ffn_post_g, ssd_w_in, ssd_conv_w, ssd_conv_b,
           ssd_dt_bias, ssd_A_log, ssd_D, ssd_norm_w, ssd_w_out, sc_w_in, sc_conv_w, sc_w_out,
           ffn_w_up, ffn_conv_w, ffn_conv_b, ffn_w_down):
    depth = mix_pre_g.shape[0]
    w_in_t = jnp.swapaxes(ssd_w_in, 1, 2)
    row = lambda v: v.reshape(1, -1)
    for i in range(depth):
        j = i // 2
        if i % 2 == 0:
            dskip = jnp.repeat(ssd_D[j], SSD_HEAD_DIM).reshape(1, -1)
            x = _ssd_layer(j, i == 0, x, row(mix_pre_g[i]), w_in_t, ssd_conv_w[j], row(ssd_conv_b[j]),
                           _pad_lanes(row(ssd_dt_bias[j])), _pad_lanes(row(ssd_A_log[j])), dskip,
                           row(ssd_norm_w[j]), ssd_w_out, row(mix_post_g[i]))
        else:
            x = _sc_ffn_layer(j, i, i == depth - 1, x, row(mix_pre_g[i]), sc_w_in, sc_conv_w[j],
                              sc_w_out, row(mix_post_g[i]), row(ffn_pre_g[i]), ffn_w_up,
                              ffn_conv_w[i], row(ffn_conv_b[i]), ffn_w_down, row(ffn_post_g[i]))
            continue
        x = _ffn_layer(i, i == depth - 1, x, row(ffn_pre_g[i]), ffn_w_up, ffn_conv_w[i], row(ffn_conv_b[i]),
                       ffn_w_down, row(ffn_post_g[i]))
    return x
```
